```python
import jax, jax.numpy as jnp
from jax import lax
import numpy as np

D_MODEL = 1024
BATCH = 4
SEQ = 8192
DEPTH = 1

HEAD_DIM = 64
SWA_Q_HEADS = 8
SWA_KV_HEADS = 2
SWA_GROUP = SWA_Q_HEADS // SWA_KV_HEADS
SWA_WIDTH = SWA_Q_HEADS * HEAD_DIM
SWA_KV_WIDTH = SWA_KV_HEADS * HEAD_DIM
FOX_HEADS = 8
FOX_WIDTH = FOX_HEADS * HEAD_DIM
WINDOW = 128
BLOCK = 128
ROPE_THETA = 10000.0
NORM_EPS = 1e-6
FORGET_BIAS_INIT = 2.0

IN_SPLITS = (SWA_WIDTH, SWA_KV_WIDTH, SWA_KV_WIDTH, SWA_WIDTH,
             FOX_WIDTH, FOX_WIDTH, FOX_WIDTH, FOX_HEADS, FOX_WIDTH,
             D_MODEL, D_MODEL)
IN_WIDTH = sum(IN_SPLITS)

kernel_name = "hybrid_swa_sink_fox_gated_block"


def rms_norm(x, g):
    xf = x.astype(jnp.float32)
    y = xf * lax.rsqrt(jnp.mean(xf * xf, axis=-1, keepdims=True) + NORM_EPS)
    return (y * g.astype(jnp.float32)).astype(x.dtype)


def rope_tables(positions):
    inv_freq = ROPE_THETA ** (-jnp.arange(0, HEAD_DIM, 2, dtype=jnp.float32) / HEAD_DIM)
    ang = positions.astype(jnp.float32)[..., None] * inv_freq
    ang = jnp.concatenate([ang, ang], axis=-1)[:, :, None, :]
    return jnp.cos(ang), jnp.sin(ang)


def apply_rope(t, cos, sin):
    half = HEAD_DIM // 2
    tf = t.astype(jnp.float32)
    rot = jnp.concatenate([-tf[..., half:], tf[..., :half]], axis=-1)
    return (tf * cos + rot * sin).astype(t.dtype)


def split_cols(p, sizes):
    offs = np.cumsum(sizes)[:-1].tolist()
    return jnp.split(p, offs, axis=-1)


def sliding_window_gqa_sinks(q, k, v, sinks):
    b, s = q.shape[:2]
    nb = s // BLOCK
    q = q.reshape(b, nb, BLOCK, SWA_KV_HEADS, SWA_GROUP, HEAD_DIM)
    k = k.reshape(b, nb, BLOCK, SWA_KV_HEADS, HEAD_DIM)
    v = v.reshape(b, nb, BLOCK, SWA_KV_HEADS, HEAD_DIM)
    pad_k = jnp.zeros_like(k[:, :1])
    pad_v = jnp.zeros_like(v[:, :1])
    k_band = jnp.concatenate([jnp.concatenate([pad_k, k[:, :-1]], axis=1), k], axis=2)
    v_band = jnp.concatenate([jnp.concatenate([pad_v, v[:, :-1]], axis=1), v], axis=2)
    logits = jnp.einsum('bnqkgd,bnskd->bnkgqs', q, k_band).astype(jnp.float32) * (HEAD_DIM ** -0.5)
    blk = jnp.arange(nb)[:, None, None]
    qpos = blk * BLOCK + jnp.arange(BLOCK)[None, :, None]
    kpos = (blk - 1) * BLOCK + jnp.arange(2 * BLOCK)[None, None, :]
    rel = qpos - kpos
    mask = (rel >= 0) & (rel < WINDOW) & (kpos >= 0)
    logits = jnp.where(mask[None, :, None, None], logits, -jnp.inf)
    sink = sinks.astype(jnp.float32).reshape(SWA_KV_HEADS, SWA_GROUP)[None, None, :, :, None, None]
    m = jnp.maximum(jnp.max(logits, axis=-1, keepdims=True), sink)
    p = jnp.exp(logits - m)
    denom = jnp.sum(p, axis=-1, keepdims=True) + jnp.exp(sink - m)
    out = jnp.einsum('bnkgqs,bnskd->bnqkgd', (p / denom).astype(v.dtype), v_band)
    return out.reshape(b, s, SWA_WIDTH)


def forgetting_attention(q, k, v, cum):
    b, s = q.shape[:2]
    nb = s // BLOCK
    q_blocks = q.reshape(b, nb, BLOCK, FOX_HEADS, HEAD_DIM).transpose(1, 0, 2, 3, 4)
    cq_blocks = cum.reshape(b, nb, BLOCK, FOX_HEADS).transpose(1, 0, 3, 2)
    ck = cum.transpose(0, 2, 1)[:, :, None, :]
    kpos = jnp.arange(s)
    scale = HEAD_DIM ** -0.5

    def one_block(args):
        qb, cqb, n = args
        logits = jnp.einsum('bqhd,bshd->bhqs', qb, k).astype(jnp.float32) * scale
        logits = logits + cqb[..., None] - ck
        qpos = n * BLOCK + jnp.arange(BLOCK)
        mask = kpos[None, :] <= qpos[:, None]
        logits = jnp.where(mask, logits, -jnp.inf)
        p = jax.nn.softmax(logits, axis=-1)
        return jnp.einsum('bhqs,bshd->bqhd', p.astype(v.dtype), v)

    out = lax.map(one_block, (q_blocks, cq_blocks, jnp.arange(nb)))
    return out.transpose(1, 0, 2, 3, 4).reshape(b, s, FOX_WIDTH)


def setup_inputs(seed: int = 0) -> dict:
    key = jax.random.key(seed)
    ks = jax.random.split(key, 14)
    f32 = jnp.float32
    x = jax.random.normal(ks[0], (BATCH, SEQ, D_MODEL), f32)
    c = jax.random.normal(ks[1], (BATCH, D_MODEL), f32)
    positions = jnp.broadcast_to(jnp.arange(SEQ, dtype=jnp.int32), (BATCH, SEQ))
    w_ada = jax.random.normal(ks[2], (DEPTH, D_MODEL, 3 * D_MODEL), f32) * (0.5 * D_MODEL ** -0.5)
    b_ada = jax.random.normal(ks[3], (DEPTH, 3 * D_MODEL), f32) * 0.02
    g_norm = 1.0 + 0.05 * jax.random.normal(ks[4], (DEPTH, D_MODEL), f32)
    w_in = jax.random.normal(ks[5], (DEPTH, D_MODEL, IN_WIDTH), f32) * D_MODEL ** -0.5
    b_f = FORGET_BIAS_INIT + 0.1 * jax.random.normal(ks[6], (DEPTH, FOX_HEADS), f32)
    sinks = jax.random.normal(ks[7], (DEPTH, SWA_Q_HEADS), f32) * 0.5
    w_o_swa = jax.random.normal(ks[8], (DEPTH, SWA_WIDTH, D_MODEL), f32) * SWA_WIDTH ** -0.5
    w_o_fox = jax.random.normal(ks[9], (DEPTH, FOX_WIDTH, D_MODEL), f32) * FOX_WIDTH ** -0.5
    w_out = jax.random.normal(ks[10], (DEPTH, D_MODEL, D_MODEL), f32) * D_MODEL ** -0.5
    g_final = 1.0 + 0.05 * jax.random.normal(ks[11], (D_MODEL,), f32)
    return {"x": x, "c": c, "positions": positions, "w_ada": w_ada, "b_ada": b_ada,
            "g_norm": g_norm, "w_in": w_in, "b_f": b_f, "sinks": sinks,
            "w_o_swa": w_o_swa, "w_o_fox": w_o_fox, "w_out": w_out, "g_final": g_final}


def reference(x, c, positions, w_ada, b_ada, g_norm, w_in, b_f, sinks, w_o_swa, w_o_fox, w_out, g_final):
    b, s, _ = x.shape
    cos, sin = rope_tables(positions)
    for l in range(DEPTH):
        ada = c @ w_ada[l] + b_ada[l]
        shift, scale, gate = jnp.split(ada, 3, axis=-1)
        h = rms_norm(x, g_norm[l]) * (1.0 + scale[:, None, :]) + shift[:, None, :]
        proj = h @ w_in[l]
        qa, ka, va, za, qb, kb, vb, fb, zb, ga, gb = split_cols(proj, IN_SPLITS)
        qa = apply_rope(qa.reshape(b, s, SWA_Q_HEADS, HEAD_DIM), cos, sin)
        ka = apply_rope(ka.reshape(b, s, SWA_KV_HEADS, HEAD_DIM), cos, sin)
        va = va.reshape(b, s, SWA_KV_HEADS, HEAD_DIM)
        att_a = sliding_window_gqa_sinks(qa, ka, va, sinks[l])
        y_a = (att_a * jax.nn.silu(za)) @ w_o_swa[l]
        log_f = jax.nn.log_sigmoid(fb.astype(jnp.float32) + b_f[l].astype(jnp.float32))
        cum = jnp.cumsum(log_f, axis=1)
        att_b = forgetting_attention(qb.reshape(b, s, FOX_HEADS, HEAD_DIM),
                                     kb.reshape(b, s, FOX_HEADS, HEAD_DIM),
                                     vb.reshape(b, s, FOX_HEADS, HEAD_DIM), cum)
        y_b = (att_b * jax.nn.silu(zb)) @ w_o_fox[l]
        merged = jax.nn.sigmoid(ga) * y_a + jax.nn.sigmoid(gb) * y_b
        x = x + gate[:, None, :] * (merged @ w_out[l])
    return rms_norm(x, g_final)
```

```python
import functools
import math

import numpy as np
import jax
import jax.numpy as jnp
from jax import lax
from jax.experimental import pallas as pl
from jax.experimental.pallas import tpu as pltpu

D_MODEL = 1024
HEAD_DIM = 64
HALF = HEAD_DIM // 2
SWA_Q_HEADS = 8
SWA_KV_HEADS = 2
SWA_WIDTH = SWA_Q_HEADS * HEAD_DIM
SWA_KV_WIDTH = SWA_KV_HEADS * HEAD_DIM
FOX_HEADS = 8
FOX_WIDTH = FOX_HEADS * HEAD_DIM
WINDOW = 128
ROPE_THETA = 10000.0
NORM_EPS = 1e-6
QK_SCALE = HEAD_DIM ** -0.5

LANES = 128
V7X_VMEM_BYTES = 64 * 1024 * 1024

F32 = jnp.float32
BF16 = jnp.bfloat16

_QA, _KA, _VA, _ZA = 0, 1024, 1536, 1792
_QB, _KB, _VB, _FB, _ZB, _GA, _GB = 2304, 2816, 3328, 3840, 3968, 4480, 5504
_PROJ_COLS = 6528

_ONE_LANE = 24


def _vmem_limit(nbytes):
    return int(min(nbytes, V7X_VMEM_BYTES - 8 * 1024 * 1024))


def _split3(v):
    hi = v.astype(BF16)
    r1 = v - hi.astype(F32)
    mid = r1.astype(BF16)
    lo = (r1 - mid.astype(F32)).astype(BF16)
    return hi, mid, lo


def _ada_kernel(c_ref, w_ref, b_ref, o_ref):
    o_ref[...] = jnp.dot(c_ref[...], w_ref[...], preferred_element_type=F32,
                         precision=lax.Precision.HIGHEST) + b_ref[...]


def _ada(c_pad, w_ada, b_ada):
    rows = c_pad.shape[0]
    n = w_ada.shape[1]
    nblk = n // D_MODEL
    return pl.pallas_call(
        _ada_kernel,
        grid=(nblk,),
        in_specs=[pl.BlockSpec((rows, D_MODEL), lambda j: (0, 0)),
                  pl.BlockSpec((D_MODEL, D_MODEL), lambda j: (0, j)),
                  pl.BlockSpec((1, D_MODEL), lambda j: (0, j))],
        out_specs=pl.BlockSpec((rows, D_MODEL), lambda j: (0, j)),
        out_shape=jax.ShapeDtypeStruct((rows, n), F32),
        name="ada_mod",
    )(c_pad, w_ada, b_ada)


def _sigmoid(z):
    return 1.0 / (1.0 + jnp.exp(-z))


def _proj_kernel(x_ref, pos_ref, scale_ref, shift_ref, gn_ref, invf_ref, bf_ref, w_ref, e_ref,
                 qa_ref, ka_ref, va_ref, sza_ref, qaug_ref, kaug_ref, vb_ref, szb_ref,
                 sga_ref, sgb_ref, carry_ref, *, tm):
    i = pl.program_id(1)

    @pl.when(i == 0)
    def _():
        carry_ref[...] = jnp.zeros_like(carry_ref)

    x = x_ref[0]
    ms = jnp.mean(x * x, axis=-1, keepdims=True)
    y = x * lax.rsqrt(ms + NORM_EPS) * gn_ref[...]
    h = y * (1.0 + scale_ref[0]) + shift_ref[0]
    hb = h.astype(BF16)

    def proj(off, n):
        return jnp.dot(hb, w_ref[:, off:off + n], preferred_element_type=F32)

    lane = lax.broadcasted_iota(jnp.int32, (tm, LANES), 1)
    low = lane < HEAD_DIM

    ang = pos_ref[0] * invf_ref[...]
    cos = jnp.cos(ang)
    sin = jnp.sin(ang)

    r = proj(_QA, 2 * SWA_WIDTH)
    for c in range(SWA_WIDTH // LANES):
        a = c * LANES
        t = r[:, a:a + LANES] * cos + r[:, SWA_WIDTH + a:SWA_WIDTH + a + LANES] * sin
        qa_ref[0, :, a:a + LANES] = (t * QK_SCALE).astype(BF16)
    r = proj(_KA, 4 * SWA_KV_WIDTH)
    for c in range(2 * SWA_KV_WIDTH // LANES):
        a = c * LANES
        t = r[:, a:a + LANES] * cos + r[:, 2 * SWA_KV_WIDTH + a:2 * SWA_KV_WIDTH + a + LANES] * sin
        ka_ref[0, :, a:a + LANES] = t.astype(BF16)
    va_ref[0] = proj(_VA, 2 * SWA_KV_WIDTH).astype(BF16)
    z = proj(_ZA, SWA_WIDTH)
    sza_ref[0] = (z * _sigmoid(z)).astype(BF16)

    vb_ref[0] = proj(_VB, FOX_WIDTH).astype(BF16)
    z = proj(_ZB, FOX_WIDTH)
    szb_ref[0] = (z * _sigmoid(z)).astype(BF16)
    sga_ref[0] = _sigmoid(proj(_GA, D_MODEL)).astype(BF16)
    sgb_ref[0] = _sigmoid(proj(_GB, D_MODEL)).astype(BF16)

    zf = proj(_FB, LANES) + bf_ref[...]
    logf = jnp.minimum(zf, 0.0) - jnp.log1p(jnp.exp(-jnp.abs(zf)))
    grp = lane // FOX_HEADS
    hi, mid, lo = _split3(logf)
    zero = jnp.zeros((tm, LANES), F32)
    parts = jnp.where(grp == 0, hi.astype(F32),
                      jnp.where(grp == 1, mid.astype(F32),
                                jnp.where(grp == 2, lo.astype(F32), zero))).astype(BF16)
    tri = (lax.broadcasted_iota(jnp.int32, (tm, tm), 0)
           >= lax.broadcasted_iota(jnp.int32, (tm, tm), 1)).astype(BF16)
    rsum = jnp.dot(tri, parts, preferred_element_type=F32)
    c0 = rsum + pltpu.roll(rsum, LANES - FOX_HEADS, 1) + pltpu.roll(rsum, LANES - 2 * FOX_HEADS, 1)
    c0 = jnp.where(grp == 0, c0, zero)
    cl = c0 + pltpu.roll(c0, FOX_HEADS, 1) + pltpu.roll(c0, 2 * FOX_HEADS, 1)
    cum = cl + carry_ref[...]
    carry_ref[...] = cum[tm - 1:tm, :]

    hi, mid, lo = _split3(cum)
    one = jnp.ones((tm, LANES), F32)
    carrier = jnp.where(grp == 0, hi.astype(F32),
                        jnp.where(grp == 1, mid.astype(F32),
                                  jnp.where(grp == 2, lo.astype(F32),
                                            jnp.where(lane == _ONE_LANE, one, zero)))).astype(BF16)
    aug = jnp.dot(carrier, e_ref[...], preferred_element_type=F32)

    rq = proj(_QB, FOX_WIDTH)
    rk = proj(_KB, FOX_WIDTH)
    for hd in range(FOX_HEADS):
        a = (hd // 2) * LANES
        gq = aug[:, a:a + LANES]
        gk = aug[:, FOX_WIDTH + a:FOX_WIDTH + a + LANES]
        qv = rq[:, a:a + LANES] * QK_SCALE
        kv = rk[:, a:a + LANES]
        if hd % 2 == 0:
            qaug_ref[0, hd] = jnp.where(low, qv, gq).astype(BF16)
            kaug_ref[0, hd] = jnp.where(low, kv, gk).astype(BF16)
        else:
            qaug_ref[0, hd] = jnp.where(low, gq, qv).astype(BF16)
            kaug_ref[0, hd] = jnp.where(low, gk, kv).astype(BF16)


def _aug_placement():
    e = np.zeros((LANES, 2 * FOX_HEADS * HEAD_DIM), np.float32)
    for hd in range(FOX_HEADS):
        base = (hd // 2) * LANES + (HEAD_DIM if hd % 2 == 0 else 0)
        kbase = FOX_HEADS * HEAD_DIM + base
        for part in range(3):
            e[part * FOX_HEADS + hd, base + part] = 1.0
            e[_ONE_LANE, base + 3 + part] = 1.0
            e[_ONE_LANE, kbase + part] = 1.0
            e[part * FOX_HEADS + hd, kbase + 3 + part] = -1.0
    return e


def _proj(x, pos_f, scale, shift, gn, invf, bf_rep, w_all, e_mat, *, tm):
    b, s, _ = x.shape
    grid = (b, s // tm)
    row = lambda bi, i: (bi, i, 0)
    per_b = lambda bi, i: (bi, 0, 0)
    const2 = lambda bi, i: (0, 0)
    once = pl.Buffered(1)
    in_specs = [
        pl.BlockSpec((1, tm, D_MODEL), row),
        pl.BlockSpec((1, tm, 1), row),
        pl.BlockSpec((1, 1, D_MODEL), per_b),
        pl.BlockSpec((1, 1, D_MODEL), per_b),
        pl.BlockSpec((1, D_MODEL), const2),
        pl.BlockSpec((1, LANES), const2),
        pl.BlockSpec((1, LANES), const2),
        pl.BlockSpec((D_MODEL, _PROJ_COLS), const2, pipeline_mode=once),
        pl.BlockSpec((LANES, 2 * FOX_WIDTH), const2, pipeline_mode=once),
    ]
    head4 = lambda bi, i: (bi, 0, i, 0)
    out_specs = [
        pl.BlockSpec((1, tm, SWA_WIDTH), row),
        pl.BlockSpec((1, tm, 2 * SWA_KV_WIDTH), row),
        pl.BlockSpec((1, tm, 2 * SWA_KV_WIDTH), row),
        pl.BlockSpec((1, tm, SWA_WIDTH), row),
        pl.BlockSpec((1, FOX_HEADS, tm, LANES), head4),
        pl.BlockSpec((1, FOX_HEADS, tm, LANES), head4),
        pl.BlockSpec((1, tm, FOX_WIDTH), row),
        pl.BlockSpec((1, tm, FOX_WIDTH), row),
        pl.BlockSpec((1, tm, D_MODEL), row),
        pl.BlockSpec((1, tm, D_MODEL), row),
    ]
    sds = jax.ShapeDtypeStruct
    out_shape = [
        sds((b, s, SWA_WIDTH), BF16), sds((b, s, 2 * SWA_KV_WIDTH), BF16),
        sds((b, s, 2 * SWA_KV_WIDTH), BF16), sds((b, s, SWA_WIDTH), BF16),
        sds((b, FOX_HEADS, s, LANES), BF16), sds((b, FOX_HEADS, s, LANES), BF16),
        sds((b, s, FOX_WIDTH), BF16), sds((b, s, FOX_WIDTH), BF16),
        sds((b, s, D_MODEL), BF16), sds((b, s, D_MODEL), BF16),
    ]
    return pl.pallas_call(
        functools.partial(_proj_kernel, tm=tm),
        grid=grid, in_specs=in_specs, out_specs=out_specs, out_shape=out_shape,
        scratch_shapes=[pltpu.VMEM((1, LANES), F32)],
        compiler_params=pltpu.CompilerParams(
            dimension_semantics=("arbitrary", "arbitrary"),
            vmem_limit_bytes=_vmem_limit(56 * 1024 * 1024)),
        name="in_proj",
    )(x, pos_f, scale, shift, gn, invf, bf_rep, w_all, e_mat)


def _swa_kernel(sinks_ref, q_ref, kc_ref, kp_ref, vc_ref, vp_ref, o_ref, kband, vband, *, tq):
    i = pl.program_id(1)
    nblk = tq // WINDOW
    kband[0:WINDOW] = kp_ref[0]
    kband[WINDOW:] = kc_ref[0]
    vband[0:WINDOW] = vp_ref[0]
    vband[WINDOW:] = vc_ref[0]

    rows = SWA_Q_HEADS // SWA_KV_HEADS * WINDOW
    t = lax.broadcasted_iota(jnp.int32, (rows, 2 * WINDOW), 0) % WINDOW
    col = lax.broadcasted_iota(jnp.int32, (rows, 2 * WINDOW), 1)
    band_mask = jnp.logical_or(jnp.logical_and(col < WINDOW, col > t),
                               jnp.logical_and(col >= WINDOW, (col - WINDOW) <= t))
    first_mask = jnp.logical_and(band_mask, jnp.logical_or(col >= WINDOW, i > 0))
    rowid = lax.broadcasted_iota(jnp.int32, (rows, 1), 0) // WINDOW
    lane = lax.broadcasted_iota(jnp.int32, (WINDOW, LANES), 1)
    low = lane < HEAD_DIM

    for r in range(nblk):
        mask = first_mask if r == 0 else band_mask
        for g in range(SWA_KV_HEADS):
            kb = kband[r * WINDOW:(r + 2) * WINDOW, g * LANES:(g + 1) * LANES]
            vb = vband[r * WINDOW:(r + 2) * WINDOW, g * LANES:(g + 1) * LANES]
            qs = []
            for c in range(2):
                a = (2 * g + c) * LANES
                qc = q_ref[0, r * WINDOW:(r + 1) * WINDOW, a:a + LANES].astype(F32)
                qs.append(jnp.where(low, qc, 0.0).astype(BF16))
                qs.append(jnp.where(low, 0.0, qc).astype(BF16))
            qst = jnp.concatenate(qs, axis=0)
            s = lax.dot_general(qst, kb, (((1,), (1,)), ((), ())), preferred_element_type=F32)
            s = jnp.where(mask, s, -jnp.inf)
            sink = jnp.zeros((rows, 1), F32)
            for k in range(SWA_Q_HEADS // SWA_KV_HEADS):
                sink = jnp.where(rowid == k, sinks_ref[g * (SWA_Q_HEADS // SWA_KV_HEADS) + k], sink)
            m = jnp.maximum(jnp.max(s, axis=-1, keepdims=True), sink)
            p = jnp.exp(s - m)
            den = jnp.sum(p, axis=-1, keepdims=True) + jnp.exp(sink - m)
            o = jnp.dot((p / den).astype(BF16), vb, preferred_element_type=F32)
            for c in range(2):
                a = (2 * g + c) * LANES
                oc = jnp.where(low, o[2 * c * WINDOW:(2 * c + 1) * WINDOW],
                               o[(2 * c + 1) * WINDOW:(2 * c + 2) * WINDOW])
                o_ref[0, r * WINDOW:(r + 1) * WINDOW, a:a + LANES] = oc.astype(BF16)


def _swa(sinks, qa, ka, va, *, tq):
    b, s, _ = qa.shape
    kvw = 2 * SWA_KV_WIDTH
    per = tq // WINDOW
    cur = lambda bi, i: (bi, i, 0)
    prev = lambda bi, i: (bi, jnp.maximum(i * per - 1, 0), 0)
    return pl.pallas_call(
        functools.partial(_swa_kernel, tq=tq),
        grid=(b, s // tq),
        in_specs=[pl.BlockSpec(memory_space=pltpu.SMEM),
                  pl.BlockSpec((1, tq, SWA_WIDTH), cur),
                  pl.BlockSpec((1, tq, kvw), cur),
                  pl.BlockSpec((1, WINDOW, kvw), prev),
                  pl.BlockSpec((1, tq, kvw), cur),
                  pl.BlockSpec((1, WINDOW, kvw), prev)],
        out_specs=pl.BlockSpec((1, tq, SWA_WIDTH), cur),
        out_shape=jax.ShapeDtypeStruct((b, s, SWA_WIDTH), BF16),
        scratch_shapes=[pltpu.VMEM((tq + WINDOW, kvw), BF16), pltpu.VMEM((tq + WINDOW, kvw), BF16)],
        compiler_params=pltpu.CompilerParams(dimension_semantics=("arbitrary", "arbitrary")),
        name="swa_attn",
    )(sinks, qa, ka, ka, va, va)


def _fox_kernel(q_ref, k_ref, v_ref, o_ref, m_ref, l_ref, acc_ref, *, t):
    i = pl.program_id(2)
    m_ref[...] = jnp.full_like(m_ref, -jnp.inf)
    l_ref[...] = jnp.zeros_like(l_ref)
    acc_ref[...] = jnp.zeros_like(acc_ref)
    nt = (((1,), (1,)), ((), ()))

    def tile(j, mask):
        start = pl.multiple_of(j * t, t)
        v = v_ref[0, pl.ds(start, t), :]
        for hh in range(2):
            s = lax.dot_general(q_ref[0, hh], k_ref[0, hh, pl.ds(start, t), :], nt,
                                preferred_element_type=F32)
            if mask is not None:
                s = jnp.where(mask, s, -jnp.inf)
            m_old = m_ref[hh]
            m_new = jnp.maximum(m_old, jnp.max(s, axis=-1, keepdims=True))
            p = jnp.exp(s - m_new)
            alpha = jnp.exp(m_old - m_new)
            l_ref[hh] = alpha * l_ref[hh] + jnp.sum(p, axis=-1, keepdims=True)
            acc_ref[hh] = alpha * acc_ref[hh] + jnp.dot(p.astype(BF16), v, preferred_element_type=F32)
            m_ref[hh] = m_new

    def body(j, carry):
        tile(j, None)
        return carry

    lax.fori_loop(0, i, body, 0)
    causal = (lax.broadcasted_iota(jnp.int32, (t, t), 0) >= lax.broadcasted_iota(jnp.int32, (t, t), 1))
    tile(i, causal)

    low = lax.broadcasted_iota(jnp.int32, (t, LANES), 1) < HEAD_DIM
    o = jnp.where(low, acc_ref[0] / l_ref[0], acc_ref[1] / l_ref[1])
    o_ref[0] = o.astype(BF16)


def _fox(q_aug, k_aug, vb, *, t):
    b, _, s, _ = q_aug.shape
    npair = FOX_HEADS // 2
    return pl.pallas_call(
        functools.partial(_fox_kernel, t=t),
        grid=(b, npair, s // t),
        in_specs=[pl.BlockSpec((1, 2, t, LANES), lambda bi, p, i: (bi, p, i, 0)),
                  pl.BlockSpec((1, 2, s, LANES), lambda bi, p, i: (bi, p, 0, 0)),
                  pl.BlockSpec((1, s, LANES), lambda bi, p, i: (bi, 0, p))],
        out_specs=pl.BlockSpec((1, t, LANES), lambda bi, p, i: (bi, i, p)),
        out_shape=jax.ShapeDtypeStruct((b, s, FOX_WIDTH), BF16),
        scratch_shapes=[pltpu.VMEM((2, t, 1), F32), pltpu.VMEM((2, t, 1), F32),
                        pltpu.VMEM((2, t, LANES), F32)],
        compiler_params=pltpu.CompilerParams(
            dimension_semantics=("arbitrary", "arbitrary", "arbitrary"),
            vmem_limit_bytes=_vmem_limit(48 * 1024 * 1024)),
        name="fox_attn",
    )(q_aug, k_aug, vb)


def _out_kernel(x_ref, aa_ref, sza_ref, ab_ref, szb_ref, sga_ref, sgb_ref, gate_ref,
                woa_ref, wob_ref, wout_ref, gf_ref, o_ref):
    a = (aa_ref[0].astype(F32) * sza_ref[0].astype(F32)).astype(BF16)
    bb = (ab_ref[0].astype(F32) * szb_ref[0].astype(F32)).astype(BF16)
    ya = jnp.dot(a, woa_ref[...], preferred_element_type=F32)
    yb = jnp.dot(bb, wob_ref[...], preferred_element_type=F32)
    merged = sga_ref[0].astype(F32) * ya + sgb_ref[0].astype(F32) * yb
    d = jnp.dot(merged.astype(BF16), wout_ref[...], preferred_element_type=F32)
    xo = x_ref[0] + gate_ref[0] * d
    ms = jnp.mean(xo * xo, axis=-1, keepdims=True)
    o_ref[0] = xo * lax.rsqrt(ms + NORM_EPS) * gf_ref[...]


def _out(x, att_a, sza, att_b, szb, sga, sgb, gate, woa, wob, wout, gf, *, tm):
    b, s, _ = x.shape
    row = lambda bi, i: (bi, i, 0)
    per_b = lambda bi, i: (bi, 0, 0)
    const2 = lambda bi, i: (0, 0)
    once = pl.Buffered(1)
    return pl.pallas_call(
        _out_kernel,
        grid=(b, s // tm),
        in_specs=[pl.BlockSpec((1, tm, D_MODEL), row),
                  pl.BlockSpec((1, tm, SWA_WIDTH), row), pl.BlockSpec((1, tm, SWA_WIDTH), row),
                  pl.BlockSpec((1, tm, FOX_WIDTH), row), pl.BlockSpec((1, tm, FOX_WIDTH), row),
                  pl.BlockSpec((1, tm, D_MODEL), row), pl.BlockSpec((1, tm, D_MODEL), row),
                  pl.BlockSpec((1, 1, D_MODEL), per_b),
                  pl.BlockSpec((SWA_WIDTH, D_MODEL), const2, pipeline_mode=once),
                  pl.BlockSpec((FOX_WIDTH, D_MODEL), const2, pipeline_mode=once),
                  pl.BlockSpec((D_MODEL, D_MODEL), const2, pipeline_mode=once),
                  pl.BlockSpec((1, D_MODEL), const2)],
        out_specs=pl.BlockSpec((1, tm, D_MODEL), row),
        out_shape=jax.ShapeDtypeStruct((b, s, D_MODEL), F32),
        compiler_params=pltpu.CompilerParams(
            dimension_semantics=("arbitrary", "arbitrary"),
            vmem_limit_bytes=_vmem_limit(48 * 1024 * 1024)),
        name="out_proj",
    )(x, att_a, sza, att_b, szb, sga, sgb, gate, woa, wob, wout, gf)


def _rot_cols(w, nheads):
    w3 = w.reshape(w.shape[0], nheads, HEAD_DIM)
    return jnp.concatenate([-w3[..., HALF:], w3[..., :HALF]], axis=-1).reshape(w.shape)


def _dup_heads(w, nheads):
    w3 = w.reshape(w.shape[0], nheads, 1, HEAD_DIM)
    return jnp.broadcast_to(w3, (w.shape[0], nheads, 2, HEAD_DIM)).reshape(w.shape[0], 2 * nheads * HEAD_DIM)


def _layout_w_in(w):
    o = np.cumsum([0, SWA_WIDTH, SWA_KV_WIDTH, SWA_KV_WIDTH, SWA_WIDTH, FOX_WIDTH, FOX_WIDTH, FOX_WIDTH,
                   FOX_HEADS, FOX_WIDTH, D_MODEL, D_MODEL]).tolist()
    qa, ka, va, za, qb, kb, vb, fb, zb, ga, gb = [w[:, o[k]:o[k + 1]] for k in range(11)]
    cols = [qa, _rot_cols(qa, SWA_Q_HEADS),
            _dup_heads(ka, SWA_KV_HEADS), _dup_heads(_rot_cols(ka, SWA_KV_HEADS), SWA_KV_HEADS),
            _dup_heads(va, SWA_KV_HEADS), za, qb, kb, vb,
            jnp.tile(fb, (1, LANES // FOX_HEADS)), zb, ga, gb]
    w_all = jnp.concatenate(cols, axis=1).astype(BF16)
    assert w_all.shape[1] == _PROJ_COLS
    return w_all


def kernel(x, c, positions, w_ada, b_ada, g_norm, w_in, b_f, sinks, w_o_swa, w_o_fox, w_out, g_final):
    b, s, _ = x.shape
    depth = w_in.shape[0]
    assert depth == 1, "the output stage fuses the final RMSNorm into the single layer"
    inv_freq = ROPE_THETA ** (-jnp.arange(0, HEAD_DIM, 2, dtype=F32) / HEAD_DIM)
    invf = jnp.tile(inv_freq, LANES // HALF)[None, :]
    pos_f = positions.astype(F32)[..., None]
    e_mat = jnp.asarray(_aug_placement(), BF16)
    c_pad = jnp.zeros((8, D_MODEL), F32).at[:b].set(c)
    for l in range(depth):
        ada = _ada(c_pad, w_ada[l], b_ada[l][None, :])[:b]
        shift, scale, gate = [ada[:, None, k * D_MODEL:(k + 1) * D_MODEL] for k in range(3)]
        w_all = _layout_w_in(w_in[l])
        bf_rep = jnp.tile(b_f[l].astype(F32), LANES // FOX_HEADS)[None, :]
        qa, ka, va, sza, q_aug, k_aug, vb, szb, sga, sgb = _proj(
            x, pos_f, scale, shift, g_norm[l][None, :], invf, bf_rep, w_all, e_mat, tm=512)
        att_a = _swa(sinks[l].astype(F32), qa, ka, va, tq=512)
        att_b = _fox(q_aug, k_aug, vb, t=512)
        x = _out(x, att_a, sza, att_b, szb, sga, sgb, gate,
                 w_o_swa[l].astype(BF16), w_o_fox[l].astype(BF16), w_out[l].astype(BF16),
                 g_final[None, :], tm=512)
    return x
```

```python
import functools
import math

import numpy as np
import jax
import jax.numpy as jnp
from jax import lax
from jax.experimental import pallas as pl
from jax.experimental.pallas import tpu as pltpu

D_MODEL = 1024
HEAD_DIM = 64
HALF = HEAD_DIM // 2
SWA_Q_HEADS = 8
SWA_KV_HEADS = 2
SWA_WIDTH = SWA_Q_HEADS * HEAD_DIM
SWA_KV_WIDTH = SWA_KV_HEADS * HEAD_DIM
FOX_HEADS = 8
FOX_WIDTH = FOX_HEADS * HEAD_DIM
WINDOW = 128
ROPE_THETA = 10000.0
NORM_EPS = 1e-6
QK_SCALE = HEAD_DIM ** -0.5
LOG2E = math.log2(math.e)

LANES = 128
V7X_VMEM_BYTES = 64 * 1024 * 1024

F32 = jnp.float32
BF16 = jnp.bfloat16

_QA, _KA, _VA, _ZA = 0, 1024, 1536, 1792
_QB, _KB, _VB, _FB, _ZB, _GA, _GB = 2304, 2816, 3328, 3840, 3968, 4480, 5504
_PROJ_COLS = 6528

_ONE_LANE = 24


def _vmem_limit(nbytes):
    return int(min(nbytes, V7X_VMEM_BYTES - 8 * 1024 * 1024))


def _split3(v):
    hi = v.astype(BF16)
    r1 = v - hi.astype(F32)
    mid = r1.astype(BF16)
    lo = (r1 - mid.astype(F32)).astype(BF16)
    return hi, mid, lo


def _ada_kernel(c_ref, w_ref, b_ref, o_ref):
    o_ref[...] = jnp.dot(c_ref[...], w_ref[...], preferred_element_type=F32,
                         precision=lax.Precision.HIGHEST) + b_ref[...]


def _ada(c_pad, w_ada, b_ada):
    rows = c_pad.shape[0]
    n = w_ada.shape[1]
    nblk = n // D_MODEL
    return pl.pallas_call(
        _ada_kernel,
        grid=(nblk,),
        in_specs=[pl.BlockSpec((rows, D_MODEL), lambda j: (0, 0)),
                  pl.BlockSpec((D_MODEL, D_MODEL), lambda j: (0, j)),
                  pl.BlockSpec((1, D_MODEL), lambda j: (0, j))],
        out_specs=pl.BlockSpec((rows, D_MODEL), lambda j: (0, j)),
        out_shape=jax.ShapeDtypeStruct((rows, n), F32),
        name="ada_mod",
    )(c_pad, w_ada, b_ada)


def _sigmoid(z):
    return 1.0 / (1.0 + jnp.exp(-z))


def _proj_kernel(x_ref, pos_ref, scale_ref, shift_ref, gn_ref, invf_ref, bf_ref, w_ref, e_ref,
                 qa_ref, ka_ref, va_ref, sza_ref, qaug_ref, kaug_ref, vb_ref, szb_ref,
                 sga_ref, sgb_ref, carry_ref, *, tm):
    i = pl.program_id(1)

    @pl.when(i == 0)
    def _():
        carry_ref[...] = jnp.zeros_like(carry_ref)

    x = x_ref[0]
    ms = jnp.mean(x * x, axis=-1, keepdims=True)
    y = x * lax.rsqrt(ms + NORM_EPS) * gn_ref[...]
    h = y * (1.0 + scale_ref[0]) + shift_ref[0]
    hb = h.astype(BF16)

    def proj(off, n):
        return jnp.dot(hb, w_ref[:, off:off + n], preferred_element_type=F32)

    lane = lax.broadcasted_iota(jnp.int32, (tm, LANES), 1)
    low = lane < HEAD_DIM

    ang = pos_ref[0] * invf_ref[...]
    cos = jnp.cos(ang)
    sin = jnp.sin(ang)

    r = proj(_QA, 2 * SWA_WIDTH)
    for c in range(SWA_WIDTH // LANES):
        a = c * LANES
        t = r[:, a:a + LANES] * cos + r[:, SWA_WIDTH + a:SWA_WIDTH + a + LANES] * sin
        qa_ref[0, :, a:a + LANES] = (t * QK_SCALE).astype(BF16)
    r = proj(_KA, 4 * SWA_KV_WIDTH)
    for c in range(2 * SWA_KV_WIDTH // LANES):
        a = c * LANES
        t = r[:, a:a + LANES] * cos + r[:, 2 * SWA_KV_WIDTH + a:2 * SWA_KV_WIDTH + a + LANES] * sin
        ka_ref[0, :, a:a + LANES] = t.astype(BF16)
    va_ref[0] = proj(_VA, 2 * SWA_KV_WIDTH).astype(BF16)
    z = proj(_ZA, SWA_WIDTH)
    sza_ref[0] = (z * _sigmoid(z)).astype(BF16)

    vb_ref[0] = proj(_VB, FOX_WIDTH).astype(BF16)
    z = proj(_ZB, FOX_WIDTH)
    szb_ref[0] = (z * _sigmoid(z)).astype(BF16)
    sga_ref[0] = _sigmoid(proj(_GA, D_MODEL)).astype(BF16)
    sgb_ref[0] = _sigmoid(proj(_GB, D_MODEL)).astype(BF16)

    zf = proj(_FB, LANES) + bf_ref[...]
    logf = jnp.minimum(zf, 0.0) - jnp.log1p(jnp.exp(-jnp.abs(zf)))
    grp = lane // FOX_HEADS
    hi, mid, lo = _split3(logf)
    zero = jnp.zeros((tm, LANES), F32)
    parts = jnp.where(grp == 0, hi.astype(F32),
                      jnp.where(grp == 1, mid.astype(F32),
                                jnp.where(grp == 2, lo.astype(F32), zero))).astype(BF16)
    tri = (lax.broadcasted_iota(jnp.int32, (tm, tm), 0)
           >= lax.broadcasted_iota(jnp.int32, (tm, tm), 1)).astype(BF16)
    rsum = jnp.dot(tri, parts, preferred_element_type=F32)
    c0 = rsum + pltpu.roll(rsum, LANES - FOX_HEADS, 1) + pltpu.roll(rsum, LANES - 2 * FOX_HEADS, 1)
    c0 = jnp.where(grp == 0, c0, zero)
    cl = c0 + pltpu.roll(c0, FOX_HEADS, 1) + pltpu.roll(c0, 2 * FOX_HEADS, 1)
    cum = cl + carry_ref[...]
    carry_ref[...] = cum[tm - 1:tm, :]

    hi, mid, lo = _split3(cum * LOG2E)
    one =jnp.ones((tm, LANES), F32)
    carrier = jnp.where(grp == 0, hi.astype(F32),
                        jnp.where(grp == 1, mid.astype(F32),
                                  jnp.where(grp == 2, lo.astype(F32),
                                            jnp.where(lane == _ONE_LANE, one, zero)))).astype(BF16)
    aug = jnp.dot(carrier, e_ref[...], preferred_element_type=F32)

    rq = proj(_QB, FOX_WIDTH)
    rk = proj(_KB, FOX_WIDTH)
    for hd in range(FOX_HEADS):
        a = (hd // 2) * LANES
        gq = aug[:, a:a + LANES]
        gk = aug[:, FOX_WIDTH + a:FOX_WIDTH + a + LANES]
        qv = rq[:, a:a + LANES] * (QK_SCALE * LOG2E)
        kv = rk[:, a:a + LANES]
        if hd % 2 == 0:
            qaug_ref[0, hd] = jnp.where(low, qv, gq).astype(BF16)
            kaug_ref[0, hd] = jnp.where(low, kv, gk).astype(BF16)
        else:
            qaug_ref[0, hd] = jnp.where(low, gq, qv).astype(BF16)
            kaug_ref[0, hd] = jnp.where(low, gk, kv).astype(BF16)


def _aug_placement():
    e = np.zeros((LANES, 2 * FOX_HEADS * HEAD_DIM), np.float32)
    for hd in range(FOX_HEADS):
        base = (hd // 2) * LANES + (HEAD_DIM if hd % 2 == 0 else 0)
        kbase = FOX_HEADS * HEAD_DIM + base
        for part in range(3):
            e[part * FOX_HEADS + hd, base + part] = 1.0
            e[_ONE_LANE, base + 3 + part] = 1.0
            e[_ONE_LANE, kbase + part] = 1.0
            e[part * FOX_HEADS + hd, kbase + 3 + part] = -1.0
    return e


def _proj(x, pos_f, scale, shift, gn, invf, bf_rep, w_all, e_mat, *, tm):
    b, s, _ = x.shape
    grid = (b, s // tm)
    row = lambda bi, i: (bi, i, 0)
    per_b = lambda bi, i: (bi, 0, 0)
    const2 = lambda bi, i: (0, 0)
    once = pl.Buffered(1)
    in_specs = [
        pl.BlockSpec((1, tm, D_MODEL), row),
        pl.BlockSpec((1, tm, 1), row),
        pl.BlockSpec((1, 1, D_MODEL), per_b),
        pl.BlockSpec((1, 1, D_MODEL), per_b),
        pl.BlockSpec((1, D_MODEL), const2),
        pl.BlockSpec((1, LANES), const2),
        pl.BlockSpec((1, LANES), const2),
        pl.BlockSpec((D_MODEL, _PROJ_COLS), const2, pipeline_mode=once),
        pl.BlockSpec((LANES, 2 * FOX_WIDTH), const2, pipeline_mode=once),
    ]
    head4 = lambda bi, i: (bi, 0, i, 0)
    out_specs = [
        pl.BlockSpec((1, tm, SWA_WIDTH), row),
        pl.BlockSpec((1, tm, 2 * SWA_KV_WIDTH), row),
        pl.BlockSpec((1, tm, 2 * SWA_KV_WIDTH), row),
        pl.BlockSpec((1, tm, SWA_WIDTH), row),
        pl.BlockSpec((1, FOX_HEADS, tm, LANES), head4),
        pl.BlockSpec((1, FOX_HEADS, tm, LANES), head4),
        pl.BlockSpec((1, tm, FOX_WIDTH), row),
        pl.BlockSpec((1, tm, FOX_WIDTH), row),
        pl.BlockSpec((1, tm, D_MODEL), row),
        pl.BlockSpec((1, tm, D_MODEL), row),
    ]
    sds = jax.ShapeDtypeStruct
    out_shape = [
        sds((b, s, SWA_WIDTH), BF16), sds((b, s, 2 * SWA_KV_WIDTH), BF16),
        sds((b, s, 2 * SWA_KV_WIDTH), BF16), sds((b, s, SWA_WIDTH), BF16),
        sds((b, FOX_HEADS, s, LANES), BF16), sds((b, FOX_HEADS, s, LANES), BF16),
        sds((b, s, FOX_WIDTH), BF16), sds((b, s, FOX_WIDTH), BF16),
        sds((b, s, D_MODEL), BF16), sds((b, s, D_MODEL), BF16),
    ]
    return pl.pallas_call(
        functools.partial(_proj_kernel, tm=tm),
        grid=grid, in_specs=in_specs, out_specs=out_specs, out_shape=out_shape,
        scratch_shapes=[pltpu.VMEM((1, LANES), F32)],
        compiler_params=pltpu.CompilerParams(
            dimension_semantics=("arbitrary", "arbitrary"),
            vmem_limit_bytes=_vmem_limit(56 * 1024 * 1024)),
        name="in_proj",
    )(x, pos_f, scale, shift, gn, invf, bf_rep, w_all, e_mat)


def _swa_kernel(sinks_ref, q_ref, kc_ref, kp_ref, vc_ref, vp_ref, o_ref, kband, vband, *, tq):
    i = pl.program_id(1)
    nblk = tq // WINDOW
    kband[0:WINDOW] = kp_ref[0]
    kband[WINDOW:] = kc_ref[0]
    vband[0:WINDOW] = vp_ref[0]
    vband[WINDOW:] = vc_ref[0]

    rows = SWA_Q_HEADS // SWA_KV_HEADS * WINDOW
    t = lax.broadcasted_iota(jnp.int32, (rows, 2 * WINDOW), 0) % WINDOW
    col = lax.broadcasted_iota(jnp.int32, (rows, 2 * WINDOW), 1)
    band_mask = jnp.logical_or(jnp.logical_and(col < WINDOW, col > t),
                               jnp.logical_and(col >= WINDOW, (col - WINDOW) <= t))
    first_mask = jnp.logical_and(band_mask, jnp.logical_or(col >= WINDOW, i > 0))
    rowid = lax.broadcasted_iota(jnp.int32, (rows, 1), 0) // WINDOW
    lane = lax.broadcasted_iota(jnp.int32, (WINDOW, LANES), 1)
    low = lane < HEAD_DIM

    for r in range(nblk):
        mask = first_mask if r == 0 else band_mask
        for g in range(SWA_KV_HEADS):
            kb = kband[r * WINDOW:(r + 2) * WINDOW, g * LANES:(g + 1) * LANES]
            vb = vband[r * WINDOW:(r + 2) * WINDOW, g * LANES:(g + 1) * LANES]
            qs = []
            for c in range(2):
                a = (2 * g + c) * LANES
                qc = q_ref[0, r * WINDOW:(r + 1) * WINDOW, a:a + LANES].astype(F32)
                qs.append(jnp.where(low, qc, 0.0).astype(BF16))
                qs.append(jnp.where(low, 0.0, qc).astype(BF16))
            qst = jnp.concatenate(qs, axis=0)
            s = lax.dot_general(qst, kb, (((1,), (1,)), ((), ())), preferred_element_type=F32)
            s = jnp.where(mask, s, -jnp.inf)
            sink = jnp.zeros((rows, 1), F32)
            for k in range(SWA_Q_HEADS // SWA_KV_HEADS):
                sink = jnp.where(rowid == k, sinks_ref[g * (SWA_Q_HEADS // SWA_KV_HEADS) + k], sink)
            m = jnp.maximum(jnp.max(s, axis=-1, keepdims=True), sink)
            p = jnp.exp(s - m)
            den = jnp.sum(p, axis=-1, keepdims=True) + jnp.exp(sink - m)
            o = jnp.dot((p / den).astype(BF16), vb, preferred_element_type=F32)
            for c in range(2):
                a = (2 * g + c) * LANES
                oc = jnp.where(low, o[2 * c * WINDOW:(2 * c + 1) * WINDOW],
                               o[(2 * c + 1) * WINDOW:(2 * c + 2) * WINDOW])
                o_ref[0, r * WINDOW:(r + 1) * WINDOW, a:a + LANES] = oc.astype(BF16)


def _swa(sinks, qa, ka, va, *, tq):
    b, s, _ = qa.shape
    kvw = 2 * SWA_KV_WIDTH
    per = tq // WINDOW
    cur = lambda bi, i: (bi, i, 0)
    prev = lambda bi, i: (bi, jnp.maximum(i * per - 1, 0), 0)
    return pl.pallas_call(
        functools.partial(_swa_kernel, tq=tq),
        grid=(b, s // tq),
        in_specs=[pl.BlockSpec(memory_space=pltpu.SMEM),
                  pl.BlockSpec((1, tq, SWA_WIDTH), cur),
                  pl.BlockSpec((1, tq, kvw), cur),
                  pl.BlockSpec((1, WINDOW, kvw), prev),
                  pl.BlockSpec((1, tq, kvw), cur),
                  pl.BlockSpec((1, WINDOW, kvw), prev)],
        out_specs=pl.BlockSpec((1, tq, SWA_WIDTH), cur),
        out_shape=jax.ShapeDtypeStruct((b, s, SWA_WIDTH), BF16),
        scratch_shapes=[pltpu.VMEM((tq + WINDOW, kvw), BF16), pltpu.VMEM((tq + WINDOW, kvw), BF16)],
        compiler_params=pltpu.CompilerParams(dimension_semantics=("arbitrary", "arbitrary")),
        name="swa_attn",
    )(sinks, qa, ka, ka, va, va)


_FOX_ROWS = 64


def _fox_kernel(q_ref, k_ref, v_ref, o_ref, s_buf, mx_buf, p_buf, m_ref, l_ref, acc_ref, *, t):
    i = pl.program_id(2)
    m_ref[...] = jnp.full_like(m_ref, -jnp.inf)
    l_ref[...] = jnp.zeros_like(l_ref)
    acc_ref[...] = jnp.zeros_like(acc_ref)
    nt = (((1,), (1,)), ((), ()))
    ncol = t // LANES

    def qk(j, slot, mask):
        start = pl.multiple_of(j * t, t)
        for hh in range(2):
            s = lax.dot_general(q_ref[0, hh], k_ref[0, hh, pl.ds(start, t), :], nt,
                                preferred_element_type=F32)
            if mask is not None:
                s = jnp.where(mask, s, -jnp.inf)
            s_buf[slot, hh] = s
            mx = s[:, 0:LANES]
            for c in range(1, ncol):
                mx = jnp.maximum(mx, s[:, c * LANES:(c + 1) * LANES])
            mx_buf[slot, hh] = mx

    def sp(j, slot):
        start = pl.multiple_of(j * t, t)
        v = v_ref[0, pl.ds(start, t), :]
        for hh in range(2):
            for r in range(t // _FOX_ROWS):
                rows = slice(r * _FOX_ROWS, (r + 1) * _FOX_ROWS)
                m_old = m_ref[hh, rows, :]
                m_new = jnp.maximum(m_old, jnp.max(mx_buf[slot, hh, rows, :], axis=-1, keepdims=True))
                alpha = jnp.exp2(m_old - m_new)
                m_ref[hh, rows, :] = m_new
                psum = None
                for c in range(ncol):
                    p = jnp.exp2(s_buf[slot, hh, rows, c * LANES:(c + 1) * LANES] - m_new)
                    p_buf[hh, rows, c * LANES:(c + 1) * LANES] = p.astype(BF16)
                    psum = p if psum is None else psum + p
                l_ref[hh, rows, :] = alpha * l_ref[hh, rows, :] + psum
                acc_ref[hh, rows, :] = alpha * acc_ref[hh, rows, :]
            acc_ref[hh] += jnp.dot(p_buf[hh], v, preferred_element_type=F32)

    causal = (lax.broadcasted_iota(jnp.int32, (t, t), 0) >= lax.broadcasted_iota(jnp.int32, (t, t), 1))
    qk(i, 0, causal)

    def body(n, carry):
        qk(n, (n + 1) % 2, None)
        sp(jnp.where(n == 0, i, n - 1), n % 2)
        return carry

    lax.fori_loop(0, i, body, 0)
    sp(jnp.where(i == 0, i, i - 1), i % 2)

    low = lax.broadcasted_iota(jnp.int32, (t, LANES), 1) < HEAD_DIM
    l0 = jnp.sum(l_ref[0], axis=-1, keepdims=True)
    l1 = jnp.sum(l_ref[1], axis=-1, keepdims=True)
    o = jnp.where(low, acc_ref[0] / l0, acc_ref[1] / l1)
    o_ref[0] = o.astype(BF16)


def _fox(q_aug, k_aug, vb, *, t):
    b, _, s, _ = q_aug.shape
    npair = FOX_HEADS // 2
    return pl.pallas_call(
        functools.partial(_fox_kernel, t=t),
        grid=(b, npair, s // t),
        in_specs=[pl.BlockSpec((1, 2, t, LANES), lambda bi, p, i: (bi, p, i, 0)),
                  pl.BlockSpec((1, 2, s, LANES), lambda bi, p, i: (bi, p, 0, 0)),
                  pl.BlockSpec((1, s, LANES), lambda bi, p, i: (bi, 0, p))],
        out_specs=pl.BlockSpec((1, t, LANES), lambda bi, p, i: (bi, i, p)),
        out_shape=jax.ShapeDtypeStruct((b, s, FOX_WIDTH), BF16),
        scratch_shapes=[pltpu.VMEM((2, 2, t, t), F32),
                        pltpu.VMEM((2, 2, t, LANES), F32),
                        pltpu.VMEM((2, t, t), BF16),
                        pltpu.VMEM((2, t, LANES), F32),
                        pltpu.VMEM((2, t, LANES), F32),
                        pltpu.VMEM((2, t, LANES), F32)],
        compiler_params=pltpu.CompilerParams(
            dimension_semantics=("arbitrary", "arbitrary", "arbitrary"),
            vmem_limit_bytes=_vmem_limit(48 * 1024 * 1024)),
        name="fox_attn",
    )(q_aug, k_aug, vb)


def _out_kernel(x_ref, aa_ref, sza_ref, ab_ref, szb_ref, sga_ref, sgb_ref, gate_ref,
                woa_ref, wob_ref, wout_ref, gf_ref, o_ref):
    a = (aa_ref[0].astype(F32) * sza_ref[0].astype(F32)).astype(BF16)
    bb = (ab_ref[0].astype(F32) * szb_ref[0].astype(F32)).astype(BF16)
    ya = jnp.dot(a, woa_ref[...], preferred_element_type=F32)
    yb = jnp.dot(bb, wob_ref[...], preferred_element_type=F32)
    merged = sga_ref[0].astype(F32) * ya + sgb_ref[0].astype(F32) * yb
    d = jnp.dot(merged.astype(BF16), wout_ref[...], preferred_element_type=F32)
    xo = x_ref[0] + gate_ref[0] * d
    ms = jnp.mean(xo * xo, axis=-1, keepdims=True)
    o_ref[0] = xo * lax.rsqrt(ms + NORM_EPS) * gf_ref[...]


def _out(x, att_a, sza, att_b, szb, sga, sgb, gate, woa, wob, wout, gf, *, tm):
    b, s, _ = x.shape
    row = lambda bi, i: (bi, i, 0)
    per_b = lambda bi, i: (bi, 0, 0)
    const2 = lambda bi, i: (0, 0)
    once = pl.Buffered(1)
    return pl.pallas_call(
        _out_kernel,
        grid=(b, s // tm),
        in_specs=[pl.BlockSpec((1, tm, D_MODEL), row),
                  pl.BlockSpec((1, tm, SWA_WIDTH), row), pl.BlockSpec((1, tm, SWA_WIDTH), row),
                  pl.BlockSpec((1, tm, FOX_WIDTH), row), pl.BlockSpec((1, tm, FOX_WIDTH), row),
                  pl.BlockSpec((1, tm, D_MODEL), row), pl.BlockSpec((1, tm, D_MODEL), row),
                  pl.BlockSpec((1, 1, D_MODEL), per_b),
                  pl.BlockSpec((SWA_WIDTH, D_MODEL), const2, pipeline_mode=once),
                  pl.BlockSpec((FOX_WIDTH, D_MODEL), const2, pipeline_mode=once),
                  pl.BlockSpec((D_MODEL, D_MODEL), const2, pipeline_mode=once),
                  pl.BlockSpec((1, D_MODEL), const2)],
        out_specs=pl.BlockSpec((1, tm, D_MODEL), row),
        out_shape=jax.ShapeDtypeStruct((b, s, D_MODEL), F32),
        compiler_params=pltpu.CompilerParams(
            dimension_semantics=("arbitrary", "arbitrary"),
            vmem_limit_bytes=_vmem_limit(48 * 1024 * 1024)),
        name="out_proj",
    )(x, att_a, sza, att_b, szb, sga, sgb, gate, woa, wob, wout, gf)


def _rot_cols(w, nheads):
    w3 = w.reshape(w.shape[0], nheads, HEAD_DIM)
    return jnp.concatenate([-w3[..., HALF:], w3[..., :HALF]], axis=-1).reshape(w.shape)


def _dup_heads(w, nheads):
    w3 = w.reshape(w.shape[0], nheads, 1, HEAD_DIM)
    return jnp.broadcast_to(w3, (w.shape[0], nheads, 2, HEAD_DIM)).reshape(w.shape[0], 2 * nheads * HEAD_DIM)


def _layout_w_in(w):
    o = np.cumsum([0, SWA_WIDTH, SWA_KV_WIDTH, SWA_KV_WIDTH, SWA_WIDTH, FOX_WIDTH, FOX_WIDTH, FOX_WIDTH,
                   FOX_HEADS, FOX_WIDTH, D_MODEL, D_MODEL]).tolist()
    qa, ka, va, za, qb, kb, vb, fb, zb, ga, gb = [w[:, o[k]:o[k + 1]] for k in range(11)]
    cols = [qa, _rot_cols(qa, SWA_Q_HEADS),
            _dup_heads(ka, SWA_KV_HEADS), _dup_heads(_rot_cols(ka, SWA_KV_HEADS), SWA_KV_HEADS),
            _dup_heads(va, SWA_KV_HEADS), za, qb, kb, vb,
            jnp.tile(fb, (1, LANES // FOX_HEADS)), zb, ga, gb]
    w_all = jnp.concatenate(cols, axis=1).astype(BF16)
    assert w_all.shape[1] == _PROJ_COLS
    return w_all


def kernel(x, c, positions, w_ada, b_ada, g_norm, w_in, b_f, sinks, w_o_swa, w_o_fox, w_out, g_final):
    b, s, _ = x.shape
    depth = w_in.shape[0]
    assert depth == 1, "the output stage fuses the final RMSNorm into the single layer"
    inv_freq = ROPE_THETA ** (-jnp.arange(0, HEAD_DIM, 2, dtype=F32) / HEAD_DIM)
    invf = jnp.tile(inv_freq, LANES // HALF)[None, :]
    pos_f = positions.astype(F32)[..., None]
    e_mat = jnp.asarray(_aug_placement(), BF16)
    c_pad = jnp.zeros((8, D_MODEL), F32).at[:b].set(c)
    for l in range(depth):
        ada = _ada(c_pad, w_ada[l], b_ada[l][None, :])[:b]
        shift, scale, gate = [ada[:, None, k * D_MODEL:(k + 1) * D_MODEL] for k in range(3)]
        w_all = _layout_w_in(w_in[l])
        bf_rep = jnp.tile(b_f[l].astype(F32), LANES // FOX_HEADS)[None, :]
        qa, ka, va, sza, q_aug, k_aug, vb, szb, sga, sgb = _proj(
            x, pos_f, scale, shift, g_norm[l][None, :], invf, bf_rep, w_all, e_mat, tm=512)
        att_a = _swa(sinks[l].astype(F32), qa, ka, va, tq=512)
        att_b = _fox(q_aug, k_aug, vb, t=512)
        x = _out(x, att_a, sza, att_b, szb, sga, sgb, gate,
                 w_o_swa[l].astype(BF16), w_o_fox[l].astype(BF16), w_out[l].astype(BF16),
                 g_final[None, :], tm=512)
    return x
```

```python
import functools
import math

import numpy as np
import jax
import jax.numpy as jnp
from jax import lax
from jax.experimental import pallas as pl
from jax.experimental.pallas import tpu as pltpu

D_MODEL = 1024
HEAD_DIM = 64
HALF = HEAD_DIM // 2
SWA_Q_HEADS = 8
SWA_KV_HEADS = 2
SWA_WIDTH = SWA_Q_HEADS * HEAD_DIM
SWA_KV_WIDTH = SWA_KV_HEADS * HEAD_DIM
FOX_HEADS = 8
FOX_WIDTH = FOX_HEADS * HEAD_DIM
WINDOW = 128
ROPE_THETA = 10000.0
NORM_EPS = 1e-6
QK_SCALE = HEAD_DIM ** -0.5
LOG2E = math.log2(math.e)

LANES = 128
V7X_VMEM_BYTES = 64 * 1024 * 1024

F32 = jnp.float32
BF16 = jnp.bfloat16

_QA, _KA, _VA, _ZA = 0, 1024, 1536, 1792
_QB, _KB, _FB, _ZB, _GA, _GB = 2304, 2816, 3328, 3456, 3968, 4992
_PROJ_COLS = 6016

_ONE_LANE = 24


def _vmem_limit(nbytes):
    return int(min(nbytes, V7X_VMEM_BYTES - 8 * 1024 * 1024))


def _split3(v):
    hi = v.astype(BF16)
    r1 = v - hi.astype(F32)
    mid = r1.astype(BF16)
    lo = (r1 - mid.astype(F32)).astype(BF16)
    return hi, mid, lo


def _ada_kernel(c_ref, w_ref, b_ref, o_ref):
    o_ref[...] = jnp.dot(c_ref[...], w_ref[...], preferred_element_type=F32,
                         precision=lax.Precision.HIGHEST) + b_ref[...]


def _ada(c_pad, w_ada, b_ada):
    rows = c_pad.shape[0]
    n = w_ada.shape[1]
    nblk = n // D_MODEL
    return pl.pallas_call(
        _ada_kernel,
        grid=(nblk,),
        in_specs=[pl.BlockSpec((rows, D_MODEL), lambda j: (0, 0)),
                  pl.BlockSpec((D_MODEL, D_MODEL), lambda j: (0, j)),
                  pl.BlockSpec((1, D_MODEL), lambda j: (0, j))],
        out_specs=pl.BlockSpec((rows, D_MODEL), lambda j: (0, j)),
        out_shape=jax.ShapeDtypeStruct((rows, n), F32),
        name="ada_mod",
    )(c_pad, w_ada, b_ada)


def _sigmoid(z):
    return 1.0 / (1.0 + jnp.exp(-z))


def _proj_kernel(x_ref, pos_ref, scale_ref, shift_ref, gn_ref, invf_ref, bf_ref, w_ref, wvbt_ref, e_ref,
                 qa_ref, ka_ref, va_ref, sza_ref, qaug_ref, kaug_ref, vbt_ref, szb_ref,
                 sga_ref, sgb_ref, carry_ref, *, tm):
    i = pl.program_id(1)

    @pl.when(i == 0)
    def _():
        carry_ref[...] = jnp.zeros_like(carry_ref)

    x = x_ref[0]
    ms = jnp.mean(x * x, axis=-1, keepdims=True)
    y = x * lax.rsqrt(ms + NORM_EPS) * gn_ref[...]
    h = y * (1.0 + scale_ref[0]) + shift_ref[0]
    hb = h.astype(BF16)

    def proj(off, n):
        return jnp.dot(hb, w_ref[:, off:off + n], preferred_element_type=F32)

    lane = lax.broadcasted_iota(jnp.int32, (tm, LANES), 1)
    low = lane < HEAD_DIM

    ang = pos_ref[0] * invf_ref[...]
    cos = jnp.cos(ang)
    sin = jnp.sin(ang)

    r = proj(_QA, 2 * SWA_WIDTH)
    for c in range(SWA_WIDTH // LANES):
        a = c * LANES
        t = r[:, a:a + LANES] * cos + r[:, SWA_WIDTH + a:SWA_WIDTH + a + LANES] * sin
        qa_ref[0, :, a:a + LANES] = (t * QK_SCALE).astype(BF16)
    r = proj(_KA, 4 * SWA_KV_WIDTH)
    for c in range(2 * SWA_KV_WIDTH // LANES):
        a = c * LANES
        t = r[:, a:a + LANES] * cos + r[:, 2 * SWA_KV_WIDTH + a:2 * SWA_KV_WIDTH + a + LANES] * sin
        ka_ref[0, :, a:a + LANES] = t.astype(BF16)
    va_ref[0] = proj(_VA, 2 * SWA_KV_WIDTH).astype(BF16)
    z = proj(_ZA, SWA_WIDTH)
    sza_ref[0] = (z * _sigmoid(z)).astype(BF16)

    vbt_ref[0] = lax.dot_general(wvbt_ref[...], hb, (((1,), (1,)), ((), ())),
                                 preferred_element_type=F32).astype(BF16)
    z = proj(_ZB, FOX_WIDTH)
    szb_ref[0] = (z * _sigmoid(z)).astype(BF16)
    sga_ref[0] = _sigmoid(proj(_GA, D_MODEL)).astype(BF16)
    sgb_ref[0] = _sigmoid(proj(_GB, D_MODEL)).astype(BF16)

    zf = proj(_FB, LANES) + bf_ref[...]
    logf = jnp.minimum(zf, 0.0) - jnp.log1p(jnp.exp(-jnp.abs(zf)))
    grp = lane // FOX_HEADS
    hi, mid, lo = _split3(logf)
    zero = jnp.zeros((tm, LANES), F32)
    parts = jnp.where(grp == 0, hi.astype(F32),
                      jnp.where(grp == 1, mid.astype(F32),
                                jnp.where(grp == 2, lo.astype(F32), zero))).astype(BF16)
    tri = (lax.broadcasted_iota(jnp.int32, (tm, tm), 0)
           >= lax.broadcasted_iota(jnp.int32, (tm, tm), 1)).astype(BF16)
    rsum = jnp.dot(tri, parts, preferred_element_type=F32)
    c0 = rsum + pltpu.roll(rsum, LANES - FOX_HEADS, 1) + pltpu.roll(rsum, LANES - 2 * FOX_HEADS, 1)
    c0 = jnp.where(grp == 0, c0, zero)
    cl = c0 + pltpu.roll(c0, FOX_HEADS, 1) + pltpu.roll(c0, 2 * FOX_HEADS, 1)
    cum = cl + carry_ref[...]
    carry_ref[...] = cum[tm - 1:tm, :]

    hi, mid, lo = _split3(cum * LOG2E)
    one =jnp.ones((tm, LANES), F32)
    carrier = jnp.where(grp == 0, hi.astype(F32),
                        jnp.where(grp == 1, mid.astype(F32),
                                  jnp.where(grp == 2, lo.astype(F32),
                                            jnp.where(lane == _ONE_LANE, one, zero)))).astype(BF16)
    aug = jnp.dot(carrier, e_ref[...], preferred_element_type=F32)

    rq = proj(_QB, FOX_WIDTH)
    rk = proj(_KB, FOX_WIDTH)
    for hd in range(FOX_HEADS):
        a = (hd // 2) * LANES
        gq = aug[:, a:a + LANES]
        gk = aug[:, FOX_WIDTH + a:FOX_WIDTH + a + LANES]
        qv = rq[:, a:a + LANES] * (QK_SCALE * LOG2E)
        kv = rk[:, a:a + LANES]
        if hd % 2 == 0:
            qaug_ref[0, hd] = jnp.where(low, qv, gq).astype(BF16)
            kaug_ref[0, hd] = jnp.where(low, kv, gk).astype(BF16)
        else:
            qaug_ref[0, hd] = jnp.where(low, gq, qv).astype(BF16)
            kaug_ref[0, hd] = jnp.where(low, gk, kv).astype(BF16)


def _aug_placement():
    e = np.zeros((LANES, 2 * FOX_HEADS * HEAD_DIM), np.float32)
    for hd in range(FOX_HEADS):
        base = (hd // 2) * LANES + (HEAD_DIM if hd % 2 == 0 else 0)
        kbase = FOX_HEADS * HEAD_DIM + base
        for part in range(3):
            e[part * FOX_HEADS + hd, base + part] = 1.0
            e[_ONE_LANE, base + 3 + part] = 1.0
            e[_ONE_LANE, kbase + part] = 1.0
            e[part * FOX_HEADS + hd, kbase + 3 + part] = -1.0
    return e


def _proj(x, pos_f, scale, shift, gn, invf, bf_rep, w_all, w_vbt, e_mat, *, tm):
    b, s, _ = x.shape
    grid = (b, s // tm)
    row = lambda bi, i: (bi, i, 0)
    per_b = lambda bi, i: (bi, 0, 0)
    const2 = lambda bi, i: (0, 0)
    once = pl.Buffered(1)
    in_specs = [
        pl.BlockSpec((1, tm, D_MODEL), row),
        pl.BlockSpec((1, tm, 1), row),
        pl.BlockSpec((1, 1, D_MODEL), per_b),
        pl.BlockSpec((1, 1, D_MODEL), per_b),
        pl.BlockSpec((1, D_MODEL), const2),
        pl.BlockSpec((1, LANES), const2),
        pl.BlockSpec((1, LANES), const2),
        pl.BlockSpec((D_MODEL, _PROJ_COLS), const2, pipeline_mode=once),
        pl.BlockSpec((FOX_WIDTH, D_MODEL), const2, pipeline_mode=once),
        pl.BlockSpec((LANES, 2 * FOX_WIDTH), const2, pipeline_mode=once),
    ]
    head4 = lambda bi, i: (bi, 0, i, 0)
    out_specs = [
        pl.BlockSpec((1, tm, SWA_WIDTH), row),
        pl.BlockSpec((1, tm, 2 * SWA_KV_WIDTH), row),
        pl.BlockSpec((1, tm, 2 * SWA_KV_WIDTH), row),
        pl.BlockSpec((1, tm, SWA_WIDTH), row),
        pl.BlockSpec((1, FOX_HEADS, tm, LANES), head4),
        pl.BlockSpec((1, FOX_HEADS, tm, LANES), head4),
        pl.BlockSpec((1, FOX_WIDTH, tm), lambda bi, i: (bi, 0, i)),
        pl.BlockSpec((1, tm, FOX_WIDTH), row),
        pl.BlockSpec((1, tm, D_MODEL), row),
        pl.BlockSpec((1, tm, D_MODEL), row),
    ]
    sds = jax.ShapeDtypeStruct
    out_shape = [
        sds((b, s, SWA_WIDTH), BF16), sds((b, s, 2 * SWA_KV_WIDTH), BF16),
        sds((b, s, 2 * SWA_KV_WIDTH), BF16), sds((b, s, SWA_WIDTH), BF16),
        sds((b, FOX_HEADS, s, LANES), BF16), sds((b, FOX_HEADS, s, LANES), BF16),
        sds((b, FOX_WIDTH, s), BF16), sds((b, s, FOX_WIDTH), BF16),
        sds((b, s, D_MODEL), BF16), sds((b, s, D_MODEL), BF16),
    ]
    return pl.pallas_call(
        functools.partial(_proj_kernel, tm=tm),
        grid=grid, in_specs=in_specs, out_specs=out_specs, out_shape=out_shape,
        scratch_shapes=[pltpu.VMEM((1, LANES), F32)],
        compiler_params=pltpu.CompilerParams(
            dimension_semantics=("arbitrary", "arbitrary"),
            vmem_limit_bytes=_vmem_limit(56 * 1024 * 1024)),
        name="in_proj",
    )(x, pos_f, scale, shift, gn, invf, bf_rep, w_all, w_vbt, e_mat)


def _swa_kernel(sinks_ref, q_ref, kc_ref, kp_ref, vc_ref, vp_ref, o_ref, kband, vband, *, tq):
    i = pl.program_id(1)
    nblk = tq // WINDOW
    kband[0:WINDOW] = kp_ref[0]
    kband[WINDOW:] = kc_ref[0]
    vband[0:WINDOW] = vp_ref[0]
    vband[WINDOW:] = vc_ref[0]

    rows = SWA_Q_HEADS // SWA_KV_HEADS * WINDOW
    t = lax.broadcasted_iota(jnp.int32, (rows, 2 * WINDOW), 0) % WINDOW
    col = lax.broadcasted_iota(jnp.int32, (rows, 2 * WINDOW), 1)
    band_mask = jnp.logical_or(jnp.logical_and(col < WINDOW, col > t),
                               jnp.logical_and(col >= WINDOW, (col - WINDOW) <= t))
    first_mask = jnp.logical_and(band_mask, jnp.logical_or(col >= WINDOW, i > 0))
    rowid = lax.broadcasted_iota(jnp.int32, (rows, 1), 0) // WINDOW
    lane = lax.broadcasted_iota(jnp.int32, (WINDOW, LANES), 1)
    low = lane < HEAD_DIM

    for r in range(nblk):
        mask = first_mask if r == 0 else band_mask
        for g in range(SWA_KV_HEADS):
            kb = kband[r * WINDOW:(r + 2) * WINDOW, g * LANES:(g + 1) * LANES]
            vb = vband[r * WINDOW:(r + 2) * WINDOW, g * LANES:(g + 1) * LANES]
            qs = []
            for c in range(2):
                a = (2 * g + c) * LANES
                qc = q_ref[0, r * WINDOW:(r + 1) * WINDOW, a:a + LANES].astype(F32)
                qs.append(jnp.where(low, qc, 0.0).astype(BF16))
                qs.append(jnp.where(low, 0.0, qc).astype(BF16))
            qst = jnp.concatenate(qs, axis=0)
            s = lax.dot_general(qst, kb, (((1,), (1,)), ((), ())), preferred_element_type=F32)
            s = jnp.where(mask, s, -jnp.inf)
            sink = jnp.zeros((rows, 1), F32)
            for k in range(SWA_Q_HEADS // SWA_KV_HEADS):
                sink = jnp.where(rowid == k, sinks_ref[g * (SWA_Q_HEADS // SWA_KV_HEADS) + k], sink)
            m = jnp.maximum(jnp.max(s, axis=-1, keepdims=True), sink)
            p = jnp.exp(s - m)
            den = jnp.sum(p, axis=-1, keepdims=True) + jnp.exp(sink - m)
            o = jnp.dot((p / den).astype(BF16), vb, preferred_element_type=F32)
            for c in range(2):
                a = (2 * g + c) * LANES
                oc = jnp.where(low, o[2 * c * WINDOW:(2 * c + 1) * WINDOW],
                               o[(2 * c + 1) * WINDOW:(2 * c + 2) * WINDOW])
                o_ref[0, r * WINDOW:(r + 1) * WINDOW, a:a + LANES] = oc.astype(BF16)


def _swa(sinks, qa, ka, va, *, tq):
    b, s, _ = qa.shape
    kvw = 2 * SWA_KV_WIDTH
    per = tq // WINDOW
    cur = lambda bi, i: (bi, i, 0)
    prev = lambda bi, i: (bi, jnp.maximum(i * per - 1, 0), 0)
    return pl.pallas_call(
        functools.partial(_swa_kernel, tq=tq),
        grid=(b, s // tq),
        in_specs=[pl.BlockSpec(memory_space=pltpu.SMEM),
                  pl.BlockSpec((1, tq, SWA_WIDTH), cur),
                  pl.BlockSpec((1, tq, kvw), cur),
                  pl.BlockSpec((1, WINDOW, kvw), prev),
                  pl.BlockSpec((1, tq, kvw), cur),
                  pl.BlockSpec((1, WINDOW, kvw), prev)],
        out_specs=pl.BlockSpec((1, tq, SWA_WIDTH), cur),
        out_shape=jax.ShapeDtypeStruct((b, s, SWA_WIDTH), BF16),
        scratch_shapes=[pltpu.VMEM((tq + WINDOW, kvw), BF16), pltpu.VMEM((tq + WINDOW, kvw), BF16)],
        compiler_params=pltpu.CompilerParams(dimension_semantics=("arbitrary", "arbitrary")),
        name="swa_attn",
    )(sinks, qa, ka, ka, va, va)


_FOX_ROWS = 64


_ONES_ROWS = 16
_ACC_ROWS = HEAD_DIM + _ONES_ROWS
_PIPE = 2


def _fox_schedule(n_rows):
    seq = []
    for i in range(n_rows):
        seq.append((i, i, 1, i % 2, int(i == 0)))
        for j in range(i):
            seq.append((i, j, 0, i % 2, int(j == i - 1)))
    dummy = (0, 0, 1, 2, 0)
    n_steps = len(seq) + _PIPE
    n_steps += n_steps % 2
    tbl = [dummy] * _PIPE + seq + [dummy] * (n_steps - len(seq))
    return np.asarray(tbl, np.int32).T.copy(), n_steps


def _fox_kernel(tbl_ref, q_ref, k_ref, vt_ref, bias_ref, o_ref,
                s0, s1, mx0, mx1, p0, p1, a0, a1, m_ref, acc_ref, *, t, n_steps):
    nt = (((1,), (1,)), ((), ()))
    g = t // 8
    for ref in (s1, mx1, p0, a0, m_ref, acc_ref):
        ref[...] = jnp.zeros_like(ref)

    def step(k, s_w, mx_w, s_r, mx_r, p_w, a_w, p_r, a_r):
        qrow = pl.multiple_of(tbl_ref[0, k + 2] * t, t)
        krow = pl.multiple_of(tbl_ref[1, k + 2] * t, t)
        diag = tbl_ref[2, k + 2]
        for hh in range(2):
            s = lax.dot_general(k_ref[0, hh, pl.ds(krow, t), :], q_ref[0, hh, pl.ds(qrow, t), :], nt,
                                preferred_element_type=F32) + bias_ref[diag]
            s_w[hh] = s
            mx_w[hh] = jnp.max(s.reshape(g, 8, t), axis=0)

        first = tbl_ref[2, k + 1] == 1
        for hh in range(2):
            m_old = jnp.where(first, -jnp.inf, m_ref[hh])
            m_new = jnp.maximum(m_old, jnp.max(mx_r[hh], axis=0, keepdims=True))
            m_ref[hh] = m_new
            a_w[hh] = jnp.exp2(m_old - m_new)
            p = jnp.exp2(s_r[hh].reshape(g, 8, t) - m_new[None])
            p_w[hh] = p.reshape(t, t).astype(BF16)

        vcol = pl.multiple_of(tbl_ref[1, k] * t, t)
        asl = tbl_ref[3, k]
        for hh in range(2):
            vt = vt_ref[0, hh * HEAD_DIM:(hh + 1) * HEAD_DIM, pl.ds(vcol, t)]
            lhs = jnp.concatenate([vt, jnp.ones((_ONES_ROWS, t), BF16)], axis=0)
            part = jnp.dot(lhs, p_r[hh], preferred_element_type=F32)
            old = acc_ref[asl, hh].reshape(_ACC_ROWS // 8, 8, t)
            acc_ref[asl, hh] = (a_r[hh][None] * old).reshape(_ACC_ROWS, t) + part

    def finish_row(k):
        @pl.when(tbl_ref[4, k] == 1)
        def _():
            sl = tbl_ref[3, k]
            orow = pl.multiple_of(tbl_ref[0, k] * t, t)
            halves = []
            for hh in range(2):
                a = acc_ref[sl, hh]
                halves.append(a[0:HEAD_DIM] / a[HEAD_DIM:HEAD_DIM + 1])
            o_t = jnp.concatenate(halves, axis=0)
            o_ref[0, pl.ds(orow, t), :] = o_t.T.astype(BF16)

    def body(kk, carry):
        k = 2 * kk
        step(k, s0, mx0, s1, mx1, p1, a1, p0, a0)
        step(k + 1, s1, mx1, s0, mx0, p0, a0, p1, a1)
        finish_row(k)
        finish_row(k + 1)
        return carry

    assert n_steps % 2 == 0
    lax.fori_loop(0, n_steps // 2, body, 0)


def _fox(q_aug, k_aug, vbt, *, t):
    b, _, s, _ = q_aug.shape
    npair = FOX_HEADS // 2
    tbl, n_steps = _fox_schedule(s // t)
    tri = np.triu(np.ones((t, t), bool))
    bias = jnp.asarray(np.stack([np.zeros((t, t), np.float32),
                                 np.where(tri, 0.0, -np.inf).astype(np.float32)]))
    pair = lambda bi, p: (bi, p, 0, 0)
    stat = pltpu.VMEM((2, 8, t), F32)
    return pl.pallas_call(
        functools.partial(_fox_kernel, t=t, n_steps=n_steps),
        grid=(b, npair),
        in_specs=[pl.BlockSpec(memory_space=pltpu.SMEM),
                  pl.BlockSpec((1, 2, s, LANES), pair),
                  pl.BlockSpec((1, 2, s, LANES), pair),
                  pl.BlockSpec((1, 2 * HEAD_DIM, s), lambda bi, p: (bi, p, 0)),
                  pl.BlockSpec((2, t, t), lambda bi, p: (0, 0, 0), pipeline_mode=pl.Buffered(1))],
        out_specs=pl.BlockSpec((1, s, LANES), lambda bi, p: (bi, 0, p)),
        out_shape=jax.ShapeDtypeStruct((b, s, FOX_WIDTH), BF16),
        scratch_shapes=[pltpu.VMEM((2, t, t), F32), pltpu.VMEM((2, t, t), F32),
                        stat, stat,
                        pltpu.VMEM((2, t, t), BF16), pltpu.VMEM((2, t, t), BF16),
                        stat, stat,
                        stat,
                        pltpu.VMEM((3, 2, _ACC_ROWS, t), F32)],
        compiler_params=pltpu.CompilerParams(
            dimension_semantics=("arbitrary", "arbitrary"),
            vmem_limit_bytes=_vmem_limit(56 * 1024 * 1024)),
        name="fox_attn",
    )(jnp.asarray(tbl), q_aug, k_aug, vbt, bias)


def _out_kernel(x_ref, aa_ref, sza_ref, ab_ref, szb_ref, sga_ref, sgb_ref, gate_ref,
                woa_ref, wob_ref, wout_ref, gf_ref, o_ref):
    a = (aa_ref[0].astype(F32) * sza_ref[0].astype(F32)).astype(BF16)
    bb = (ab_ref[0].astype(F32) * szb_ref[0].astype(F32)).astype(BF16)
    ya = jnp.dot(a, woa_ref[...], preferred_element_type=F32)
    yb = jnp.dot(bb, wob_ref[...], preferred_element_type=F32)
    merged = sga_ref[0].astype(F32) * ya + sgb_ref[0].astype(F32) * yb
    d = jnp.dot(merged.astype(BF16), wout_ref[...], preferred_element_type=F32)
    xo = x_ref[0] + gate_ref[0] * d
    ms = jnp.mean(xo * xo, axis=-1, keepdims=True)
    o_ref[0] = xo * lax.rsqrt(ms + NORM_EPS) * gf_ref[...]


def _out(x, att_a, sza, att_b, szb, sga, sgb, gate, woa, wob, wout, gf, *, tm):
    b, s, _ = x.shape
    row = lambda bi, i: (bi, i, 0)
    per_b = lambda bi, i: (bi, 0, 0)
    const2 = lambda bi, i: (0, 0)
    once = pl.Buffered(1)
    return pl.pallas_call(
        _out_kernel,
        grid=(b, s // tm),
        in_specs=[pl.BlockSpec((1, tm, D_MODEL), row),
                  pl.BlockSpec((1, tm, SWA_WIDTH), row), pl.BlockSpec((1, tm, SWA_WIDTH), row),
                  pl.BlockSpec((1, tm, FOX_WIDTH), row), pl.BlockSpec((1, tm, FOX_WIDTH), row),
                  pl.BlockSpec((1, tm, D_MODEL), row), pl.BlockSpec((1, tm, D_MODEL), row),
                  pl.BlockSpec((1, 1, D_MODEL), per_b),
                  pl.BlockSpec((SWA_WIDTH, D_MODEL), const2, pipeline_mode=once),
                  pl.BlockSpec((FOX_WIDTH, D_MODEL), const2, pipeline_mode=once),
                  pl.BlockSpec((D_MODEL, D_MODEL), const2, pipeline_mode=once),
                  pl.BlockSpec((1, D_MODEL), const2)],
        out_specs=pl.BlockSpec((1, tm, D_MODEL), row),
        out_shape=jax.ShapeDtypeStruct((b, s, D_MODEL), F32),
        compiler_params=pltpu.CompilerParams(
            dimension_semantics=("arbitrary", "arbitrary"),
            vmem_limit_bytes=_vmem_limit(48 * 1024 * 1024)),
        name="out_proj",
    )(x, att_a, sza, att_b, szb, sga, sgb, gate, woa, wob, wout, gf)


def _rot_cols(w, nheads):
    w3 = w.reshape(w.shape[0], nheads, HEAD_DIM)
    return jnp.concatenate([-w3[..., HALF:], w3[..., :HALF]], axis=-1).reshape(w.shape)


def _dup_heads(w, nheads):
    w3 = w.reshape(w.shape[0], nheads, 1, HEAD_DIM)
    return jnp.broadcast_to(w3, (w.shape[0], nheads, 2, HEAD_DIM)).reshape(w.shape[0], 2 * nheads * HEAD_DIM)


def _layout_w_in(w):
    o = np.cumsum([0, SWA_WIDTH, SWA_KV_WIDTH, SWA_KV_WIDTH, SWA_WIDTH, FOX_WIDTH, FOX_WIDTH, FOX_WIDTH,
                   FOX_HEADS, FOX_WIDTH, D_MODEL, D_MODEL]).tolist()
    qa, ka, va, za, qb, kb, vb, fb, zb, ga, gb = [w[:, o[k]:o[k + 1]] for k in range(11)]
    cols = [qa, _rot_cols(qa, SWA_Q_HEADS),
            _dup_heads(ka, SWA_KV_HEADS), _dup_heads(_rot_cols(ka, SWA_KV_HEADS), SWA_KV_HEADS),
            _dup_heads(va, SWA_KV_HEADS), za, qb, kb,
            jnp.tile(fb, (1, LANES // FOX_HEADS)), zb, ga, gb]
    w_all = jnp.concatenate(cols, axis=1).astype(BF16)
    assert w_all.shape[1] == _PROJ_COLS
    return w_all, vb.T.astype(BF16)


def kernel(x, c, positions, w_ada, b_ada, g_norm, w_in, b_f, sinks, w_o_swa, w_o_fox, w_out, g_final):
    b, s, _ = x.shape
    depth = w_in.shape[0]
    assert depth == 1, "the output stage fuses the final RMSNorm into the single layer"
    inv_freq = ROPE_THETA ** (-jnp.arange(0, HEAD_DIM, 2, dtype=F32) / HEAD_DIM)
    invf = jnp.tile(inv_freq, LANES // HALF)[None, :]
    pos_f = positions.astype(F32)[..., None]
    e_mat = jnp.asarray(_aug_placement(), BF16)
    c_pad = jnp.zeros((8, D_MODEL), F32).at[:b].set(c)
    for l in range(depth):
        ada = _ada(c_pad, w_ada[l], b_ada[l][None, :])[:b]
        shift, scale, gate = [ada[:, None, k * D_MODEL:(k + 1) * D_MODEL] for k in range(3)]
        w_all, w_vbt = _layout_w_in(w_in[l])
        bf_rep = jnp.tile(b_f[l].astype(F32), LANES // FOX_HEADS)[None, :]
        qa, ka, va, sza, q_aug, k_aug, vbt, szb, sga, sgb = _proj(
            x, pos_f, scale, shift, g_norm[l][None, :], invf, bf_rep, w_all, w_vbt, e_mat, tm=512)
        att_a = _swa(sinks[l].astype(F32), qa, ka, va, tq=512)
        att_b = _fox(q_aug, k_aug, vbt, t=512)
        x = _out(x, att_a, sza, att_b, szb, sga, sgb, gate,
                 w_o_swa[l].astype(BF16), w_o_fox[l].astype(BF16), w_out[l].astype(BF16),
                 g_final[None, :], tm=512)
    return x
```

```python
import functools
import math

import numpy as np
import jax
import jax.numpy as jnp
from jax import lax
from jax.experimental import pallas as pl
from jax.experimental.pallas import tpu as pltpu

D_MODEL = 1024
HEAD_DIM = 64
HALF = HEAD_DIM // 2
SWA_Q_HEADS = 8
SWA_KV_HEADS = 2
SWA_WIDTH = SWA_Q_HEADS * HEAD_DIM
SWA_KV_WIDTH = SWA_KV_HEADS * HEAD_DIM
FOX_HEADS = 8
FOX_WIDTH = FOX_HEADS * HEAD_DIM
WINDOW = 128
ROPE_THETA = 10000.0
NORM_EPS = 1e-6
QK_SCALE = HEAD_DIM ** -0.5
LOG2E = math.log2(math.e)

LANES = 128
V7X_VMEM_BYTES = 64 * 1024 * 1024

_PROJ_TM = 512
_SWA_TQ = 512
_FOX_T = 512
_OUT_TM = 512

F32 = jnp.float32
BF16 = jnp.bfloat16

_QA, _KA, _ZA = 0, 1024, 1536
_QB, _KB, _FB, _ZB, _GA, _GB = 2048, 2560, 3072, 3200, 3712, 4736
_PROJ_COLS = 5760

_ONE_LANE = 24


def _vmem_limit(nbytes):
    return int(min(nbytes, V7X_VMEM_BYTES - 8 * 1024 * 1024))


def _split3(v):
    hi = v.astype(BF16)
    r1 = v - hi.astype(F32)
    mid = r1.astype(BF16)
    lo = (r1 - mid.astype(F32)).astype(BF16)
    return hi, mid, lo


def _ada_kernel(c_ref, w_ref, b_ref, o_ref):
    o_ref[...] = jnp.dot(c_ref[...], w_ref[...], preferred_element_type=F32,
                         precision=lax.Precision.HIGHEST) + b_ref[...]


def _ada(c_pad, w_ada, b_ada):
    rows = c_pad.shape[0]
    n = w_ada.shape[1]
    nblk = n // D_MODEL
    return pl.pallas_call(
        _ada_kernel,
        grid=(nblk,),
        in_specs=[pl.BlockSpec((rows, D_MODEL), lambda j: (0, 0)),
                  pl.BlockSpec((D_MODEL, D_MODEL), lambda j: (0, j)),
                  pl.BlockSpec((1, D_MODEL), lambda j: (0, j))],
        out_specs=pl.BlockSpec((rows, D_MODEL), lambda j: (0, j)),
        out_shape=jax.ShapeDtypeStruct((rows, n), F32),
        name="ada_mod",
    )(c_pad, w_ada, b_ada)


def _sigmoid(z):
    return 0.5 * jnp.tanh(0.5 * z) + 0.5


def _proj_kernel(x_ref, pos_ref, scale_ref, shift_ref, gn_ref, invf_ref, bf_ref, w_ref, wvt_ref, e_ref,
                 qa_ref, ka_ref, vat_ref, sza_ref, qaug_ref, kaug_ref, vbt_ref, szb_ref,
                 sga_ref, sgb_ref, carry_ref, *, tm):
    i = pl.program_id(1)

    @pl.when(i == 0)
    def _():
        carry_ref[...] = jnp.zeros_like(carry_ref)

    x = x_ref[0]
    ms = jnp.mean(x * x, axis=-1, keepdims=True)
    y = x * lax.rsqrt(ms + NORM_EPS) * gn_ref[...]
    h = y * (1.0 + scale_ref[0]) + shift_ref[0]
    hb = h.astype(BF16)

    def proj(off, n):
        return jnp.dot(hb, w_ref[:, off:off + n], preferred_element_type=F32)

    lane = lax.broadcasted_iota(jnp.int32, (tm, LANES), 1)
    low = lane < HEAD_DIM


    zf = proj(_FB, LANES) + bf_ref[...]
    logf = jnp.minimum(zf, 0.0) - jnp.log1p(jnp.exp(-jnp.abs(zf)))
    grp = lane // FOX_HEADS
    hi, mid, lo = _split3(logf)
    zero = jnp.zeros((tm, LANES), F32)
    parts = jnp.where(grp == 0, hi.astype(F32),
                      jnp.where(grp == 1, mid.astype(F32),
                                jnp.where(grp == 2, lo.astype(F32), zero))).astype(BF16)
    tri = (lax.broadcasted_iota(jnp.int32, (tm, tm), 0)
           >= lax.broadcasted_iota(jnp.int32, (tm, tm), 1)).astype(BF16)
    rsum = jnp.dot(tri, parts, preferred_element_type=F32)
    c0 = rsum + pltpu.roll(rsum, LANES - FOX_HEADS, 1) + pltpu.roll(rsum, LANES - 2 * FOX_HEADS, 1)
    c0 = jnp.where(grp == 0, c0, zero)
    cl = c0 + pltpu.roll(c0, FOX_HEADS, 1) + pltpu.roll(c0, 2 * FOX_HEADS, 1)
    cum = cl + carry_ref[...]
    carry_ref[...] = cum[tm - 1:tm, :]

    hi, mid, lo = _split3(cum * LOG2E)
    one =jnp.ones((tm, LANES), F32)
    carrier = jnp.where(grp == 0, hi.astype(F32),
                        jnp.where(grp == 1, mid.astype(F32),
                                  jnp.where(grp == 2, lo.astype(F32),
                                            jnp.where(lane == _ONE_LANE, one, zero)))).astype(BF16)
    aug = jnp.dot(carrier, e_ref[...], preferred_element_type=F32)

    sga_ref[0] = _sigmoid(proj(_GA, D_MODEL)).astype(BF16)
    sgb_ref[0] = _sigmoid(proj(_GB, D_MODEL)).astype(BF16)
    z = proj(_ZA, SWA_WIDTH)
    sza_ref[0] = (z * _sigmoid(z)).astype(BF16)
    z = proj(_ZB, FOX_WIDTH)
    szb_ref[0] = (z * _sigmoid(z)).astype(BF16)

    ang = invf_ref[...] * pos_ref[0]
    cos = jnp.tile(jnp.cos(ang), (LANES // HALF, 1)).T
    sin = jnp.tile(jnp.sin(ang), (LANES // HALF, 1)).T
    r = proj(_QA, 2 * SWA_WIDTH)
    for c in range(SWA_WIDTH // LANES):
        a = c * LANES
        t = r[:, a:a + LANES] * cos + r[:, SWA_WIDTH + a:SWA_WIDTH + a + LANES] * sin
        qa_ref[0, :, a:a + LANES] = (t * (QK_SCALE * LOG2E)).astype(BF16)
    r = proj(_KA, 4 * SWA_KV_WIDTH)
    for c in range(2 * SWA_KV_WIDTH // LANES):
        a = c * LANES
        t = r[:, a:a + LANES] * cos + r[:, 2 * SWA_KV_WIDTH + a:2 * SWA_KV_WIDTH + a + LANES] * sin
        ka_ref[0, :, a:a + LANES] = t.astype(BF16)

    rq = proj(_QB, FOX_WIDTH)
    rk = proj(_KB, FOX_WIDTH)
    for hd in range(FOX_HEADS):
        a = (hd // 2) * LANES
        gq = aug[:, a:a + LANES]
        gk = aug[:, FOX_WIDTH + a:FOX_WIDTH + a + LANES]
        qv = rq[:, a:a + LANES] * (QK_SCALE * LOG2E)
        kv = rk[:, a:a + LANES]
        if hd % 2 == 0:
            qaug_ref[0, hd] = jnp.where(low, qv, gq).astype(BF16)
            kaug_ref[0, hd] = jnp.where(low, kv, gk).astype(BF16)
        else:
            qaug_ref[0, hd] = jnp.where(low, gq, qv).astype(BF16)
            kaug_ref[0, hd] = jnp.where(low, gk, kv).astype(BF16)

    vt = lax.dot_general(wvt_ref[...], hb, (((1,), (1,)), ((), ())), preferred_element_type=F32)
    vat_ref[0] = vt[0:SWA_KV_WIDTH].astype(BF16)
    vbt_ref[0] = vt[SWA_KV_WIDTH:].astype(BF16)


def _aug_placement():
    e = np.zeros((LANES, 2 * FOX_HEADS * HEAD_DIM), np.float32)
    for hd in range(FOX_HEADS):
        base = (hd // 2) * LANES + (HEAD_DIM if hd % 2 == 0 else 0)
        kbase = FOX_HEADS * HEAD_DIM + base
        for part in range(3):
            e[part * FOX_HEADS + hd, base + part] = 1.0
            e[_ONE_LANE, base + 3 + part] = 1.0
            e[_ONE_LANE, kbase + part] = 1.0
            e[part * FOX_HEADS + hd, kbase + 3 + part] = -1.0
    return e


def _proj(x, pos_f, scale, shift, gn, invf, bf_rep, w_all, w_vbt, e_mat, *, tm):
    b, s, _ = x.shape
    grid = (b, s // tm)
    row = lambda bi, i: (bi, i, 0)
    per_b = lambda bi, i: (bi, 0, 0)
    const2 = lambda bi, i: (0, 0)
    once = pl.Buffered(1)
    in_specs = [
        pl.BlockSpec((1, tm, D_MODEL), row),
        pl.BlockSpec((1, 1, tm), lambda bi, i: (bi, 0, i)),
        pl.BlockSpec((1, 1, D_MODEL), per_b),
        pl.BlockSpec((1, 1, D_MODEL), per_b),
        pl.BlockSpec((1, D_MODEL), const2),
        pl.BlockSpec((HALF, tm), const2),
        pl.BlockSpec((1, LANES), const2),
        pl.BlockSpec((D_MODEL, _PROJ_COLS), const2, pipeline_mode=once),
        pl.BlockSpec((SWA_KV_WIDTH + FOX_WIDTH, D_MODEL), const2, pipeline_mode=once),
        pl.BlockSpec((LANES, 2 * FOX_WIDTH), const2, pipeline_mode=once),
    ]
    head4 = lambda bi, i: (bi, 0, i, 0)
    out_specs = [
        pl.BlockSpec((1, tm, SWA_WIDTH), row),
        pl.BlockSpec((1, tm, 2 * SWA_KV_WIDTH), row),
        pl.BlockSpec((1, SWA_KV_WIDTH, tm), lambda bi, i: (bi, 0, i)),
        pl.BlockSpec((1, tm, SWA_WIDTH), row),
        pl.BlockSpec((1, FOX_HEADS, tm, LANES), head4),
        pl.BlockSpec((1, FOX_HEADS, tm, LANES), head4),
        pl.BlockSpec((1, FOX_WIDTH, tm), lambda bi, i: (bi, 0, i)),
        pl.BlockSpec((1, tm, FOX_WIDTH), row),
        pl.BlockSpec((1, tm, D_MODEL), row),
        pl.BlockSpec((1, tm, D_MODEL), row),
    ]
    sds = jax.ShapeDtypeStruct
    out_shape = [
        sds((b, s, SWA_WIDTH), BF16), sds((b, s, 2 * SWA_KV_WIDTH), BF16),
        sds((b, SWA_KV_WIDTH, s), BF16), sds((b, s, SWA_WIDTH), BF16),
        sds((b, FOX_HEADS, s, LANES), BF16), sds((b, FOX_HEADS, s, LANES), BF16),
        sds((b, FOX_WIDTH, s), BF16), sds((b, s, FOX_WIDTH), BF16),
        sds((b, s, D_MODEL), BF16), sds((b, s, D_MODEL), BF16),
    ]
    return pl.pallas_call(
        functools.partial(_proj_kernel, tm=tm),
        grid=grid, in_specs=in_specs, out_specs=out_specs, out_shape=out_shape,
        scratch_shapes=[pltpu.VMEM((1, LANES), F32)],
        compiler_params=pltpu.CompilerParams(
            dimension_semantics=("arbitrary", "arbitrary"),
            vmem_limit_bytes=_vmem_limit(56 * 1024 * 1024)),
        name="in_proj",
    )(x, pos_f, scale, shift, gn, invf, bf_rep, w_all, w_vbt, e_mat)


def _swa_kernel(sinks_ref, q_ref, kc_ref, kp_ref, vtc_ref, vtp_ref, bias_ref, o_ref, kband, vtband, *, tq):
    i = pl.program_id(1)
    nblk = tq // WINDOW
    nhq = SWA_Q_HEADS // SWA_KV_HEADS
    kband[0:WINDOW] = kp_ref[0]
    kband[WINDOW:] = kc_ref[0]
    vtband[:, 0:WINDOW] = vtp_ref[0]
    vtband[:, WINDOW:] = vtc_ref[0]

    cols = nhq * WINDOW
    first_plane = jnp.where(i > 0, 0, 1)
    headid = lax.broadcasted_iota(jnp.int32, (1, cols), 1) // WINDOW
    low = lax.broadcasted_iota(jnp.int32, (WINDOW, LANES), 1) < HEAD_DIM
    ones = jnp.ones((_ONES_ROWS, 2 * WINDOW), BF16)
    nt = (((1,), (1,)), ((), ()))

    def logits(r, g):
        kb = kband[r * WINDOW:(r + 2) * WINDOW, g * LANES:(g + 1) * LANES]
        qs = []
        for c in range(2):
            a = (2 * g + c) * LANES
            qc = q_ref[0, r * WINDOW:(r + 1) * WINDOW, a:a + LANES]
            qs.append(jnp.where(low, qc, jnp.zeros_like(qc)))
            qs.append(jnp.where(low, jnp.zeros_like(qc), qc))
        qst = jnp.concatenate(qs, axis=0)
        s = lax.dot_general(kb, qst, nt, preferred_element_type=F32)
        return s + bias_ref[first_plane if r == 0 else 0]

    def attend(r, g, s):
        sink = jnp.zeros((1, cols), F32)
        for k in range(nhq):
            sink = jnp.where(headid == k, sinks_ref[g * nhq + k] * LOG2E, sink)
        m = jnp.maximum(jnp.max(s, axis=0, keepdims=True), sink)
        p = jnp.exp2(s - m).astype(BF16)
        vt = vtband[g * HEAD_DIM:(g + 1) * HEAD_DIM, r * WINDOW:(r + 2) * WINDOW]
        pv = jnp.dot(jnp.concatenate([vt, ones], axis=0), p, preferred_element_type=F32)
        den = pv[HEAD_DIM:HEAD_DIM + 1] + jnp.exp2(sink - m)
        o_t = pv[0:HEAD_DIM] / den
        for c in range(2):
            a = (2 * g + c) * LANES
            pair = jnp.concatenate([o_t[:, 2 * c * WINDOW:(2 * c + 1) * WINDOW],
                                    o_t[:, (2 * c + 1) * WINDOW:(2 * c + 2) * WINDOW]], axis=0)
            o_ref[0, r * WINDOW:(r + 1) * WINDOW, a:a + LANES] = pair.T.astype(BF16)

    work = [(r, g) for r in range(nblk) for g in range(SWA_KV_HEADS)]
    s_next = logits(*work[0])
    for n, (r, g) in enumerate(work):
        s_cur = s_next
        if n + 1 < len(work):
            s_next = logits(*work[n + 1])
        attend(r, g, s_cur)


def _swa(sinks, qa, ka, vat, *, tq):
    b, s, _ = qa.shape
    kvw = 2 * SWA_KV_WIDTH
    per = tq // WINDOW
    cur = lambda bi, i: (bi, i, 0)
    prev = lambda bi, i: (bi, jnp.maximum(i * per - 1, 0), 0)
    cur_t = lambda bi, i: (bi, 0, i)
    prev_t = lambda bi, i: (bi, 0, jnp.maximum(i * per - 1, 0))
    key = np.arange(2 * WINDOW)[:, None]
    qt = np.tile(np.arange(WINDOW), SWA_Q_HEADS // SWA_KV_HEADS)[None, :]
    band = np.where(key < WINDOW, key > qt, (key - WINDOW) <= qt)
    planes = np.stack([band, band & (key >= WINDOW)])
    bias = jnp.asarray(np.where(planes, 0.0, -np.inf).astype(np.float32))
    return pl.pallas_call(
        functools.partial(_swa_kernel, tq=tq),
        grid=(b, s // tq),
        in_specs=[pl.BlockSpec(memory_space=pltpu.SMEM),
                  pl.BlockSpec((1, tq, SWA_WIDTH), cur),
                  pl.BlockSpec((1, tq, kvw), cur),
                  pl.BlockSpec((1, WINDOW, kvw), prev),
                  pl.BlockSpec((1, SWA_KV_WIDTH, tq), cur_t),
                  pl.BlockSpec((1, SWA_KV_WIDTH, WINDOW), prev_t),
                  pl.BlockSpec(bias.shape, lambda bi, i: (0, 0, 0), pipeline_mode=pl.Buffered(1))],
        out_specs=pl.BlockSpec((1, tq, SWA_WIDTH), cur),
        out_shape=jax.ShapeDtypeStruct((b, s, SWA_WIDTH), BF16),
        scratch_shapes=[pltpu.VMEM((tq + WINDOW, kvw), BF16),
                        pltpu.VMEM((SWA_KV_WIDTH, tq + WINDOW), BF16)],
        compiler_params=pltpu.CompilerParams(dimension_semantics=("arbitrary", "arbitrary")),
        name="swa_attn",
    )(sinks, qa, ka, ka, vat, vat, bias)


_FOX_ROWS = 64


_ONES_ROWS = 16
_ACC_ROWS = HEAD_DIM + _ONES_ROWS
_PIPE = 2
_FOX_UNROLL = 4
_ACC_SLOTS = 3


def _fox_schedule(n_rows):
    seq = []
    for i in range(n_rows):
        seq.append((i, i, 1, i % _ACC_SLOTS, int(i == 0)))
        for j in range(i):
            seq.append((i, j, 0, i % _ACC_SLOTS, int(j == i - 1)))
    dummy = (0, 0, 1, _ACC_SLOTS, 0)
    n_steps = len(seq) + _PIPE
    n_steps += -n_steps % _FOX_UNROLL
    tbl = [dummy] * _PIPE + seq + [dummy] * (n_steps - len(seq))
    return np.asarray(tbl, np.int32).T.copy(), n_steps


def _fox_kernel(tbl_ref, q_ref, k_ref, vt_ref, bias_ref, o_ref,
                s0, s1, mx0, mx1, p0, p1, a0, a1, m_ref, acc_ref, *, t, n_steps):
    nt = (((1,), (1,)), ((), ()))
    g = t // 8
    for ref in (s1, mx1, p0, a0, m_ref, acc_ref):
        ref[...] = jnp.zeros_like(ref)

    def step(k, s_w, mx_w, s_r, mx_r, p_w, a_w, p_r, a_r):
        qrow = pl.multiple_of(tbl_ref[0, k + 2] * t, t)
        krow = pl.multiple_of(tbl_ref[1, k + 2] * t, t)
        diag = tbl_ref[2, k + 2]
        for hh in range(2):
            s = lax.dot_general(k_ref[0, hh, pl.ds(krow, t), :], q_ref[0, hh, pl.ds(qrow, t), :], nt,
                                preferred_element_type=F32) + bias_ref[diag]
            s_w[hh] = s
            mx_w[hh] = jnp.max(s.reshape(g, 8, t), axis=0)

        first = tbl_ref[2, k + 1] == 1
        for hh in range(2):
            m_old = jnp.where(first, -jnp.inf, m_ref[hh])
            m_new = jnp.maximum(m_old, jnp.max(mx_r[hh], axis=0, keepdims=True))
            m_ref[hh] = m_new
            a_w[hh] = jnp.exp2(m_old - m_new)
            p = jnp.exp2(s_r[hh].reshape(g, 8, t) - m_new[None])
            p_w[hh] = p.reshape(t, t).astype(BF16)

        vcol = pl.multiple_of(tbl_ref[1, k] * t, t)
        asl = tbl_ref[3, k]
        for hh in range(2):
            vt = vt_ref[0, hh * HEAD_DIM:(hh + 1) * HEAD_DIM, pl.ds(vcol, t)]
            lhs = jnp.concatenate([vt, jnp.ones((_ONES_ROWS, t), BF16)], axis=0)
            part = jnp.dot(lhs, p_r[hh], preferred_element_type=F32)
            old = acc_ref[asl, hh].reshape(_ACC_ROWS // 8, 8, t)
            acc_ref[asl, hh] = (a_r[hh][None] * old).reshape(_ACC_ROWS, t) + part

    def finish_row(k):
        @pl.when(tbl_ref[4, k] == 1)
        def _():
            sl = tbl_ref[3, k]
            orow = pl.multiple_of(tbl_ref[0, k] * t, t)
            halves = []
            for hh in range(2):
                a = acc_ref[sl, hh]
                halves.append(a[0:HEAD_DIM] / a[HEAD_DIM:HEAD_DIM + 1])
            o_t = jnp.concatenate(halves, axis=0)
            o_ref[0, pl.ds(orow, t), :] = o_t.T.astype(BF16)

    def body(kk, carry):
        k = _FOX_UNROLL * kk
        for u in range(0, _FOX_UNROLL, 2):
            step(k + u, s0, mx0, s1, mx1, p1, a1, p0, a0)
            step(k + u + 1, s1, mx1, s0, mx0, p0, a0, p1, a1)
        for u in range(_FOX_UNROLL):
            finish_row(k + u)
        return carry

    assert n_steps % _FOX_UNROLL == 0 and _FOX_UNROLL % 2 == 0
    lax.fori_loop(0, n_steps // _FOX_UNROLL, body, 0)


def _fox(q_aug, k_aug, vbt, *, t):
    b, _, s, _ = q_aug.shape
    npair = FOX_HEADS // 2
    tbl, n_steps = _fox_schedule(s // t)
    tri = np.triu(np.ones((t, t), bool))
    bias = jnp.asarray(np.stack([np.zeros((t, t), np.float32),
                                 np.where(tri, 0.0, -np.inf).astype(np.float32)]))
    pair = lambda bi, p: (bi, p, 0, 0)
    stat = pltpu.VMEM((2, 8, t), F32)
    return pl.pallas_call(
        functools.partial(_fox_kernel, t=t, n_steps=n_steps),
        grid=(b, npair),
        in_specs=[pl.BlockSpec(memory_space=pltpu.SMEM),
                  pl.BlockSpec((1, 2, s, LANES), pair),
                  pl.BlockSpec((1, 2, s, LANES), pair),
                  pl.BlockSpec((1, 2 * HEAD_DIM, s), lambda bi, p: (bi, p, 0)),
                  pl.BlockSpec((2, t, t), lambda bi, p: (0, 0, 0), pipeline_mode=pl.Buffered(1))],
        out_specs=pl.BlockSpec((1, s, LANES), lambda bi, p: (bi, 0, p)),
        out_shape=jax.ShapeDtypeStruct((b, s, FOX_WIDTH), BF16),
        scratch_shapes=[pltpu.VMEM((2, t, t), F32), pltpu.VMEM((2, t, t), F32),
                        stat, stat,
                        pltpu.VMEM((2, t, t), BF16), pltpu.VMEM((2, t, t), BF16),
                        stat, stat,
                        stat,
                        pltpu.VMEM((_ACC_SLOTS + 1, 2, _ACC_ROWS, t), F32)],
        compiler_params=pltpu.CompilerParams(
            dimension_semantics=("arbitrary", "arbitrary"),
            vmem_limit_bytes=_vmem_limit(56 * 1024 * 1024)),
        name="fox_attn",
    )(jnp.asarray(tbl), q_aug, k_aug, vbt, bias)


def _out_kernel(x_ref, aa_ref, sza_ref, ab_ref, szb_ref, sga_ref, sgb_ref, gate_ref,
                woa_ref, wob_ref, wout_ref, gf_ref, o_ref):
    a = (aa_ref[0].astype(F32) * sza_ref[0].astype(F32)).astype(BF16)
    bb = (ab_ref[0].astype(F32) * szb_ref[0].astype(F32)).astype(BF16)
    ya = jnp.dot(a, woa_ref[...], preferred_element_type=F32)
    yb = jnp.dot(bb, wob_ref[...], preferred_element_type=F32)
    merged = sga_ref[0].astype(F32) * ya + sgb_ref[0].astype(F32) * yb
    d = jnp.dot(merged.astype(BF16), wout_ref[...], preferred_element_type=F32)
    xo = x_ref[0] + gate_ref[0] * d
    ms = jnp.mean(xo * xo, axis=-1, keepdims=True)
    o_ref[0] = xo * lax.rsqrt(ms + NORM_EPS) * gf_ref[...]


def _out(x, att_a, sza, att_b, szb, sga, sgb, gate, woa, wob, wout, gf, *, tm):
    b, s, _ = x.shape
    row = lambda bi, i: (bi, i, 0)
    per_b = lambda bi, i: (bi, 0, 0)
    const2 = lambda bi, i: (0, 0)
    once = pl.Buffered(1)
    return pl.pallas_call(
        _out_kernel,
        grid=(b, s // tm),
        in_specs=[pl.BlockSpec((1, tm, D_MODEL), row),
                  pl.BlockSpec((1, tm, SWA_WIDTH), row), pl.BlockSpec((1, tm, SWA_WIDTH), row),
                  pl.BlockSpec((1, tm, FOX_WIDTH), row), pl.BlockSpec((1, tm, FOX_WIDTH), row),
                  pl.BlockSpec((1, tm, D_MODEL), row), pl.BlockSpec((1, tm, D_MODEL), row),
                  pl.BlockSpec((1, 1, D_MODEL), per_b),
                  pl.BlockSpec((SWA_WIDTH, D_MODEL), const2, pipeline_mode=once),
                  pl.BlockSpec((FOX_WIDTH, D_MODEL), const2, pipeline_mode=once),
                  pl.BlockSpec((D_MODEL, D_MODEL), const2, pipeline_mode=once),
                  pl.BlockSpec((1, D_MODEL), const2)],
        out_specs=pl.BlockSpec((1, tm, D_MODEL), row),
        out_shape=jax.ShapeDtypeStruct((b, s, D_MODEL), F32),
        compiler_params=pltpu.CompilerParams(
            dimension_semantics=("arbitrary", "arbitrary"),
            vmem_limit_bytes=_vmem_limit(48 * 1024 * 1024)),
        name="out_proj",
    )(x, att_a, sza, att_b, szb, sga, sgb, gate, woa, wob, wout, gf)


def _rot_cols(w, nheads):
    w3 = w.reshape(w.shape[0], nheads, HEAD_DIM)
    return jnp.concatenate([-w3[..., HALF:], w3[..., :HALF]], axis=-1).reshape(w.shape)


def _dup_heads(w, nheads):
    w3 = w.reshape(w.shape[0], nheads, 1, HEAD_DIM)
    return jnp.broadcast_to(w3, (w.shape[0], nheads, 2, HEAD_DIM)).reshape(w.shape[0], 2 * nheads * HEAD_DIM)


def _layout_w_in(w):
    o = np.cumsum([0, SWA_WIDTH, SWA_KV_WIDTH, SWA_KV_WIDTH, SWA_WIDTH, FOX_WIDTH, FOX_WIDTH, FOX_WIDTH,
                   FOX_HEADS, FOX_WIDTH, D_MODEL, D_MODEL]).tolist()
    qa, ka, va, za, qb, kb, vb, fb, zb, ga, gb = [w[:, o[k]:o[k + 1]] for k in range(11)]
    cols = [qa, _rot_cols(qa, SWA_Q_HEADS),
            _dup_heads(ka, SWA_KV_HEADS), _dup_heads(_rot_cols(ka, SWA_KV_HEADS), SWA_KV_HEADS),
            za, qb, kb, jnp.tile(fb, (1, LANES // FOX_HEADS)), zb, ga, gb]
    w_all = jnp.concatenate(cols, axis=1).astype(BF16)
    assert w_all.shape[1] == _PROJ_COLS
    return w_all, jnp.concatenate([va, vb], axis=1).T.astype(BF16)


def kernel(x, c, positions, w_ada, b_ada, g_norm, w_in, b_f, sinks, w_o_swa, w_o_fox, w_out, g_final):
    b, s, _ = x.shape
    depth = w_in.shape[0]
    assert depth == 1, "the output stage fuses the final RMSNorm into the single layer"
    inv_freq = ROPE_THETA ** (-jnp.arange(0, HEAD_DIM, 2, dtype=F32) / HEAD_DIM)
    invf = jnp.broadcast_to(inv_freq[:, None], (HALF, _PROJ_TM))
    pos_f = positions.astype(F32)[:, None, :]
    e_mat = jnp.asarray(_aug_placement(), BF16)
    c_pad = jnp.zeros((8, D_MODEL), F32).at[:b].set(c)
    for l in range(depth):
        ada = _ada(c_pad, w_ada[l], b_ada[l][None, :])[:b]
        shift, scale, gate = [ada[:, None, k * D_MODEL:(k + 1) * D_MODEL] for k in range(3)]
        w_all, w_vt = _layout_w_in(w_in[l])
        bf_rep = jnp.tile(b_f[l].astype(F32), LANES // FOX_HEADS)[None, :]
        qa, ka, vat, sza, q_aug, k_aug, vbt, szb, sga, sgb = _proj(
            x, pos_f, scale, shift, g_norm[l][None, :], invf, bf_rep, w_all, w_vt, e_mat, tm=_PROJ_TM)
        att_a = _swa(sinks[l].astype(F32), qa, ka, vat, tq=_SWA_TQ)
        att_b = _fox(q_aug, k_aug, vbt, t=_FOX_T)
        x = _out(x, att_a, sza, att_b, szb, sga, sgb, gate,
                 w_o_swa[l].astype(BF16), w_o_fox[l].astype(BF16), w_out[l].astype(BF16),
                 g_final[None, :], tm=_OUT_TM)
    return x
```

```python
import functools
import math

import numpy as np
import jax
import jax.numpy as jnp
from jax import lax
from jax.experimental import pallas as pl
from jax.experimental.pallas import tpu as pltpu

D_MODEL = 1024
HEAD_DIM = 64
HALF = HEAD_DIM // 2
SWA_Q_HEADS = 8
SWA_KV_HEADS = 2
SWA_WIDTH = SWA_Q_HEADS * HEAD_DIM
SWA_KV_WIDTH = SWA_KV_HEADS * HEAD_DIM
FOX_HEADS = 8
FOX_WIDTH = FOX_HEADS * HEAD_DIM
WINDOW = 128
ROPE_THETA = 10000.0
NORM_EPS = 1e-6
QK_SCALE = HEAD_DIM ** -0.5
LOG2E = math.log2(math.e)

LANES = 128
V7X_VMEM_BYTES = 64 * 1024 * 1024

_PROJ_TM = 512
_SWA_TQ = 512
_FOX_T = 512
_OUT_TM = 512

F32 = jnp.float32
BF16 = jnp.bfloat16

_QA, _KA, _ZA = 0, 1024, 1536
_QB, _KB, _FB, _ZB, _GA, _GB = 2048, 2560, 3072, 3200, 3712, 4736
_PROJ_COLS = 5760

_ONE_LANE = 24
_BLK_LANE = 25
_MASK_BLOCKS = 4
_MASK_OFF = 6
_MASK_BIG = 2.0 ** 100

_ONES_ROWS = 16


def _vmem_limit(nbytes):
    return int(min(nbytes, V7X_VMEM_BYTES - 8 * 1024 * 1024))


def _split3(v):
    hi = v.astype(BF16)
    r1 = v - hi.astype(F32)
    mid = r1.astype(BF16)
    lo = (r1 - mid.astype(F32)).astype(BF16)
    return hi, mid, lo


def _ada_kernel(c_ref, w_ref, b_ref, o_ref):
    o_ref[...] = jnp.dot(c_ref[...], w_ref[...], preferred_element_type=F32,
                         precision=lax.Precision.HIGHEST) + b_ref[...]


def _ada(c_pad, w_ada, b_ada):
    rows = c_pad.shape[0]
    n = w_ada.shape[1]
    nblk = n // D_MODEL
    return pl.pallas_call(
        _ada_kernel,
        grid=(nblk,),
        in_specs=[pl.BlockSpec((rows, D_MODEL), lambda j: (0, 0)),
                  pl.BlockSpec((D_MODEL, D_MODEL), lambda j: (0, j)),
                  pl.BlockSpec((1, D_MODEL), lambda j: (0, j))],
        out_specs=pl.BlockSpec((rows, D_MODEL), lambda j: (0, j)),
        out_shape=jax.ShapeDtypeStruct((rows, n), F32),
        name="ada_mod",
    )(c_pad, w_ada, b_ada)


def _sigmoid(z):
    return 0.5 * jnp.tanh(0.5 * z) + 0.5


def _proj_kernel(x_ref, pos_ref, scale_ref, shift_ref, gn_ref, invf_ref, bf_ref, w_ref, wvt_ref, e_ref,
                 qa_ref, ka_ref, vat_ref, sza_ref, qaug_ref, kaug_ref, vbt_ref, szb_ref,
                 sga_ref, sgb_ref, carry_ref, *, tm):
    i = pl.program_id(1)

    @pl.when(i == 0)
    def _():
        carry_ref[...] = jnp.zeros_like(carry_ref)

    x = x_ref[0]
    ms = jnp.mean(x * x, axis=-1, keepdims=True)
    y = x * lax.rsqrt(ms + NORM_EPS) * gn_ref[...]
    h = y * (1.0 + scale_ref[0]) + shift_ref[0]
    hb = h.astype(BF16)

    def proj(off, n):
        return jnp.dot(hb, w_ref[:, off:off + n], preferred_element_type=F32)

    lane = lax.broadcasted_iota(jnp.int32, (tm, LANES), 1)
    low = lane < HEAD_DIM


    zf = proj(_FB, LANES) + bf_ref[...]
    logf = jnp.minimum(zf, 0.0) - jnp.log1p(jnp.exp(-jnp.abs(zf)))
    grp = lane // FOX_HEADS
    hi, mid, lo = _split3(logf)
    zero = jnp.zeros((tm, LANES), F32)
    parts = jnp.where(grp == 0, hi.astype(F32),
                      jnp.where(grp == 1, mid.astype(F32),
                                jnp.where(grp == 2, lo.astype(F32), zero))).astype(BF16)
    tri = (lax.broadcasted_iota(jnp.int32, (tm, tm), 0)
           >= lax.broadcasted_iota(jnp.int32, (tm, tm), 1)).astype(BF16)
    rsum = jnp.dot(tri, parts, preferred_element_type=F32)
    c0 = rsum + pltpu.roll(rsum, LANES - FOX_HEADS, 1) + pltpu.roll(rsum, LANES - 2 * FOX_HEADS, 1)
    c0 = jnp.where(grp == 0, c0, zero)
    cl = c0 + pltpu.roll(c0, FOX_HEADS, 1) + pltpu.roll(c0, 2 * FOX_HEADS, 1)
    cum = cl + carry_ref[...]
    carry_ref[...] = cum[tm - 1:tm, :]

    hi, mid, lo = _split3(cum * LOG2E)
    rowblk = lax.broadcasted_iota(jnp.int32, (tm, LANES), 0) // (tm // _MASK_BLOCKS)
    one = jnp.ones((tm, LANES), F32)
    carrier = jnp.where(grp == 0, hi.astype(F32),
                        jnp.where(grp == 1, mid.astype(F32),
                                  jnp.where(grp == 2, lo.astype(F32),
                                            jnp.where(jnp.logical_or(lane == _ONE_LANE,
                                                                     lane == _BLK_LANE + rowblk),
                                                      one, zero)))).astype(BF16)
    aug = jnp.dot(carrier, e_ref[...], preferred_element_type=F32)

    sga_ref[0] = _sigmoid(proj(_GA, D_MODEL)).astype(BF16)
    sgb_ref[0] = _sigmoid(proj(_GB, D_MODEL)).astype(BF16)
    z = proj(_ZA, SWA_WIDTH)
    sza_ref[0] = (z * _sigmoid(z)).astype(BF16)
    z = proj(_ZB, FOX_WIDTH)
    szb_ref[0] = (z * _sigmoid(z)).astype(BF16)

    ang = invf_ref[...] * pos_ref[0]
    cos = jnp.tile(jnp.cos(ang), (LANES // HALF, 1)).T
    sin = jnp.tile(jnp.sin(ang), (LANES // HALF, 1)).T
    r = proj(_QA, 2 * SWA_WIDTH)
    for c in range(SWA_WIDTH // LANES):
        a = c * LANES
        t = r[:, a:a + LANES] * cos + r[:, SWA_WIDTH + a:SWA_WIDTH + a + LANES] * sin
        qa_ref[0, :, a:a + LANES] = (t * (QK_SCALE * LOG2E)).astype(BF16)
    r = proj(_KA, 4 * SWA_KV_WIDTH)
    for c in range(2 * SWA_KV_WIDTH // LANES):
        a = c * LANES
        t = r[:, a:a + LANES] * cos + r[:, 2 * SWA_KV_WIDTH + a:2 * SWA_KV_WIDTH + a + LANES] * sin
        ka_ref[0, :, a:a + LANES] = t.astype(BF16)

    rq = proj(_QB, FOX_WIDTH)
    rk = proj(_KB, FOX_WIDTH)
    for hd in range(FOX_HEADS):
        a = (hd // 2) * LANES
        gq = aug[:, a:a + LANES]
        gk = aug[:, FOX_WIDTH + a:FOX_WIDTH + a + LANES]
        qv = rq[:, a:a + LANES] * (QK_SCALE * LOG2E)
        kv = rk[:, a:a + LANES]
        if hd % 2 == 0:
            qaug_ref[0, hd] = jnp.where(low, qv, gq).astype(BF16)
            kaug_ref[0, hd] = jnp.where(low, kv, gk).astype(BF16)
        else:
            qaug_ref[0, hd] = jnp.where(low, gq, qv).astype(BF16)
            kaug_ref[0, hd] = jnp.where(low, gk, kv).astype(BF16)

    vt = lax.dot_general(wvt_ref[...], hb, (((1,), (1,)), ((), ())), preferred_element_type=F32)
    vat_ref[0] = vt[0:SWA_KV_WIDTH].astype(BF16)
    vbt_ref[0] = vt[SWA_KV_WIDTH:].astype(BF16)


def _aug_placement():
    e = np.zeros((LANES, 2 * FOX_HEADS * HEAD_DIM), np.float32)
    for hd in range(FOX_HEADS):
        base = (hd // 2) * LANES + (HEAD_DIM if hd % 2 == 0 else 0)
        kbase = FOX_HEADS * HEAD_DIM + base
        for part in range(3):
            e[part * FOX_HEADS + hd, base + part] = 1.0
            e[_ONE_LANE, base + 3 + part] = 1.0
            e[_ONE_LANE, kbase + part] = 1.0
            e[part * FOX_HEADS + hd, kbase + 3 + part] = -1.0
        for c in range(_MASK_BLOCKS):
            e[_BLK_LANE + c, kbase + _MASK_OFF + c] = 1.0
    return e


def _proj(x, pos_f, scale, shift, gn, invf, bf_rep, w_all, w_vt, e_mat, *, tm):
    b, s, _ = x.shape
    grid = (b, s // tm)
    row = lambda bi, i: (bi, i, 0)
    per_b = lambda bi, i: (bi, 0, 0)
    const2 = lambda bi, i: (0, 0)
    once = pl.Buffered(1)
    in_specs = [
        pl.BlockSpec((1, tm, D_MODEL), row),
        pl.BlockSpec((1, 1, tm), lambda bi, i: (bi, 0, i)),
        pl.BlockSpec((1, 1, D_MODEL), per_b),
        pl.BlockSpec((1, 1, D_MODEL), per_b),
        pl.BlockSpec((1, D_MODEL), const2),
        pl.BlockSpec((HALF, tm), const2),
        pl.BlockSpec((1, LANES), const2),
        pl.BlockSpec((D_MODEL, _PROJ_COLS), const2, pipeline_mode=once),
        pl.BlockSpec((SWA_KV_WIDTH + FOX_WIDTH, D_MODEL), const2, pipeline_mode=once),
        pl.BlockSpec((LANES, 2 * FOX_WIDTH), const2, pipeline_mode=once),
    ]
    head4 = lambda bi, i: (bi, 0, i, 0)
    out_specs = [
        pl.BlockSpec((1, tm, SWA_WIDTH), row),
        pl.BlockSpec((1, tm, 2 * SWA_KV_WIDTH), row),
        pl.BlockSpec((1, SWA_KV_WIDTH, tm), lambda bi, i: (bi, 0, i)),
        pl.BlockSpec((1, tm, SWA_WIDTH), row),
        pl.BlockSpec((1, FOX_HEADS, tm, LANES), head4),
        pl.BlockSpec((1, FOX_HEADS, tm, LANES), head4),
        pl.BlockSpec((1, FOX_WIDTH, tm), lambda bi, i: (bi, 0, i)),
        pl.BlockSpec((1, tm, FOX_WIDTH), row),
        pl.BlockSpec((1, tm, D_MODEL), row),
        pl.BlockSpec((1, tm, D_MODEL), row),
    ]
    sds = jax.ShapeDtypeStruct
    out_shape = [
        sds((b, s, SWA_WIDTH), BF16), sds((b, s, 2 * SWA_KV_WIDTH), BF16),
        sds((b, SWA_KV_WIDTH, s), BF16), sds((b, s, SWA_WIDTH), BF16),
        sds((b, FOX_HEADS, s, LANES), BF16), sds((b, FOX_HEADS, s, LANES), BF16),
        sds((b, FOX_WIDTH, s), BF16), sds((b, s, FOX_WIDTH), BF16),
        sds((b, s, D_MODEL), BF16), sds((b, s, D_MODEL), BF16),
    ]
    return pl.pallas_call(
        functools.partial(_proj_kernel, tm=tm),
        grid=grid, in_specs=in_specs, out_specs=out_specs, out_shape=out_shape,
        scratch_shapes=[pltpu.VMEM((1, LANES), F32)],
        compiler_params=pltpu.CompilerParams(
            dimension_semantics=("arbitrary", "arbitrary"),
            vmem_limit_bytes=_vmem_limit(56 * 1024 * 1024)),
        name="in_proj",
    )(x, pos_f, scale, shift, gn, invf, bf_rep, w_all, w_vt, e_mat)


def _swa_kernel(sinks_ref, q_ref, kc_ref, kp_ref, vtc_ref, vtp_ref, bias_ref, o_ref, kband, vtband, *, tq):
    i = pl.program_id(1)
    nblk = tq // WINDOW
    nhq = SWA_Q_HEADS // SWA_KV_HEADS
    kband[0:WINDOW] = kp_ref[0]
    kband[WINDOW:] = kc_ref[0]
    vtband[:, 0:WINDOW] = vtp_ref[0]
    vtband[:, WINDOW:] = vtc_ref[0]

    cols = nhq * WINDOW
    first_plane = jnp.where(i > 0, 0, 1)
    headid = lax.broadcasted_iota(jnp.int32, (1, cols), 1) // WINDOW
    low = lax.broadcasted_iota(jnp.int32, (WINDOW, LANES), 1) < HEAD_DIM
    ones = jnp.ones((_ONES_ROWS, 2 * WINDOW), BF16)
    nt = (((1,), (1,)), ((), ()))

    def logits(r, g):
        kb = kband[r * WINDOW:(r + 2) * WINDOW, g * LANES:(g + 1) * LANES]
        qs = []
        for c in range(2):
            a = (2 * g + c) * LANES
            qc = q_ref[0, r * WINDOW:(r + 1) * WINDOW, a:a + LANES]
            qs.append(jnp.where(low, qc, jnp.zeros_like(qc)))
            qs.append(jnp.where(low, jnp.zeros_like(qc), qc))
        qst = jnp.concatenate(qs, axis=0)
        s = lax.dot_general(kb, qst, nt, preferred_element_type=F32)
        return s + bias_ref[first_plane if r == 0 else 0]

    def attend(r, g, s):
        sink = jnp.zeros((1, cols), F32)
        for k in range(nhq):
            sink = jnp.where(headid == k, sinks_ref[g * nhq + k] * LOG2E, sink)
        m = jnp.maximum(jnp.max(s, axis=0, keepdims=True), sink)
        p = jnp.exp2(s - m).astype(BF16)
        vt = vtband[g * HEAD_DIM:(g + 1) * HEAD_DIM, r * WINDOW:(r + 2) * WINDOW]
        pv = jnp.dot(jnp.concatenate([vt, ones], axis=0), p, preferred_element_type=F32)
        den = pv[HEAD_DIM:HEAD_DIM + 1] + jnp.exp2(sink - m)
        o_t = pv[0:HEAD_DIM] / den
        for c in range(2):
            a = (2 * g + c) * LANES
            pair = jnp.concatenate([o_t[:, 2 * c * WINDOW:(2 * c + 1) * WINDOW],
                                    o_t[:, (2 * c + 1) * WINDOW:(2 * c + 2) * WINDOW]], axis=0)
            o_ref[0, r * WINDOW:(r + 1) * WINDOW, a:a + LANES] = pair.T.astype(BF16)

    work = [(r, g) for r in range(nblk) for g in range(SWA_KV_HEADS)]
    s_next = logits(*work[0])
    for n, (r, g) in enumerate(work):
        s_cur = s_next
        if n + 1 < len(work):
            s_next = logits(*work[n + 1])
        attend(r, g, s_cur)


def _swa(sinks, qa, ka, vat, *, tq):
    b, s, _ = qa.shape
    kvw = 2 * SWA_KV_WIDTH
    per = tq // WINDOW
    cur = lambda bi, i: (bi, i, 0)
    prev = lambda bi, i: (bi, jnp.maximum(i * per - 1, 0), 0)
    cur_t = lambda bi, i: (bi, 0, i)
    prev_t = lambda bi, i: (bi, 0, jnp.maximum(i * per - 1, 0))
    key = np.arange(2 * WINDOW)[:, None]
    qt = np.tile(np.arange(WINDOW), SWA_Q_HEADS // SWA_KV_HEADS)[None, :]
    band = np.where(key < WINDOW, key > qt, (key - WINDOW) <= qt)
    planes = np.stack([band, band & (key >= WINDOW)])
    bias = jnp.asarray(np.where(planes, 0.0, -np.inf).astype(np.float32))
    return pl.pallas_call(
        functools.partial(_swa_kernel, tq=tq),
        grid=(b, s // tq),
        in_specs=[pl.BlockSpec(memory_space=pltpu.SMEM),
                  pl.BlockSpec((1, tq, SWA_WIDTH), cur),
                  pl.BlockSpec((1, tq, kvw), cur),
                  pl.BlockSpec((1, WINDOW, kvw), prev),
                  pl.BlockSpec((1, SWA_KV_WIDTH, tq), cur_t),
                  pl.BlockSpec((1, SWA_KV_WIDTH, WINDOW), prev_t),
                  pl.BlockSpec(bias.shape, lambda bi, i: (0, 0, 0), pipeline_mode=pl.Buffered(1))],
        out_specs=pl.BlockSpec((1, tq, SWA_WIDTH), cur),
        out_shape=jax.ShapeDtypeStruct((b, s, SWA_WIDTH), BF16),
        scratch_shapes=[pltpu.VMEM((tq + WINDOW, kvw), BF16),
                        pltpu.VMEM((SWA_KV_WIDTH, tq + WINDOW), BF16)],
        compiler_params=pltpu.CompilerParams(dimension_semantics=("arbitrary", "arbitrary")),
        name="swa_attn",
    )(sinks, qa, ka, ka, vat, vat, bias)


_ACC_ROWS = HEAD_DIM + _ONES_ROWS
_PIPE = 2
_FOX_UNROLL = 4
_ACC_SLOTS = 3


def _fox_schedule(n_rows):
    seq = []
    for i in range(n_rows):
        seq.append((i, i, 1, i % _ACC_SLOTS, int(i == 0)))
        for j in range(i):
            seq.append((i, j, 0, i % _ACC_SLOTS, int(j == i - 1)))
    dummy = (0, 0, 1, _ACC_SLOTS, 0)
    n_steps = len(seq) + _PIPE
    n_steps += -n_steps % _FOX_UNROLL
    tbl = [dummy] * _PIPE + seq + [dummy] * (n_steps - len(seq))
    return np.asarray(tbl, np.int32).T.copy(), n_steps


def _fox_kernel(tbl_ref, q_ref, k_ref, vt_ref, qmask_ref, tri_ref, o_ref,
                s0, s1, mx0, mx1, p0, p1, a0, a1, m_ref, acc_ref, *, t, n_steps):
    nt = (((1,), (1,)), ((), ()))
    g = t // 8
    blk = t // _MASK_BLOCKS
    for ref in (s1, mx1, p0, a0, m_ref, acc_ref):
        ref[...] = jnp.zeros_like(ref)

    def step(k, s_w, mx_w, s_r, mx_r, p_w, a_w, p_r, a_r):
        qrow = pl.multiple_of(tbl_ref[0, k + 2] * t, t)
        krow = pl.multiple_of(tbl_ref[1, k + 2] * t, t)
        diag = tbl_ref[2, k + 2]
        for hh in range(2):
            q = q_ref[0, hh, pl.ds(qrow, t), :]
            q = jnp.where(diag == 1, q + qmask_ref[hh], q)
            s = lax.dot_general(k_ref[0, hh, pl.ds(krow, t), :], q, nt,
                                preferred_element_type=F32)
            mx = None
            for r in range(_MASK_BLOCKS):
                rows = slice(r * blk, (r + 1) * blk)
                pieces = [s[rows, :r * blk]] if r else []
                pieces.append(s[rows, r * blk:(r + 1) * blk] + tri_ref[diag])
                if r + 1 < _MASK_BLOCKS:
                    pieces.append(s[rows, (r + 1) * blk:])
                sr = jnp.concatenate(pieces, axis=1)
                s_w[hh, rows, :] = sr
                mr = jnp.max(sr.reshape(blk // 8, 8, t), axis=0)
                mx = mr if mx is None else jnp.maximum(mx, mr)
            mx_w[hh] = mx

        first = tbl_ref[2, k + 1] == 1
        for hh in range(2):
            m_old = jnp.where(first, -jnp.inf, m_ref[hh])
            m_new = jnp.maximum(m_old, jnp.max(mx_r[hh], axis=0, keepdims=True))
            m_ref[hh] = m_new
            a_w[hh] = jnp.exp2(m_old - m_new)
            p = jnp.exp2(s_r[hh].reshape(g, 8, t) - m_new[None])
            p_w[hh] = p.reshape(t, t).astype(BF16)

        vcol = pl.multiple_of(tbl_ref[1, k] * t, t)
        asl = tbl_ref[3, k]
        for hh in range(2):
            vt = vt_ref[0, hh * HEAD_DIM:(hh + 1) * HEAD_DIM, pl.ds(vcol, t)]
            lhs = jnp.concatenate([vt, jnp.ones((_ONES_ROWS, t), BF16)], axis=0)
            part = jnp.dot(lhs, p_r[hh], preferred_element_type=F32)
            old = acc_ref[asl, hh].reshape(_ACC_ROWS // 8, 8, t)
            acc_ref[asl, hh] = (a_r[hh][None] * old).reshape(_ACC_ROWS, t) + part

    def finish_row(k):
        @pl.when(tbl_ref[4, k] == 1)
        def _():
            sl = tbl_ref[3, k]
            orow = pl.multiple_of(tbl_ref[0, k] * t, t)
            halves = []
            for hh in range(2):
                a = acc_ref[sl, hh]
                halves.append(a[0:HEAD_DIM] / a[HEAD_DIM:HEAD_DIM + 1])
            o_t = jnp.concatenate(halves, axis=0)
            o_ref[0, pl.ds(orow, t), :] = o_t.T.astype(BF16)

    def body(kk, carry):
        k = _FOX_UNROLL * kk
        for u in range(0, _FOX_UNROLL, 2):
            step(k + u, s0, mx0, s1, mx1, p1, a1, p0, a0)
            step(k + u + 1, s1, mx1, s0, mx0, p0, a0, p1, a1)
        for u in range(_FOX_UNROLL):
            finish_row(k + u)
        return carry

    assert n_steps % _FOX_UNROLL == 0 and _FOX_UNROLL % 2 == 0
    lax.fori_loop(0, n_steps // _FOX_UNROLL, body, 0)


def _fox(q_aug, k_aug, vbt, *, t):
    assert t == _PROJ_TM, "k's block one-hot is laid out per input-projection tile"
    b, _, s, _ = q_aug.shape
    npair = FOX_HEADS // 2
    tbl, n_steps = _fox_schedule(s // t)
    blk = t // _MASK_BLOCKS
    tri = np.triu(np.ones((blk, blk), bool))
    tri_bias = jnp.asarray(np.stack([np.zeros((blk, blk), np.float32),
                                     np.where(tri, 0.0, -np.inf).astype(np.float32)]))
    qmask = np.zeros((2, t, LANES), np.float32)
    for hh in range(2):
        base = HEAD_DIM if hh == 0 else 0
        for c in range(1, _MASK_BLOCKS):
            qmask[hh, :c * blk, base + _MASK_OFF + c] = -_MASK_BIG
    qmask = jnp.asarray(qmask, BF16)
    pair = lambda bi, p: (bi, p, 0, 0)
    stat = pltpu.VMEM((2, 8, t), F32)
    return pl.pallas_call(
        functools.partial(_fox_kernel, t=t, n_steps=n_steps),
        grid=(b, npair),
        in_specs=[pl.BlockSpec(memory_space=pltpu.SMEM),
                  pl.BlockSpec((1, 2, s, LANES), pair),
                  pl.BlockSpec((1, 2, s, LANES), pair),
                  pl.BlockSpec((1, 2 * HEAD_DIM, s), lambda bi, p: (bi, p, 0)),
                  pl.BlockSpec((2, t, LANES), lambda bi, p: (0, 0, 0), pipeline_mode=pl.Buffered(1)),
                  pl.BlockSpec((2, blk, blk), lambda bi, p: (0, 0, 0), pipeline_mode=pl.Buffered(1))],
        out_specs=pl.BlockSpec((1, s, LANES), lambda bi, p: (bi, 0, p)),
        out_shape=jax.ShapeDtypeStruct((b, s, FOX_WIDTH), BF16),
        scratch_shapes=[pltpu.VMEM((2, t, t), F32), pltpu.VMEM((2, t, t), F32),
                        stat, stat,
                        pltpu.VMEM((2, t, t), BF16), pltpu.VMEM((2, t, t), BF16),
                        stat, stat,
                        stat,
                        pltpu.VMEM((_ACC_SLOTS + 1, 2, _ACC_ROWS, t), F32)],
        compiler_params=pltpu.CompilerParams(
            dimension_semantics=("arbitrary", "arbitrary"),
            vmem_limit_bytes=_vmem_limit(56 * 1024 * 1024)),
        name="fox_attn",
    )(jnp.asarray(tbl), q_aug, k_aug, vbt, qmask, tri_bias)


def _out_kernel(x_ref, aa_ref, sza_ref, ab_ref, szb_ref, sga_ref, sgb_ref, gate_ref,
                woa_ref, wob_ref, wout_ref, gf_ref, o_ref):
    a = (aa_ref[0].astype(F32) * sza_ref[0].astype(F32)).astype(BF16)
    bb = (ab_ref[0].astype(F32) * szb_ref[0].astype(F32)).astype(BF16)
    ya = jnp.dot(a, woa_ref[...], preferred_element_type=F32)
    yb = jnp.dot(bb, wob_ref[...], preferred_element_type=F32)
    merged = sga_ref[0].astype(F32) * ya + sgb_ref[0].astype(F32) * yb
    d = jnp.dot(merged.astype(BF16), wout_ref[...], preferred_element_type=F32)
    xo = x_ref[0] + gate_ref[0] * d
    ms = jnp.mean(xo * xo, axis=-1, keepdims=True)
    o_ref[0] = xo * lax.rsqrt(ms + NORM_EPS) * gf_ref[...]


def _out(x, att_a, sza, att_b, szb, sga, sgb, gate, woa, wob, wout, gf, *, tm):
    b, s, _ = x.shape
    row = lambda bi, i: (bi, i, 0)
    per_b = lambda bi, i: (bi, 0, 0)
    const2 = lambda bi, i: (0, 0)
    once = pl.Buffered(1)
    return pl.pallas_call(
        _out_kernel,
        grid=(b, s // tm),
        in_specs=[pl.BlockSpec((1, tm, D_MODEL), row),
                  pl.BlockSpec((1, tm, SWA_WIDTH), row), pl.BlockSpec((1, tm, SWA_WIDTH), row),
                  pl.BlockSpec((1, tm, FOX_WIDTH), row), pl.BlockSpec((1, tm, FOX_WIDTH), row),
                  pl.BlockSpec((1, tm, D_MODEL), row), pl.BlockSpec((1, tm, D_MODEL), row),
                  pl.BlockSpec((1, 1, D_MODEL), per_b),
                  pl.BlockSpec((SWA_WIDTH, D_MODEL), const2, pipeline_mode=once),
                  pl.BlockSpec((FOX_WIDTH, D_MODEL), const2, pipeline_mode=once),
                  pl.BlockSpec((D_MODEL, D_MODEL), const2, pipeline_mode=once),
                  pl.BlockSpec((1, D_MODEL), const2)],
        out_specs=pl.BlockSpec((1, tm, D_MODEL), row),
        out_shape=jax.ShapeDtypeStruct((b, s, D_MODEL), F32),
        compiler_params=pltpu.CompilerParams(
            dimension_semantics=("arbitrary", "arbitrary"),
            vmem_limit_bytes=_vmem_limit(48 * 1024 * 1024)),
        name="out_proj",
    )(x, att_a, sza, att_b, szb, sga, sgb, gate, woa, wob, wout, gf)


def _rot_cols(w, nheads):
    w3 = w.reshape(w.shape[0], nheads, HEAD_DIM)
    return jnp.concatenate([-w3[..., HALF:], w3[..., :HALF]], axis=-1).reshape(w.shape)


def _dup_heads(w, nheads):
    w3 = w.reshape(w.shape[0], nheads, 1, HEAD_DIM)
    return jnp.broadcast_to(w3, (w.shape[0], nheads, 2, HEAD_DIM)).reshape(w.shape[0], 2 * nheads * HEAD_DIM)


def _layout_w_in(w):
    o = np.cumsum([0, SWA_WIDTH, SWA_KV_WIDTH, SWA_KV_WIDTH, SWA_WIDTH, FOX_WIDTH, FOX_WIDTH, FOX_WIDTH,
                   FOX_HEADS, FOX_WIDTH, D_MODEL, D_MODEL]).tolist()
    qa, ka, va, za, qb, kb, vb, fb, zb, ga, gb = [w[:, o[k]:o[k + 1]] for k in range(11)]
    cols = [qa, _rot_cols(qa, SWA_Q_HEADS),
            _dup_heads(ka, SWA_KV_HEADS), _dup_heads(_rot_cols(ka, SWA_KV_HEADS), SWA_KV_HEADS),
            za, qb, kb, jnp.tile(fb, (1, LANES // FOX_HEADS)), zb, ga, gb]
    w_all = jnp.concatenate(cols, axis=1).astype(BF16)
    assert w_all.shape[1] == _PROJ_COLS
    return w_all, jnp.concatenate([va, vb], axis=1).T.astype(BF16)


def kernel(x, c, positions, w_ada, b_ada, g_norm, w_in, b_f, sinks, w_o_swa, w_o_fox, w_out, g_final):
    b, s, _ = x.shape
    depth = w_in.shape[0]
    assert depth == 1, "the output stage fuses the final RMSNorm into the single layer"
    inv_freq = ROPE_THETA ** (-jnp.arange(0, HEAD_DIM, 2, dtype=F32) / HEAD_DIM)
    invf = jnp.broadcast_to(inv_freq[:, None], (HALF, _PROJ_TM))
    pos_f = positions.astype(F32)[:, None, :]
    e_mat = jnp.asarray(_aug_placement(), BF16)
    c_pad = jnp.zeros((8, D_MODEL), F32).at[:b].set(c)
    for l in range(depth):
        ada = _ada(c_pad, w_ada[l], b_ada[l][None, :])[:b]
        shift, scale, gate = [ada[:, None, k * D_MODEL:(k + 1) * D_MODEL] for k in range(3)]
        w_all, w_vt = _layout_w_in(w_in[l])
        bf_rep = jnp.tile(b_f[l].astype(F32), LANES // FOX_HEADS)[None, :]
        qa, ka, vat, sza, q_aug, k_aug, vbt, szb, sga, sgb = _proj(
            x, pos_f, scale, shift, g_norm[l][None, :], invf, bf_rep, w_all, w_vt, e_mat, tm=_PROJ_TM)
        att_a = _swa(sinks[l].astype(F32), qa, ka, vat, tq=_SWA_TQ)
        att_b = _fox(q_aug, k_aug, vbt, t=_FOX_T)
        x = _out(x, att_a, sza, att_b, szb, sga, sgb, gate,
                 w_o_swa[l].astype(BF16), w_o_fox[l].astype(BF16), w_out[l].astype(BF16),
                 g_final[None, :], tm=_OUT_TM)
    return x
```

```python
import functools
import math

import numpy as np
import jax
import jax.numpy as jnp
from jax import lax
from jax.experimental import pallas as pl
from jax.experimental.pallas import tpu as pltpu

D_MODEL = 1024
HEAD_DIM = 64
HALF = HEAD_DIM // 2
SWA_Q_HEADS = 8
SWA_KV_HEADS = 2
SWA_WIDTH = SWA_Q_HEADS * HEAD_DIM
SWA_KV_WIDTH = SWA_KV_HEADS * HEAD_DIM
FOX_HEADS = 8
FOX_WIDTH = FOX_HEADS * HEAD_DIM
WINDOW = 128
ROPE_THETA = 10000.0
NORM_EPS = 1e-6
QK_SCALE = HEAD_DIM ** -0.5
LOG2E = math.log2(math.e)

LANES = 128
V7X_VMEM_BYTES = 64 * 1024 * 1024

_PROJ_TM = 512
_SWA_TQ = 512
_FOX_T = 512
_OUT_TM = 512

F32 = jnp.float32
BF16 = jnp.bfloat16

_QA, _KA, _ZA = 0, 1024, 1536
_QB, _KB, _FB, _ZB, _GA, _GB = 2048, 2560, 3072, 3200, 3712, 4736
_PROJ_COLS = 5760

_ONE_LANE = 24

_ONES_ROWS = 16


def _vmem_limit(nbytes):
    return int(min(nbytes, V7X_VMEM_BYTES - 8 * 1024 * 1024))


def _split3(v):
    hi = v.astype(BF16)
    r1 = v - hi.astype(F32)
    mid = r1.astype(BF16)
    lo = (r1 - mid.astype(F32)).astype(BF16)
    return hi, mid, lo


def _ada_kernel(c_ref, w_ref, b_ref, o_ref):
    o_ref[...] = jnp.dot(c_ref[...], w_ref[...], preferred_element_type=F32,
                         precision=lax.Precision.HIGHEST) + b_ref[...]


def _ada(c_pad, w_ada, b_ada):
    rows = c_pad.shape[0]
    n = w_ada.shape[1]
    nblk = n // D_MODEL
    return pl.pallas_call(
        _ada_kernel,
        grid=(nblk,),
        in_specs=[pl.BlockSpec((rows, D_MODEL), lambda j: (0, 0)),
                  pl.BlockSpec((D_MODEL, D_MODEL), lambda j: (0, j)),
                  pl.BlockSpec((1, D_MODEL), lambda j: (0, j))],
        out_specs=pl.BlockSpec((rows, D_MODEL), lambda j: (0, j)),
        out_shape=jax.ShapeDtypeStruct((rows, n), F32),
        name="ada_mod",
    )(c_pad, w_ada, b_ada)


def _sigmoid(z):
    return 0.5 * jnp.tanh(0.5 * z) + 0.5


def _proj_kernel(x_ref, pos_ref, scale_ref, shift_ref, gn_ref, invf_ref, bf_ref, w_ref, wvt_ref, e_ref,
                 qa_ref, ka_ref, vat_ref, sza_ref, qaug_ref, kaug_ref, vbt_ref, szb_ref,
                 sga_ref, sgb_ref, carry_ref, *, tm):
    i = pl.program_id(1)

    @pl.when(i == 0)
    def _():
        carry_ref[...] = jnp.zeros_like(carry_ref)

    x = x_ref[0]
    ms = jnp.mean(x * x, axis=-1, keepdims=True)
    y = x * lax.rsqrt(ms + NORM_EPS) * gn_ref[...]
    h = y * (1.0 + scale_ref[0]) + shift_ref[0]
    hb = h.astype(BF16)

    def proj(off, n):
        return jnp.dot(hb, w_ref[:, off:off + n], preferred_element_type=F32)

    lane = lax.broadcasted_iota(jnp.int32, (tm, LANES), 1)
    low = lane < HEAD_DIM


    zf = proj(_FB, LANES) + bf_ref[...]
    logf = jnp.minimum(zf, 0.0) - jnp.log1p(jnp.exp(-jnp.abs(zf)))
    grp = lane // FOX_HEADS
    hi, mid, lo = _split3(logf)
    zero = jnp.zeros((tm, LANES), F32)
    parts = jnp.where(grp == 0, hi.astype(F32),
                      jnp.where(grp == 1, mid.astype(F32),
                                jnp.where(grp == 2, lo.astype(F32), zero))).astype(BF16)
    tri = (lax.broadcasted_iota(jnp.int32, (tm, tm), 0)
           >= lax.broadcasted_iota(jnp.int32, (tm, tm), 1)).astype(BF16)
    rsum = jnp.dot(tri, parts, preferred_element_type=F32)
    c0 = rsum + pltpu.roll(rsum, LANES - FOX_HEADS, 1) + pltpu.roll(rsum, LANES - 2 * FOX_HEADS, 1)
    c0 = jnp.where(grp == 0, c0, zero)
    cl = c0 + pltpu.roll(c0, FOX_HEADS, 1) + pltpu.roll(c0, 2 * FOX_HEADS, 1)
    cum = cl + carry_ref[...]
    carry_ref[...] = cum[tm - 1:tm, :]

    hi, mid, lo = _split3(cum * LOG2E)
    one = jnp.ones((tm, LANES), F32)
    carrier = jnp.where(grp == 0, hi.astype(F32),
                        jnp.where(grp == 1, mid.astype(F32),
                                  jnp.where(grp == 2, lo.astype(F32),
                                            jnp.where(lane == _ONE_LANE, one, zero)))).astype(BF16)
    aug = jnp.dot(carrier, e_ref[...], preferred_element_type=F32)

    sga_ref[0] = _sigmoid(proj(_GA, D_MODEL)).astype(BF16)
    sgb_ref[0] = _sigmoid(proj(_GB, D_MODEL)).astype(BF16)
    z = proj(_ZA, SWA_WIDTH)
    sza_ref[0] = (z * _sigmoid(z)).astype(BF16)
    z = proj(_ZB, FOX_WIDTH)
    szb_ref[0] = (z * _sigmoid(z)).astype(BF16)

    ang = invf_ref[...] * pos_ref[0]
    cos = jnp.tile(jnp.cos(ang), (LANES // HALF, 1)).T
    sin = jnp.tile(jnp.sin(ang), (LANES // HALF, 1)).T
    r = proj(_QA, 2 * SWA_WIDTH)
    for c in range(SWA_WIDTH // LANES):
        a = c * LANES
        t = r[:, a:a + LANES] * cos + r[:, SWA_WIDTH + a:SWA_WIDTH + a + LANES] * sin
        qa_ref[0, :, a:a + LANES] = (t * (QK_SCALE * LOG2E)).astype(BF16)
    r = proj(_KA, 4 * SWA_KV_WIDTH)
    for c in range(2 * SWA_KV_WIDTH // LANES):
        a = c * LANES
        t = r[:, a:a + LANES] * cos + r[:, 2 * SWA_KV_WIDTH + a:2 * SWA_KV_WIDTH + a + LANES] * sin
        ka_ref[0, :, a:a + LANES] = t.astype(BF16)

    rq = proj(_QB, FOX_WIDTH)
    rk = proj(_KB, FOX_WIDTH)
    for hd in range(FOX_HEADS):
        a = (hd // 2) * LANES
        gq = aug[:, a:a + LANES]
        gk = aug[:, FOX_WIDTH + a:FOX_WIDTH + a + LANES]
        qv = rq[:, a:a + LANES] * (QK_SCALE * LOG2E)
        kv = rk[:, a:a + LANES]
        if hd % 2 == 0:
            qaug_ref[0, hd] = jnp.where(low, qv, gq).astype(BF16)
            kaug_ref[0, hd] = jnp.where(low, kv, gk).astype(BF16)
        else:
            qaug_ref[0, hd] = jnp.where(low, gq, qv).astype(BF16)
            kaug_ref[0, hd] = jnp.where(low, gk, kv).astype(BF16)

    vt = lax.dot_general(wvt_ref[...], hb, (((1,), (1,)), ((), ())), preferred_element_type=F32)
    vat_ref[0] = vt[0:SWA_KV_WIDTH].astype(BF16)
    vbt_ref[0] = vt[SWA_KV_WIDTH:].astype(BF16)


def _aug_placement():
    e = np.zeros((LANES, 2 * FOX_HEADS * HEAD_DIM), np.float32)
    for hd in range(FOX_HEADS):
        base = (hd // 2) * LANES + (HEAD_DIM if hd % 2 == 0 else 0)
        kbase = FOX_HEADS * HEAD_DIM + base
        for part in range(3):
            e[part * FOX_HEADS + hd, base + part] = 1.0
            e[_ONE_LANE, base + 3 + part] = 1.0
            e[_ONE_LANE, kbase + part] = 1.0
            e[part * FOX_HEADS + hd, kbase + 3 + part] = -1.0
    return e


def _proj(x, pos_f, scale, shift, gn, invf, bf_rep, w_all, w_vt, e_mat, *, tm):
    b, s, _ = x.shape
    grid = (b, s // tm)
    row = lambda bi, i: (bi, i, 0)
    per_b = lambda bi, i: (bi, 0, 0)
    const2 = lambda bi, i: (0, 0)
    once = pl.Buffered(1)
    in_specs = [
        pl.BlockSpec((1, tm, D_MODEL), row),
        pl.BlockSpec((1, 1, tm), lambda bi, i: (bi, 0, i)),
        pl.BlockSpec((1, 1, D_MODEL), per_b),
        pl.BlockSpec((1, 1, D_MODEL), per_b),
        pl.BlockSpec((1, D_MODEL), const2),
        pl.BlockSpec((HALF, tm), const2),
        pl.BlockSpec((1, LANES), const2),
        pl.BlockSpec((D_MODEL, _PROJ_COLS), const2, pipeline_mode=once),
        pl.BlockSpec((SWA_KV_WIDTH + FOX_WIDTH, D_MODEL), const2, pipeline_mode=once),
        pl.BlockSpec((LANES, 2 * FOX_WIDTH), const2, pipeline_mode=once),
    ]
    head4 = lambda bi, i: (bi, 0, i, 0)
    out_specs = [
        pl.BlockSpec((1, tm, SWA_WIDTH), row),
        pl.BlockSpec((1, tm, 2 * SWA_KV_WIDTH), row),
        pl.BlockSpec((1, SWA_KV_WIDTH, tm), lambda bi, i: (bi, 0, i)),
        pl.BlockSpec((1, tm, SWA_WIDTH), row),
        pl.BlockSpec((1, FOX_HEADS, tm, LANES), head4),
        pl.BlockSpec((1, FOX_HEADS, tm, LANES), head4),
        pl.BlockSpec((1, FOX_WIDTH, tm), lambda bi, i: (bi, 0, i)),
        pl.BlockSpec((1, tm, FOX_WIDTH), row),
        pl.BlockSpec((1, tm, D_MODEL), row),
        pl.BlockSpec((1, tm, D_MODEL), row),
    ]
    sds = jax.ShapeDtypeStruct
    out_shape = [
        sds((b, s, SWA_WIDTH), BF16), sds((b, s, 2 * SWA_KV_WIDTH), BF16),
        sds((b, SWA_KV_WIDTH, s), BF16), sds((b, s, SWA_WIDTH), BF16),
        sds((b, FOX_HEADS, s, LANES), BF16), sds((b, FOX_HEADS, s, LANES), BF16),
        sds((b, FOX_WIDTH, s), BF16), sds((b, s, FOX_WIDTH), BF16),
        sds((b, s, D_MODEL), BF16), sds((b, s, D_MODEL), BF16),
    ]
    return pl.pallas_call(
        functools.partial(_proj_kernel, tm=tm),
        grid=grid, in_specs=in_specs, out_specs=out_specs, out_shape=out_shape,
        scratch_shapes=[pltpu.VMEM((1, LANES), F32)],
        compiler_params=pltpu.CompilerParams(
            dimension_semantics=("arbitrary", "arbitrary"),
            vmem_limit_bytes=_vmem_limit(56 * 1024 * 1024)),
        name="in_proj",
    )(x, pos_f, scale, shift, gn, invf, bf_rep, w_all, w_vt, e_mat)


def _swa_kernel(sinks_ref, q_ref, kc_ref, kp_ref, vtc_ref, vtp_ref, bias_ref, o_ref, kband, vtband, *, tq):
    i = pl.program_id(1)
    nblk = tq // WINDOW
    nhq = SWA_Q_HEADS // SWA_KV_HEADS
    kband[0:WINDOW] = kp_ref[0]
    kband[WINDOW:] = kc_ref[0]
    vtband[:, 0:WINDOW] = vtp_ref[0]
    vtband[:, WINDOW:] = vtc_ref[0]

    cols = nhq * WINDOW
    first_plane = jnp.where(i > 0, 0, 1)
    headid = lax.broadcasted_iota(jnp.int32, (1, cols), 1) // WINDOW
    low = lax.broadcasted_iota(jnp.int32, (WINDOW, LANES), 1) < HEAD_DIM
    ones = jnp.ones((_ONES_ROWS, 2 * WINDOW), BF16)
    nt = (((1,), (1,)), ((), ()))

    def logits(r, g):
        kb = kband[r * WINDOW:(r + 2) * WINDOW, g * LANES:(g + 1) * LANES]
        qs = []
        for c in range(2):
            a = (2 * g + c) * LANES
            qc = q_ref[0, r * WINDOW:(r + 1) * WINDOW, a:a + LANES]
            qs.append(jnp.where(low, qc, jnp.zeros_like(qc)))
            qs.append(jnp.where(low, jnp.zeros_like(qc), qc))
        qst = jnp.concatenate(qs, axis=0)
        s = lax.dot_general(kb, qst, nt, preferred_element_type=F32)
        return s + bias_ref[first_plane if r == 0 else 0]

    def attend(r, g, s):
        sink = jnp.zeros((1, cols), F32)
        for k in range(nhq):
            sink = jnp.where(headid == k, sinks_ref[g * nhq + k] * LOG2E, sink)
        m = jnp.maximum(jnp.max(s, axis=0, keepdims=True), sink)
        p = jnp.exp2(s - m).astype(BF16)
        vt = vtband[g * HEAD_DIM:(g + 1) * HEAD_DIM, r * WINDOW:(r + 2) * WINDOW]
        pv = jnp.dot(jnp.concatenate([vt, ones], axis=0), p, preferred_element_type=F32)
        den = pv[HEAD_DIM:HEAD_DIM + 1] + jnp.exp2(sink - m)
        o_t = pv[0:HEAD_DIM] / den
        for c in range(2):
            a = (2 * g + c) * LANES
            pair = jnp.concatenate([o_t[:, 2 * c * WINDOW:(2 * c + 1) * WINDOW],
                                    o_t[:, (2 * c + 1) * WINDOW:(2 * c + 2) * WINDOW]], axis=0)
            o_ref[0, r * WINDOW:(r + 1) * WINDOW, a:a + LANES] = pair.T.astype(BF16)

    work = [(r, g) for r in range(nblk) for g in range(SWA_KV_HEADS)]
    s_next = logits(*work[0])
    for n, (r, g) in enumerate(work):
        s_cur = s_next
        if n + 1 < len(work):
            s_next = logits(*work[n + 1])
        attend(r, g, s_cur)


def _swa(sinks, qa, ka, vat, *, tq):
    b, s, _ = qa.shape
    kvw = 2 * SWA_KV_WIDTH
    per = tq // WINDOW
    cur = lambda bi, i: (bi, i, 0)
    prev = lambda bi, i: (bi, jnp.maximum(i * per - 1, 0), 0)
    cur_t = lambda bi, i: (bi, 0, i)
    prev_t = lambda bi, i: (bi, 0, jnp.maximum(i * per - 1, 0))
    key = np.arange(2 * WINDOW)[:, None]
    qt = np.tile(np.arange(WINDOW), SWA_Q_HEADS // SWA_KV_HEADS)[None, :]
    band = np.where(key < WINDOW, key > qt, (key - WINDOW) <= qt)
    planes = np.stack([band, band & (key >= WINDOW)])
    bias = jnp.asarray(np.where(planes, 0.0, -np.inf).astype(np.float32))
    return pl.pallas_call(
        functools.partial(_swa_kernel, tq=tq),
        grid=(b, s // tq),
        in_specs=[pl.BlockSpec(memory_space=pltpu.SMEM),
                  pl.BlockSpec((1, tq, SWA_WIDTH), cur),
                  pl.BlockSpec((1, tq, kvw), cur),
                  pl.BlockSpec((1, WINDOW, kvw), prev),
                  pl.BlockSpec((1, SWA_KV_WIDTH, tq), cur_t),
                  pl.BlockSpec((1, SWA_KV_WIDTH, WINDOW), prev_t),
                  pl.BlockSpec(bias.shape, lambda bi, i: (0, 0, 0), pipeline_mode=pl.Buffered(1))],
        out_specs=pl.BlockSpec((1, tq, SWA_WIDTH), cur),
        out_shape=jax.ShapeDtypeStruct((b, s, SWA_WIDTH), BF16),
        scratch_shapes=[pltpu.VMEM((tq + WINDOW, kvw), BF16),
                        pltpu.VMEM((SWA_KV_WIDTH, tq + WINDOW), BF16)],
        compiler_params=pltpu.CompilerParams(dimension_semantics=("arbitrary", "arbitrary")),
        name="swa_attn",
    )(sinks, qa, ka, ka, vat, vat, bias)


_ACC_ROWS = HEAD_DIM + _ONES_ROWS
_PIPE = 2
_FOX_UNROLL = 4


def _fox_schedule(n_rows):
    phases = [[(i, i, i) for i in range(n_rows)],
              [(i, j, i) for i in range(n_rows) for j in range(i)]]
    dummy = (0, 0, n_rows)
    tbl, spans = [], []
    for seq in phases:
        n_steps = len(seq) + _PIPE
        n_steps += -n_steps % _FOX_UNROLL
        spans.append((len(tbl), n_steps))
        tbl += [dummy] * _PIPE + seq + [dummy] * (n_steps - len(seq))
    return np.asarray(tbl, np.int32).T.copy(), spans


def _fox_kernel(tbl_ref, q_ref, k_ref, vt_ref, bias_ref, o_ref,
                s0, s1, mx0, mx1, p0, p1, a0, a1, m_ref, acc_ref, *, t, n_rows, spans):
    nt = (((1,), (1,)), ((), ()))
    g = t // 8
    for ref in (s0, s1, mx0, mx1, p0, p1, a0, a1):
        ref[...] = jnp.zeros_like(ref)
    m_ref[n_rows] = jnp.zeros_like(m_ref[n_rows])
    acc_ref[n_rows] = jnp.zeros_like(acc_ref[n_rows])

    def step(k, diagonal, s_w, mx_w, s_r, mx_r, p_w, a_w, p_r, a_r):
        qrow = pl.multiple_of(tbl_ref[0, k + 2] * t, t)
        krow = pl.multiple_of(tbl_ref[1, k + 2] * t, t)
        for hh in range(2):
            s = lax.dot_general(k_ref[0, hh, pl.ds(krow, t), :], q_ref[0, hh, pl.ds(qrow, t), :], nt,
                                preferred_element_type=F32)
            if diagonal:
                s = s + bias_ref[...]
            s_w[hh] = s
            mx_w[hh] = jnp.max(s.reshape(g, 8, t), axis=0)

        srow = tbl_ref[2, k + 1]
        for hh in range(2):
            m_tile = jnp.max(mx_r[hh], axis=0, keepdims=True)
            if diagonal:
                m_new = jnp.broadcast_to(m_tile, (8, t))
            else:
                m_old = m_ref[srow, hh]
                m_new = jnp.maximum(m_old, m_tile)
                a_w[hh] = jnp.exp2(m_old - m_new)
            m_ref[srow, hh] = m_new
            p = jnp.exp2(s_r[hh].reshape(g, 8, t) - m_new[None])
            p_w[hh] = p.reshape(t, t).astype(BF16)

        vcol = pl.multiple_of(tbl_ref[1, k] * t, t)
        arow = tbl_ref[2, k]
        for hh in range(2):
            vt = vt_ref[0, hh * HEAD_DIM:(hh + 1) * HEAD_DIM, pl.ds(vcol, t)]
            lhs = jnp.concatenate([vt, jnp.ones((_ONES_ROWS, t), BF16)], axis=0)
            part = jnp.dot(lhs, p_r[hh], preferred_element_type=F32)
            if diagonal:
                acc_ref[arow, hh] = part
            else:
                old = acc_ref[arow, hh].reshape(_ACC_ROWS // 8, 8, t)
                acc_ref[arow, hh] = (a_r[hh][None] * old).reshape(_ACC_ROWS, t) + part

    assert _FOX_UNROLL % 2 == 0
    for diagonal, (first, n_steps) in zip((True, False), spans):
        def body(kk, carry, diagonal=diagonal, first=first):
            k = first + _FOX_UNROLL * kk
            for u in range(0, _FOX_UNROLL, 2):
                step(k + u, diagonal, s0, mx0, s1, mx1, p1, a1, p0, a0)
                step(k + u + 1, diagonal, s1, mx1, s0, mx0, p0, a0, p1, a1)
            return carry

        lax.fori_loop(0, n_steps // _FOX_UNROLL, body, 0)

    def finish_row(i, carry):
        halves = []
        for hh in range(2):
            a = acc_ref[i, hh]
            halves.append(a[0:HEAD_DIM] / a[HEAD_DIM:HEAD_DIM + 1])
        o_t = jnp.concatenate(halves, axis=0)
        o_ref[0, pl.ds(pl.multiple_of(i * t, t), t), :] = o_t.T.astype(BF16)
        return carry

    lax.fori_loop(0, n_rows, finish_row, 0)


def _fox(q_aug, k_aug, vbt, *, t):
    b, _, s, _ = q_aug.shape
    npair = FOX_HEADS // 2
    n_rows = s // t
    tbl, spans = _fox_schedule(n_rows)
    tri = np.triu(np.ones((t, t), bool))
    bias = jnp.asarray(np.where(tri, 0.0, -np.inf).astype(np.float32))
    pair = lambda bi, p: (bi, p, 0, 0)
    stat = pltpu.VMEM((2, 8, t), F32)
    return pl.pallas_call(
        functools.partial(_fox_kernel, t=t, n_rows=n_rows, spans=tuple(spans)),
        grid=(b, npair),
        in_specs=[pl.BlockSpec(memory_space=pltpu.SMEM),
                  pl.BlockSpec((1, 2, s, LANES), pair),
                  pl.BlockSpec((1, 2, s, LANES), pair),
                  pl.BlockSpec((1, 2 * HEAD_DIM, s), lambda bi, p: (bi, p, 0)),
                  pl.BlockSpec((t, t), lambda bi, p: (0, 0), pipeline_mode=pl.Buffered(1))],
        out_specs=pl.BlockSpec((1, s, LANES), lambda bi, p: (bi, 0, p)),
        out_shape=jax.ShapeDtypeStruct((b, s, FOX_WIDTH), BF16),
        scratch_shapes=[pltpu.VMEM((2, t, t), F32), pltpu.VMEM((2, t, t), F32),
                        stat, stat,
                        pltpu.VMEM((2, t, t), BF16), pltpu.VMEM((2, t, t), BF16),
                        stat, stat,
                        pltpu.VMEM((n_rows + 1, 2, 8, t), F32),
                        pltpu.VMEM((n_rows + 1, 2, _ACC_ROWS, t), F32)],
        compiler_params=pltpu.CompilerParams(
            dimension_semantics=("arbitrary", "arbitrary"),
            vmem_limit_bytes=_vmem_limit(56 * 1024 * 1024)),
        name="fox_attn",
    )(jnp.asarray(tbl), q_aug, k_aug, vbt, bias)


def _out_kernel(x_ref, aa_ref, sza_ref, ab_ref, szb_ref, sga_ref, sgb_ref, gate_ref,
                woa_ref, wob_ref, wout_ref, gf_ref, o_ref):
    a = (aa_ref[0].astype(F32) * sza_ref[0].astype(F32)).astype(BF16)
    bb = (ab_ref[0].astype(F32) * szb_ref[0].astype(F32)).astype(BF16)
    ya = jnp.dot(a, woa_ref[...], preferred_element_type=F32)
    yb = jnp.dot(bb, wob_ref[...], preferred_element_type=F32)
    merged = sga_ref[0].astype(F32) * ya + sgb_ref[0].astype(F32) * yb
    d = jnp.dot(merged.astype(BF16), wout_ref[...], preferred_element_type=F32)
    xo = x_ref[0] + gate_ref[0] * d
    ms = jnp.mean(xo * xo, axis=-1, keepdims=True)
    o_ref[0] = xo * lax.rsqrt(ms + NORM_EPS) * gf_ref[...]


def _out(x, att_a, sza, att_b, szb, sga, sgb, gate, woa, wob, wout, gf, *, tm):
    b, s, _ = x.shape
    row = lambda bi, i: (bi, i, 0)
    per_b = lambda bi, i: (bi, 0, 0)
    const2 = lambda bi, i: (0, 0)
    once = pl.Buffered(1)
    return pl.pallas_call(
        _out_kernel,
        grid=(b, s // tm),
        in_specs=[pl.BlockSpec((1, tm, D_MODEL), row),
                  pl.BlockSpec((1, tm, SWA_WIDTH), row), pl.BlockSpec((1, tm, SWA_WIDTH), row),
                  pl.BlockSpec((1, tm, FOX_WIDTH), row), pl.BlockSpec((1, tm, FOX_WIDTH), row),
                  pl.BlockSpec((1, tm, D_MODEL), row), pl.BlockSpec((1, tm, D_MODEL), row),
                  pl.BlockSpec((1, 1, D_MODEL), per_b),
                  pl.BlockSpec((SWA_WIDTH, D_MODEL), const2, pipeline_mode=once),
                  pl.BlockSpec((FOX_WIDTH, D_MODEL), const2, pipeline_mode=once),
                  pl.BlockSpec((D_MODEL, D_MODEL), const2, pipeline_mode=once),
                  pl.BlockSpec((1, D_MODEL), const2)],
        out_specs=pl.BlockSpec((1, tm, D_MODEL), row),
        out_shape=jax.ShapeDtypeStruct((b, s, D_MODEL), F32),
        compiler_params=pltpu.CompilerParams(
            dimension_semantics=("arbitrary", "arbitrary"),
            vmem_limit_bytes=_vmem_limit(48 * 1024 * 1024)),
        name="out_proj",
    )(x, att_a, sza, att_b, szb, sga, sgb, gate, woa, wob, wout, gf)


def _rot_cols(w, nheads):
    w3 = w.reshape(w.shape[0], nheads, HEAD_DIM)
    return jnp.concatenate([-w3[..., HALF:], w3[..., :HALF]], axis=-1).reshape(w.shape)


def _dup_heads(w, nheads):
    w3 = w.reshape(w.shape[0], nheads, 1, HEAD_DIM)
    return jnp.broadcast_to(w3, (w.shape[0], nheads, 2, HEAD_DIM)).reshape(w.shape[0], 2 * nheads * HEAD_DIM)


def _layout_w_in(w):
    o = np.cumsum([0, SWA_WIDTH, SWA_KV_WIDTH, SWA_KV_WIDTH, SWA_WIDTH, FOX_WIDTH, FOX_WIDTH, FOX_WIDTH,
                   FOX_HEADS, FOX_WIDTH, D_MODEL, D_MODEL]).tolist()
    qa, ka, va, za, qb, kb, vb, fb, zb, ga, gb = [w[:, o[k]:o[k + 1]] for k in range(11)]
    cols = [qa, _rot_cols(qa, SWA_Q_HEADS),
            _dup_heads(ka, SWA_KV_HEADS), _dup_heads(_rot_cols(ka, SWA_KV_HEADS), SWA_KV_HEADS),
            za, qb, kb, jnp.tile(fb, (1, LANES // FOX_HEADS)), zb, ga, gb]
    w_all = jnp.concatenate(cols, axis=1).astype(BF16)
    assert w_all.shape[1] == _PROJ_COLS
    return w_all, jnp.concatenate([va, vb], axis=1).T.astype(BF16)


def kernel(x, c, positions, w_ada, b_ada, g_norm, w_in, b_f, sinks, w_o_swa, w_o_fox, w_out, g_final):
    b, s, _ = x.shape
    depth = w_in.shape[0]
    assert depth == 1, "the output stage fuses the final RMSNorm into the single layer"
    inv_freq = ROPE_THETA ** (-jnp.arange(0, HEAD_DIM, 2, dtype=F32) / HEAD_DIM)
    invf = jnp.broadcast_to(inv_freq[:, None], (HALF, _PROJ_TM))
    pos_f = positions.astype(F32)[:, None, :]
    e_mat = jnp.asarray(_aug_placement(), BF16)
    c_pad = jnp.zeros((8, D_MODEL), F32).at[:b].set(c)
    for l in range(depth):
        ada = _ada(c_pad, w_ada[l], b_ada[l][None, :])[:b]
        shift, scale, gate = [ada[:, None, k * D_MODEL:(k + 1) * D_MODEL] for k in range(3)]
        w_all, w_vt = _layout_w_in(w_in[l])
        bf_rep = jnp.tile(b_f[l].astype(F32), LANES // FOX_HEADS)[None, :]
        qa, ka, vat, sza, q_aug, k_aug, vbt, szb, sga, sgb = _proj(
            x, pos_f, scale, shift, g_norm[l][None, :], invf, bf_rep, w_all, w_vt, e_mat, tm=_PROJ_TM)
        att_a = _swa(sinks[l].astype(F32), qa, ka, vat, tq=_SWA_TQ)
        att_b = _fox(q_aug, k_aug, vbt, t=_FOX_T)
        x = _out(x, att_a, sza, att_b, szb, sga, sgb, gate,
                 w_o_swa[l].astype(BF16), w_o_fox[l].astype(BF16), w_out[l].astype(BF16),
                 g_final[None, :], tm=_OUT_TM)
    return x
```

```python
import functools
import math

import numpy as np
import jax
import jax.numpy as jnp
from jax import lax
from jax.experimental import pallas as pl
from jax.experimental.pallas import tpu as pltpu

D_MODEL = 1024
HEAD_DIM = 64
HALF = HEAD_DIM // 2
SWA_Q_HEADS = 8
SWA_KV_HEADS = 2
SWA_WIDTH = SWA_Q_HEADS * HEAD_DIM
SWA_KV_WIDTH = SWA_KV_HEADS * HEAD_DIM
FOX_HEADS = 8
FOX_WIDTH = FOX_HEADS * HEAD_DIM
WINDOW = 128
ROPE_THETA = 10000.0
NORM_EPS = 1e-6
QK_SCALE = HEAD_DIM ** -0.5
LOG2E = math.log2(math.e)

LANES = 128
V7X_VMEM_BYTES = 64 * 1024 * 1024

_PROJ_TM = 512
_SWA_TQ = 512
_FOX_T = 512
_OUT_TM = 512

F32 = jnp.float32
BF16 = jnp.bfloat16

_QA, _KA, _ZA = 0, 1024, 1536
_QB, _KB, _FB, _ZB, _GA, _GB = 2048, 2560, 3072, 3200, 3712, 4736
_PROJ_COLS = 5760

_ONE_LANE = 24

_ONES_ROWS = 16


def _vmem_limit(nbytes):
    return int(min(nbytes, V7X_VMEM_BYTES - 8 * 1024 * 1024))


def _split3(v):
    hi = v.astype(BF16)
    r1 = v - hi.astype(F32)
    mid = r1.astype(BF16)
    lo = (r1 - mid.astype(F32)).astype(BF16)
    return hi, mid, lo


def _ada_kernel(c_ref, w_ref, b_ref, o_ref):
    o_ref[...] = jnp.dot(c_ref[...], w_ref[...], preferred_element_type=F32,
                         precision=lax.Precision.HIGHEST) + b_ref[...]


def _ada(c_pad, w_ada, b_ada):
    rows = c_pad.shape[0]
    n = w_ada.shape[1]
    nblk = n // D_MODEL
    return pl.pallas_call(
        _ada_kernel,
        grid=(nblk,),
        in_specs=[pl.BlockSpec((rows, D_MODEL), lambda j: (0, 0)),
                  pl.BlockSpec((D_MODEL, D_MODEL), lambda j: (0, j)),
                  pl.BlockSpec((1, D_MODEL), lambda j: (0, j))],
        out_specs=pl.BlockSpec((rows, D_MODEL), lambda j: (0, j)),
        out_shape=jax.ShapeDtypeStruct((rows, n), F32),
        name="ada_mod",
    )(c_pad, w_ada, b_ada)


def _sigmoid(z):
    return 0.5 * jnp.tanh(0.5 * z) + 0.5


def _proj_kernel(x_ref, pos_ref, scale_ref, shift_ref, gn_ref, invf_ref, bf_ref, w_ref, wvt_ref, e_ref,
                 qa_ref, ka_ref, vat_ref, sza_ref, qaug_ref, kaug_ref, vbt_ref, szb_ref,
                 sga_ref, sgb_ref, carry_ref, *, tm):
    i = pl.program_id(1)

    @pl.when(i == 0)
    def _():
        carry_ref[...] = jnp.zeros_like(carry_ref)

    x = x_ref[0]
    ms = jnp.mean(x * x, axis=-1, keepdims=True)
    y = x * lax.rsqrt(ms + NORM_EPS) * gn_ref[...]
    h = y * (1.0 + scale_ref[0]) + shift_ref[0]
    hb = h.astype(BF16)

    def proj(off, n):
        return jnp.dot(hb, w_ref[:, off:off + n], preferred_element_type=F32)

    lane = lax.broadcasted_iota(jnp.int32, (tm, LANES), 1)
    low = lane < HEAD_DIM


    zf = proj(_FB, LANES) + bf_ref[...]
    logf = jnp.minimum(zf, 0.0) - jnp.log1p(jnp.exp(-jnp.abs(zf)))
    grp = lane // FOX_HEADS
    hi, mid, lo = _split3(logf)
    zero = jnp.zeros((tm, LANES), F32)
    parts = jnp.where(grp == 0, hi.astype(F32),
                      jnp.where(grp == 1, mid.astype(F32),
                                jnp.where(grp == 2, lo.astype(F32), zero))).astype(BF16)
    tri = (lax.broadcasted_iota(jnp.int32, (tm, tm), 0)
           >= lax.broadcasted_iota(jnp.int32, (tm, tm), 1)).astype(BF16)
    rsum = jnp.dot(tri, parts, preferred_element_type=F32)
    c0 = rsum + pltpu.roll(rsum, LANES - FOX_HEADS, 1) + pltpu.roll(rsum, LANES - 2 * FOX_HEADS, 1)
    c0 = jnp.where(grp == 0, c0, zero)
    cl = c0 + pltpu.roll(c0, FOX_HEADS, 1) + pltpu.roll(c0, 2 * FOX_HEADS, 1)
    cum = cl + carry_ref[...]
    carry_ref[...] = cum[tm - 1:tm, :]

    hi, mid, lo = _split3(cum * LOG2E)
    one = jnp.ones((tm, LANES), F32)
    carrier = jnp.where(grp == 0, hi.astype(F32),
                        jnp.where(grp == 1, mid.astype(F32),
                                  jnp.where(grp == 2, lo.astype(F32),
                                            jnp.where(lane == _ONE_LANE, one, zero)))).astype(BF16)
    aug = jnp.dot(carrier, e_ref[...], preferred_element_type=F32)

    sga_ref[0] = _sigmoid(proj(_GA, D_MODEL)).astype(BF16)
    sgb_ref[0] = _sigmoid(proj(_GB, D_MODEL)).astype(BF16)
    z = proj(_ZA, SWA_WIDTH)
    sza_ref[0] = (z * _sigmoid(z)).astype(BF16)
    z = proj(_ZB, FOX_WIDTH)
    szb_ref[0] = (z * _sigmoid(z)).astype(BF16)

    ang = invf_ref[...] * pos_ref[0]
    cos = jnp.tile(jnp.cos(ang), (LANES // HALF, 1)).T
    sin = jnp.tile(jnp.sin(ang), (LANES // HALF, 1)).T
    r = proj(_QA, 2 * SWA_WIDTH)
    for c in range(SWA_WIDTH // LANES):
        a = c * LANES
        t = r[:, a:a + LANES] * cos + r[:, SWA_WIDTH + a:SWA_WIDTH + a + LANES] * sin
        qa_ref[0, :, a:a + LANES] = (t * (QK_SCALE * LOG2E)).astype(BF16)
    r = proj(_KA, 4 * SWA_KV_WIDTH)
    for c in range(2 * SWA_KV_WIDTH // LANES):
        a = c * LANES
        t = r[:, a:a + LANES] * cos + r[:, 2 * SWA_KV_WIDTH + a:2 * SWA_KV_WIDTH + a + LANES] * sin
        ka_ref[0, :, a:a + LANES] = t.astype(BF16)

    rq = proj(_QB, FOX_WIDTH)
    rk = proj(_KB, FOX_WIDTH)
    for hd in range(FOX_HEADS):
        a = (hd // 2) * LANES
        gq = aug[:, a:a + LANES]
        gk = aug[:, FOX_WIDTH + a:FOX_WIDTH + a + LANES]
        qv = rq[:, a:a + LANES] * (QK_SCALE * LOG2E)
        kv = rk[:, a:a + LANES]
        if hd % 2 == 0:
            qaug_ref[0, hd] = jnp.where(low, qv, gq).astype(BF16)
            kaug_ref[0, hd] = jnp.where(low, kv, gk).astype(BF16)
        else:
            qaug_ref[0, hd] = jnp.where(low, gq, qv).astype(BF16)
            kaug_ref[0, hd] = jnp.where(low, gk, kv).astype(BF16)

    vt = lax.dot_general(wvt_ref[...], hb, (((1,), (1,)), ((), ())), preferred_element_type=F32)
    vat_ref[0] = vt[0:SWA_KV_WIDTH].astype(BF16)
    vbt_ref[0] = vt[SWA_KV_WIDTH:].astype(BF16)


def _aug_placement():
    e = np.zeros((LANES, 2 * FOX_HEADS * HEAD_DIM), np.float32)
    for hd in range(FOX_HEADS):
        base = (hd // 2) * LANES + (HEAD_DIM if hd % 2 == 0 else 0)
        kbase = FOX_HEADS * HEAD_DIM + base
        for part in range(3):
            e[part * FOX_HEADS + hd, base + part] = 1.0
            e[_ONE_LANE, base + 3 + part] = 1.0
            e[_ONE_LANE, kbase + part] = 1.0
            e[part * FOX_HEADS + hd, kbase + 3 + part] = -1.0
    return e


def _proj(x, pos_f, scale, shift, gn, invf, bf_rep, w_all, w_vt, e_mat, *, tm):
    b, s, _ = x.shape
    grid = (b, s // tm)
    row = lambda bi, i: (bi, i, 0)
    per_b = lambda bi, i: (bi, 0, 0)
    const2 = lambda bi, i: (0, 0)
    once = pl.Buffered(1)
    in_specs = [
        pl.BlockSpec((1, tm, D_MODEL), row),
        pl.BlockSpec((1, 1, tm), lambda bi, i: (bi, 0, i)),
        pl.BlockSpec((1, 1, D_MODEL), per_b),
        pl.BlockSpec((1, 1, D_MODEL), per_b),
        pl.BlockSpec((1, D_MODEL), const2),
        pl.BlockSpec((HALF, tm), const2),
        pl.BlockSpec((1, LANES), const2),
        pl.BlockSpec((D_MODEL, _PROJ_COLS), const2, pipeline_mode=once),
        pl.BlockSpec((SWA_KV_WIDTH + FOX_WIDTH, D_MODEL), const2, pipeline_mode=once),
        pl.BlockSpec((LANES, 2 * FOX_WIDTH), const2, pipeline_mode=once),
    ]
    head4 = lambda bi, i: (bi, 0, i, 0)
    out_specs = [
        pl.BlockSpec((1, tm, SWA_WIDTH), row),
        pl.BlockSpec((1, tm, 2 * SWA_KV_WIDTH), row),
        pl.BlockSpec((1, SWA_KV_WIDTH, tm), lambda bi, i: (bi, 0, i)),
        pl.BlockSpec((1, tm, SWA_WIDTH), row),
        pl.BlockSpec((1, FOX_HEADS, tm, LANES), head4),
        pl.BlockSpec((1, FOX_HEADS, tm, LANES), head4),
        pl.BlockSpec((1, FOX_WIDTH, tm), lambda bi, i: (bi, 0, i)),
        pl.BlockSpec((1, tm, FOX_WIDTH), row),
        pl.BlockSpec((1, tm, D_MODEL), row),
        pl.BlockSpec((1, tm, D_MODEL), row),
    ]
    sds = jax.ShapeDtypeStruct
    out_shape = [
        sds((b, s, SWA_WIDTH), BF16), sds((b, s, 2 * SWA_KV_WIDTH), BF16),
        sds((b, SWA_KV_WIDTH, s), BF16), sds((b, s, SWA_WIDTH), BF16),
        sds((b, FOX_HEADS, s, LANES), BF16), sds((b, FOX_HEADS, s, LANES), BF16),
        sds((b, FOX_WIDTH, s), BF16), sds((b, s, FOX_WIDTH), BF16),
        sds((b, s, D_MODEL), BF16), sds((b, s, D_MODEL), BF16),
    ]
    return pl.pallas_call(
        functools.partial(_proj_kernel, tm=tm),
        grid=grid, in_specs=in_specs, out_specs=out_specs, out_shape=out_shape,
        scratch_shapes=[pltpu.VMEM((1, LANES), F32)],
        compiler_params=pltpu.CompilerParams(
            dimension_semantics=("arbitrary", "arbitrary"),
            vmem_limit_bytes=_vmem_limit(56 * 1024 * 1024)),
        name="in_proj",
    )(x, pos_f, scale, shift, gn, invf, bf_rep, w_all, w_vt, e_mat)


def _swa_kernel(sinks_ref, q_ref, kc_ref, kp_ref, vtc_ref, vtp_ref, bias_ref, sz_ref, o_ref,
                kband, vtband, *, tq):
    i = pl.program_id(1)
    nblk = tq // WINDOW
    nhq = SWA_Q_HEADS // SWA_KV_HEADS
    kband[0:WINDOW] = kp_ref[0]
    kband[WINDOW:] = kc_ref[0]
    vtband[:, 0:WINDOW] = vtp_ref[0]
    vtband[:, WINDOW:] = vtc_ref[0]

    cols = nhq * WINDOW
    first_plane = jnp.where(i > 0, 0, 1)
    headid = lax.broadcasted_iota(jnp.int32, (1, cols), 1) // WINDOW
    low = lax.broadcasted_iota(jnp.int32, (WINDOW, LANES), 1) < HEAD_DIM
    ones = jnp.ones((_ONES_ROWS, 2 * WINDOW), BF16)
    nt = (((1,), (1,)), ((), ()))

    def logits(r, g):
        kb = kband[r * WINDOW:(r + 2) * WINDOW, g * LANES:(g + 1) * LANES]
        qs = []
        for c in range(2):
            a = (2 * g + c) * LANES
            qc = q_ref[0, r * WINDOW:(r + 1) * WINDOW, a:a + LANES]
            qs.append(jnp.where(low, qc, jnp.zeros_like(qc)))
            qs.append(jnp.where(low, jnp.zeros_like(qc), qc))
        qst = jnp.concatenate(qs, axis=0)
        s = lax.dot_general(kb, qst, nt, preferred_element_type=F32)
        return s + bias_ref[first_plane if r == 0 else 0]

    def attend(r, g, s):
        sink = jnp.zeros((1, cols), F32)
        for k in range(nhq):
            sink = jnp.where(headid == k, sinks_ref[g * nhq + k] * LOG2E, sink)
        m = jnp.maximum(jnp.max(s, axis=0, keepdims=True), sink)
        p = jnp.exp2(s - m).astype(BF16)
        vt = vtband[g * HEAD_DIM:(g + 1) * HEAD_DIM, r * WINDOW:(r + 2) * WINDOW]
        pv = jnp.dot(jnp.concatenate([vt, ones], axis=0), p, preferred_element_type=F32)
        den = pv[HEAD_DIM:HEAD_DIM + 1] + jnp.exp2(sink - m)
        o_t = pv[0:HEAD_DIM] / den
        for c in range(2):
            a = (2 * g + c) * LANES
            pair = jnp.concatenate([o_t[:, 2 * c * WINDOW:(2 * c + 1) * WINDOW],
                                    o_t[:, (2 * c + 1) * WINDOW:(2 * c + 2) * WINDOW]], axis=0)
            gate = sz_ref[0, r * WINDOW:(r + 1) * WINDOW, a:a + LANES].astype(F32)
            o_ref[0, r * WINDOW:(r + 1) * WINDOW, a:a + LANES] = (pair.T * gate).astype(BF16)

    work = [(r, g) for r in range(nblk) for g in range(SWA_KV_HEADS)]
    s_next = logits(*work[0])
    for n, (r, g) in enumerate(work):
        s_cur = s_next
        if n + 1 < len(work):
            s_next = logits(*work[n + 1])
        attend(r, g, s_cur)


def _swa(sinks, qa, ka, vat, sza, *, tq):
    b, s, _ = qa.shape
    kvw = 2 * SWA_KV_WIDTH
    per = tq // WINDOW
    cur = lambda bi, i: (bi, i, 0)
    prev = lambda bi, i: (bi, jnp.maximum(i * per - 1, 0), 0)
    cur_t = lambda bi, i: (bi, 0, i)
    prev_t = lambda bi, i: (bi, 0, jnp.maximum(i * per - 1, 0))
    key = np.arange(2 * WINDOW)[:, None]
    qt = np.tile(np.arange(WINDOW), SWA_Q_HEADS // SWA_KV_HEADS)[None, :]
    band = np.where(key < WINDOW, key > qt, (key - WINDOW) <= qt)
    planes = np.stack([band, band & (key >= WINDOW)])
    bias = jnp.asarray(np.where(planes, 0.0, -np.inf).astype(np.float32))
    return pl.pallas_call(
        functools.partial(_swa_kernel, tq=tq),
        grid=(b, s // tq),
        in_specs=[pl.BlockSpec(memory_space=pltpu.SMEM),
                  pl.BlockSpec((1, tq, SWA_WIDTH), cur),
                  pl.BlockSpec((1, tq, kvw), cur),
                  pl.BlockSpec((1, WINDOW, kvw), prev),
                  pl.BlockSpec((1, SWA_KV_WIDTH, tq), cur_t),
                  pl.BlockSpec((1, SWA_KV_WIDTH, WINDOW), prev_t),
                  pl.BlockSpec(bias.shape, lambda bi, i: (0, 0, 0), pipeline_mode=pl.Buffered(1)),
                  pl.BlockSpec((1, tq, SWA_WIDTH), cur)],
        out_specs=pl.BlockSpec((1, tq, SWA_WIDTH), cur),
        out_shape=jax.ShapeDtypeStruct((b, s, SWA_WIDTH), BF16),
        scratch_shapes=[pltpu.VMEM((tq + WINDOW, kvw), BF16),
                        pltpu.VMEM((SWA_KV_WIDTH, tq + WINDOW), BF16)],
        compiler_params=pltpu.CompilerParams(dimension_semantics=("arbitrary", "arbitrary")),
        name="swa_attn",
    )(sinks, qa, ka, ka, vat, vat, bias, sza)


_ACC_ROWS = HEAD_DIM + _ONES_ROWS
_PIPE = 2
_FOX_UNROLL = 2


def _fox_schedule(n_rows):
    phases = [[(i, i, i) for i in range(n_rows)],
              [(i, j, i) for i in range(n_rows) for j in range(i)]]
    dummy = (0, 0, n_rows)
    tbl, spans = [], []
    for seq in phases:
        n_steps = len(seq) + _PIPE
        n_steps += -n_steps % _FOX_UNROLL
        spans.append((len(tbl), n_steps))
        tbl += [dummy] * _PIPE + seq + [dummy] * (n_steps - len(seq))
    return np.asarray(tbl, np.int32).T.copy(), spans


def _fox_kernel(tbl_ref, q_ref, k_ref, vt_ref, bias_ref, sz_ref, o_ref,
                s0, s1, mx0, mx1, p0, p1, a0, a1, m_ref, acc_ref, *, t, n_rows, spans):
    nt = (((1,), (1,)), ((), ()))
    g = t // 8
    for ref in (s0, s1, mx0, mx1, p0, p1, a0, a1):
        ref[...] = jnp.zeros_like(ref)
    m_ref[n_rows] = jnp.zeros_like(m_ref[n_rows])
    acc_ref[n_rows] = jnp.zeros_like(acc_ref[n_rows])

    def step(k, diagonal, s_w, mx_w, s_r, mx_r, p_w, a_w, p_r, a_r):
        qrow = pl.multiple_of(tbl_ref[0, k + 2] * t, t)
        krow = pl.multiple_of(tbl_ref[1, k + 2] * t, t)
        for hh in range(2):
            s = lax.dot_general(k_ref[0, hh, pl.ds(krow, t), :], q_ref[0, hh, pl.ds(qrow, t), :], nt,
                                preferred_element_type=F32)
            if diagonal:
                s = s + bias_ref[...]
            s_w[hh] = s
            mx_w[hh] = jnp.max(s.reshape(g, 8, t), axis=0)

        srow = tbl_ref[2, k + 1]
        for hh in range(2):
            m_tile = jnp.max(mx_r[hh], axis=0, keepdims=True)
            if diagonal:
                m_new = jnp.broadcast_to(m_tile, (8, t))
            else:
                m_old = m_ref[srow, hh]
                m_new = jnp.maximum(m_old, m_tile)
                a_w[hh] = jnp.exp2(m_old - m_new)
            m_ref[srow, hh] = m_new
            p = jnp.exp2(s_r[hh].reshape(g, 8, t) - m_new[None])
            p_w[hh] = p.reshape(t, t).astype(BF16)

        vcol = pl.multiple_of(tbl_ref[1, k] * t, t)
        arow = tbl_ref[2, k]
        for hh in range(2):
            vt = vt_ref[0, hh * HEAD_DIM:(hh + 1) * HEAD_DIM, pl.ds(vcol, t)]
            lhs = jnp.concatenate([vt, jnp.ones((_ONES_ROWS, t), BF16)], axis=0)
            part = jnp.dot(lhs, p_r[hh], preferred_element_type=F32)
            if diagonal:
                acc_ref[arow, hh] = part
            else:
                old = acc_ref[arow, hh].reshape(_ACC_ROWS // 8, 8, t)
                acc_ref[arow, hh] = (a_r[hh][None] * old).reshape(_ACC_ROWS, t) + part

    assert _FOX_UNROLL % 2 == 0
    for diagonal, (first, n_steps) in zip((True, False), spans):
        def body(kk, carry, diagonal=diagonal, first=first):
            k = first + _FOX_UNROLL * kk
            for u in range(0, _FOX_UNROLL, 2):
                step(k + u, diagonal, s0, mx0, s1, mx1, p1, a1, p0, a0)
                step(k + u + 1, diagonal, s1, mx1, s0, mx0, p0, a0, p1, a1)
            return carry

        lax.fori_loop(0, n_steps // _FOX_UNROLL, body, 0)

    def finish_row(i, carry):
        halves = []
        for hh in range(2):
            a = acc_ref[i, hh]
            halves.append(a[0:HEAD_DIM] / a[HEAD_DIM:HEAD_DIM + 1])
        o_t = jnp.concatenate(halves, axis=0)
        rows = pl.ds(pl.multiple_of(i * t, t), t)
        o_ref[0, rows, :] = (o_t.T * sz_ref[0, rows, :].astype(F32)).astype(BF16)
        return carry

    lax.fori_loop(0, n_rows, finish_row, 0)


def _fox(q_aug, k_aug, vbt, szb, *, t):
    b, _, s, _ = q_aug.shape
    npair = FOX_HEADS // 2
    n_rows = s // t
    tbl, spans = _fox_schedule(n_rows)
    tri = np.triu(np.ones((t, t), bool))
    bias = jnp.asarray(np.where(tri, 0.0, -np.inf).astype(np.float32))
    pair = lambda bi, p: (bi, p, 0, 0)
    stat = pltpu.VMEM((2, 8, t), F32)
    return pl.pallas_call(
        functools.partial(_fox_kernel, t=t, n_rows=n_rows, spans=tuple(spans)),
        grid=(b, npair),
        in_specs=[pl.BlockSpec(memory_space=pltpu.SMEM),
                  pl.BlockSpec((1, 2, s, LANES), pair),
                  pl.BlockSpec((1, 2, s, LANES), pair),
                  pl.BlockSpec((1, 2 * HEAD_DIM, s), lambda bi, p: (bi, p, 0)),
                  pl.BlockSpec((t, t), lambda bi, p: (0, 0), pipeline_mode=pl.Buffered(1)),
                  pl.BlockSpec((1, s, LANES), lambda bi, p: (bi, 0, p))],
        out_specs=pl.BlockSpec((1, s, LANES), lambda bi, p: (bi, 0, p)),
        out_shape=jax.ShapeDtypeStruct((b, s, FOX_WIDTH), BF16),
        scratch_shapes=[pltpu.VMEM((2, t, t), F32), pltpu.VMEM((2, t, t), F32),
                        stat, stat,
                        pltpu.VMEM((2, t, t), BF16), pltpu.VMEM((2, t, t), BF16),
                        stat, stat,
                        pltpu.VMEM((n_rows + 1, 2, 8, t), F32),
                        pltpu.VMEM((n_rows + 1, 2, _ACC_ROWS, t), F32)],
        compiler_params=pltpu.CompilerParams(
            dimension_semantics=("arbitrary", "arbitrary"),
            vmem_limit_bytes=_vmem_limit(56 * 1024 * 1024)),
        name="fox_attn",
    )(jnp.asarray(tbl), q_aug, k_aug, vbt, bias, szb)


def _out_kernel(x_ref, ga_ref, gb_ref, sga_ref, sgb_ref, gate_ref,
                woa_ref, wob_ref, wout_ref, gf_ref, o_ref):
    ya = jnp.dot(ga_ref[0], woa_ref[...], preferred_element_type=F32)
    yb = jnp.dot(gb_ref[0], wob_ref[...], preferred_element_type=F32)
    merged = sga_ref[0].astype(F32) * ya + sgb_ref[0].astype(F32) * yb
    d = jnp.dot(merged.astype(BF16), wout_ref[...], preferred_element_type=F32)
    xo = x_ref[0] + gate_ref[0] * d
    ms = jnp.mean(xo * xo, axis=-1, keepdims=True)
    o_ref[0] = xo * lax.rsqrt(ms + NORM_EPS) * gf_ref[...]


def _out(x, gated_a, gated_b, sga, sgb, gate, woa, wob, wout, gf, *, tm):
    b, s, _ = x.shape
    row = lambda bi, i: (bi, i, 0)
    per_b = lambda bi, i: (bi, 0, 0)
    const2 = lambda bi, i: (0, 0)
    once = pl.Buffered(1)
    return pl.pallas_call(
        _out_kernel,
        grid=(b, s // tm),
        in_specs=[pl.BlockSpec((1, tm, D_MODEL), row),
                  pl.BlockSpec((1, tm, SWA_WIDTH), row), pl.BlockSpec((1, tm, FOX_WIDTH), row),
                  pl.BlockSpec((1, tm, D_MODEL), row), pl.BlockSpec((1, tm, D_MODEL), row),
                  pl.BlockSpec((1, 1, D_MODEL), per_b),
                  pl.BlockSpec((SWA_WIDTH, D_MODEL), const2, pipeline_mode=once),
                  pl.BlockSpec((FOX_WIDTH, D_MODEL), const2, pipeline_mode=once),
                  pl.BlockSpec((D_MODEL, D_MODEL), const2, pipeline_mode=once),
                  pl.BlockSpec((1, D_MODEL), const2)],
        out_specs=pl.BlockSpec((1, tm, D_MODEL), row),
        out_shape=jax.ShapeDtypeStruct((b, s, D_MODEL), F32),
        compiler_params=pltpu.CompilerParams(
            dimension_semantics=("arbitrary", "arbitrary"),
            vmem_limit_bytes=_vmem_limit(48 * 1024 * 1024)),
        name="out_proj",
    )(x, gated_a, gated_b, sga, sgb, gate, woa, wob, wout, gf)


def _rot_cols(w, nheads):
    w3 = w.reshape(w.shape[0], nheads, HEAD_DIM)
    return jnp.concatenate([-w3[..., HALF:], w3[..., :HALF]], axis=-1).reshape(w.shape)


def _dup_heads(w, nheads):
    w3 = w.reshape(w.shape[0], nheads, 1, HEAD_DIM)
    return jnp.broadcast_to(w3, (w.shape[0], nheads, 2, HEAD_DIM)).reshape(w.shape[0], 2 * nheads * HEAD_DIM)


def _layout_w_in(w):
    o = np.cumsum([0, SWA_WIDTH, SWA_KV_WIDTH, SWA_KV_WIDTH, SWA_WIDTH, FOX_WIDTH, FOX_WIDTH, FOX_WIDTH,
                   FOX_HEADS, FOX_WIDTH, D_MODEL, D_MODEL]).tolist()
    qa, ka, va, za, qb, kb, vb, fb, zb, ga, gb = [w[:, o[k]:o[k + 1]] for k in range(11)]
    cols = [qa, _rot_cols(qa, SWA_Q_HEADS),
            _dup_heads(ka, SWA_KV_HEADS), _dup_heads(_rot_cols(ka, SWA_KV_HEADS), SWA_KV_HEADS),
            za, qb, kb, jnp.tile(fb, (1, LANES // FOX_HEADS)), zb, ga, gb]
    w_all = jnp.concatenate(cols, axis=1).astype(BF16)
    assert w_all.shape[1] == _PROJ_COLS
    return w_all, jnp.concatenate([va, vb], axis=1).T.astype(BF16)


def kernel(x, c, positions, w_ada, b_ada, g_norm, w_in, b_f, sinks, w_o_swa, w_o_fox, w_out, g_final):
    b, s, _ = x.shape
    depth = w_in.shape[0]
    assert depth == 1, "the output stage fuses the final RMSNorm into the single layer"
    inv_freq = ROPE_THETA ** (-jnp.arange(0, HEAD_DIM, 2, dtype=F32) / HEAD_DIM)
    invf = jnp.broadcast_to(inv_freq[:, None], (HALF, _PROJ_TM))
    pos_f = positions.astype(F32)[:, None, :]
    e_mat = jnp.asarray(_aug_placement(), BF16)
    c_pad = jnp.zeros((8, D_MODEL), F32).at[:b].set(c)
    for l in range(depth):
        ada = _ada(c_pad, w_ada[l], b_ada[l][None, :])[:b]
        shift, scale, gate = [ada[:, None, k * D_MODEL:(k + 1) * D_MODEL] for k in range(3)]
        w_all, w_vt = _layout_w_in(w_in[l])
        bf_rep = jnp.tile(b_f[l].astype(F32), LANES // FOX_HEADS)[None, :]
        qa, ka, vat, sza, q_aug, k_aug, vbt, szb, sga, sgb = _proj(
            x, pos_f, scale, shift, g_norm[l][None, :], invf, bf_rep, w_all, w_vt, e_mat, tm=_PROJ_TM)
        gated_a = _swa(sinks[l].astype(F32), qa, ka, vat, sza, tq=_SWA_TQ)
        gated_b = _fox(q_aug, k_aug, vbt, szb, t=_FOX_T)
        x = _out(x, gated_a, gated_b, sga, sgb, gate,
                 w_o_swa[l].astype(BF16), w_o_fox[l].astype(BF16), w_out[l].astype(BF16),
                 g_final[None, :], tm=_OUT_TM)
    return x
```

```python
import functools
import math

import numpy as np
import jax
import jax.numpy as jnp
from jax import lax
from jax.experimental import pallas as pl
from jax.experimental.pallas import tpu as pltpu

D_MODEL = 1024
HEAD_DIM = 64
HALF = HEAD_DIM // 2
SWA_Q_HEADS = 8
SWA_KV_HEADS = 2
SWA_WIDTH = SWA_Q_HEADS * HEAD_DIM
SWA_KV_WIDTH = SWA_KV_HEADS * HEAD_DIM
FOX_HEADS = 8
FOX_WIDTH = FOX_HEADS * HEAD_DIM
WINDOW = 128
ROPE_THETA = 10000.0
NORM_EPS = 1e-6
QK_SCALE = HEAD_DIM ** -0.5
LOG2E = math.log2(math.e)

LANES = 128
V7X_VMEM_BYTES = 64 * 1024 * 1024

_PROJ_TM = 512
_SWA_TQ = 512
_FOX_T = 512
_OUT_TM = 512

F32 = jnp.float32
BF16 = jnp.bfloat16

_QA, _KA, _ZA = 0, 1024, 1536
_QB, _KB, _FB, _ZB, _GA, _GB = 2048, 2560, 3072, 3200, 3712, 4736
_PROJ_COLS = 5760

_ONE_LANE = 24

_ONES_ROWS = 16


def _vmem_limit(nbytes):
    return int(min(nbytes, V7X_VMEM_BYTES - 8 * 1024 * 1024))


def _split3(v):
    hi = v.astype(BF16)
    r1 = v - hi.astype(F32)
    mid = r1.astype(BF16)
    lo = (r1 - mid.astype(F32)).astype(BF16)
    return hi, mid, lo


def _ada_kernel(c_ref, w_ref, b_ref, o_ref):
    o_ref[...] = jnp.dot(c_ref[...], w_ref[...], preferred_element_type=F32,
                         precision=lax.Precision.HIGHEST) + b_ref[...]


def _ada(c_pad, w_ada, b_ada):
    rows = c_pad.shape[0]
    n = w_ada.shape[1]
    nblk = n // D_MODEL
    return pl.pallas_call(
        _ada_kernel,
        grid=(nblk,),
        in_specs=[pl.BlockSpec((rows, D_MODEL), lambda j: (0, 0)),
                  pl.BlockSpec((D_MODEL, D_MODEL), lambda j: (0, j)),
                  pl.BlockSpec((1, D_MODEL), lambda j: (0, j))],
        out_specs=pl.BlockSpec((rows, D_MODEL), lambda j: (0, j)),
        out_shape=jax.ShapeDtypeStruct((rows, n), F32),
        name="ada_mod",
    )(c_pad, w_ada, b_ada)


def _sigmoid(z):
    return 0.5 * jnp.tanh(0.5 * z) + 0.5


def _proj_kernel(x_ref, pos_ref, scale_ref, shift_ref, gn_ref, invf_ref, bf_ref, w_ref, wvt_ref, e_ref,
                 qa_ref, ka_ref, vat_ref, sza_ref, qaug_ref, kaug_ref, vbt_ref, szb_ref,
                 sga_ref, sgb_ref, carry_ref, *, tm):
    i = pl.program_id(1)

    @pl.when(i == 0)
    def _():
        carry_ref[...] = jnp.zeros_like(carry_ref)

    x = x_ref[0]
    ms = jnp.mean(x * x, axis=-1, keepdims=True)
    y = x * lax.rsqrt(ms + NORM_EPS) * gn_ref[...]
    h = y * (1.0 + scale_ref[0]) + shift_ref[0]
    hb = h.astype(BF16)

    def proj(off, n):
        return jnp.dot(hb, w_ref[:, off:off + n], preferred_element_type=F32)

    lane = lax.broadcasted_iota(jnp.int32, (tm, LANES), 1)
    low = lane < HEAD_DIM


    zf = proj(_FB, LANES) + bf_ref[...]
    logf = jnp.minimum(zf, 0.0) - jnp.log1p(jnp.exp(-jnp.abs(zf)))
    grp = lane // FOX_HEADS
    hi, mid, lo = _split3(logf)
    zero = jnp.zeros((tm, LANES), F32)
    parts = jnp.where(grp == 0, hi.astype(F32),
                      jnp.where(grp == 1, mid.astype(F32),
                                jnp.where(grp == 2, lo.astype(F32), zero))).astype(BF16)
    tri = (lax.broadcasted_iota(jnp.int32, (tm, tm), 0)
           >= lax.broadcasted_iota(jnp.int32, (tm, tm), 1)).astype(BF16)
    rsum = jnp.dot(tri, parts, preferred_element_type=F32)
    c0 = rsum + pltpu.roll(rsum, LANES - FOX_HEADS, 1) + pltpu.roll(rsum, LANES - 2 * FOX_HEADS, 1)
    c0 = jnp.where(grp == 0, c0, zero)
    cl = c0 + pltpu.roll(c0, FOX_HEADS, 1) + pltpu.roll(c0, 2 * FOX_HEADS, 1)
    cum = cl + carry_ref[...]
    carry_ref[...] = cum[tm - 1:tm, :]

    hi, mid, lo = _split3(cum * LOG2E)
    one = jnp.ones((tm, LANES), F32)
    carrier = jnp.where(grp == 0, hi.astype(F32),
                        jnp.where(grp == 1, mid.astype(F32),
                                  jnp.where(grp == 2, lo.astype(F32),
                                            jnp.where(lane == _ONE_LANE, one, zero)))).astype(BF16)
    aug = jnp.dot(carrier, e_ref[...], preferred_element_type=F32)

    sga_ref[0] = _sigmoid(proj(_GA, D_MODEL)).astype(BF16)
    sgb_ref[0] = _sigmoid(proj(_GB, D_MODEL)).astype(BF16)
    z = proj(_ZA, SWA_WIDTH)
    sza_ref[0] = (z * _sigmoid(z)).astype(BF16)
    z = proj(_ZB, FOX_WIDTH)
    szb_ref[0] = (z * _sigmoid(z)).astype(BF16)

    ang = invf_ref[...] * pos_ref[0]
    cos = jnp.tile(jnp.cos(ang), (LANES // HALF, 1)).T
    sin = jnp.tile(jnp.sin(ang), (LANES // HALF, 1)).T
    r = proj(_QA, 2 * SWA_WIDTH)
    for c in range(SWA_WIDTH // LANES):
        a = c * LANES
        t = r[:, a:a + LANES] * cos + r[:, SWA_WIDTH + a:SWA_WIDTH + a + LANES] * sin
        qa_ref[0, :, a:a + LANES] = (t * (QK_SCALE * LOG2E)).astype(BF16)
    r = proj(_KA, 4 * SWA_KV_WIDTH)
    for c in range(2 * SWA_KV_WIDTH // LANES):
        a = c * LANES
        t = r[:, a:a + LANES] * cos + r[:, 2 * SWA_KV_WIDTH + a:2 * SWA_KV_WIDTH + a + LANES] * sin
        ka_ref[0, :, a:a + LANES] = t.astype(BF16)

    rq = proj(_QB, FOX_WIDTH)
    rk = proj(_KB, FOX_WIDTH)
    for hd in range(FOX_HEADS):
        a = (hd // 2) * LANES
        gq = aug[:, a:a + LANES]
        gk = aug[:, FOX_WIDTH + a:FOX_WIDTH + a + LANES]
        qv = rq[:, a:a + LANES] * (QK_SCALE * LOG2E)
        kv = rk[:, a:a + LANES]
        if hd % 2 == 0:
            qaug_ref[0, hd] = jnp.where(low, qv, gq).astype(BF16)
            kaug_ref[0, hd] = jnp.where(low, kv, gk).astype(BF16)
        else:
            qaug_ref[0, hd] = jnp.where(low, gq, qv).astype(BF16)
            kaug_ref[0, hd] = jnp.where(low, gk, kv).astype(BF16)

    vt = lax.dot_general(wvt_ref[...], hb, (((1,), (1,)), ((), ())), preferred_element_type=F32)
    vat_ref[0] = vt[0:SWA_KV_WIDTH].astype(BF16)
    vbt_ref[0] = vt[SWA_KV_WIDTH:].astype(BF16)


def _aug_placement():
    e = np.zeros((LANES, 2 * FOX_HEADS * HEAD_DIM), np.float32)
    for hd in range(FOX_HEADS):
        base = (hd // 2) * LANES + (HEAD_DIM if hd % 2 == 0 else 0)
        kbase = FOX_HEADS * HEAD_DIM + base
        for part in range(3):
            e[part * FOX_HEADS + hd, base + part] = 1.0
            e[_ONE_LANE, base + 3 + part] = 1.0
            e[_ONE_LANE, kbase + part] = 1.0
            e[part * FOX_HEADS + hd, kbase + 3 + part] = -1.0
    return e


def _proj(x, pos_f, scale, shift, gn, invf, bf_rep, w_all, w_vt, e_mat, *, tm):
    b, s, _ = x.shape
    grid = (b, s // tm)
    row = lambda bi, i: (bi, i, 0)
    per_b = lambda bi, i: (bi, 0, 0)
    const2 = lambda bi, i: (0, 0)
    once = pl.Buffered(1)
    in_specs = [
        pl.BlockSpec((1, tm, D_MODEL), row),
        pl.BlockSpec((1, 1, tm), lambda bi, i: (bi, 0, i)),
        pl.BlockSpec((1, 1, D_MODEL), per_b),
        pl.BlockSpec((1, 1, D_MODEL), per_b),
        pl.BlockSpec((1, D_MODEL), const2),
        pl.BlockSpec((HALF, tm), const2),
        pl.BlockSpec((1, LANES), const2),
        pl.BlockSpec((D_MODEL, _PROJ_COLS), const2, pipeline_mode=once),
        pl.BlockSpec((SWA_KV_WIDTH + FOX_WIDTH, D_MODEL), const2, pipeline_mode=once),
        pl.BlockSpec((LANES, 2 * FOX_WIDTH), const2, pipeline_mode=once),
    ]
    head4 = lambda bi, i: (bi, 0, i, 0)
    out_specs = [
        pl.BlockSpec((1, tm, SWA_WIDTH), row),
        pl.BlockSpec((1, tm, 2 * SWA_KV_WIDTH), row),
        pl.BlockSpec((1, SWA_KV_WIDTH, tm), lambda bi, i: (bi, 0, i)),
        pl.BlockSpec((1, tm, SWA_WIDTH), row),
        pl.BlockSpec((1, FOX_HEADS, tm, LANES), head4),
        pl.BlockSpec((1, FOX_HEADS, tm, LANES), head4),
        pl.BlockSpec((1, FOX_WIDTH, tm), lambda bi, i: (bi, 0, i)),
        pl.BlockSpec((1, tm, FOX_WIDTH), row),
        pl.BlockSpec((1, tm, D_MODEL), row),
        pl.BlockSpec((1, tm, D_MODEL), row),
    ]
    sds = jax.ShapeDtypeStruct
    out_shape = [
        sds((b, s, SWA_WIDTH), BF16), sds((b, s, 2 * SWA_KV_WIDTH), BF16),
        sds((b, SWA_KV_WIDTH, s), BF16), sds((b, s, SWA_WIDTH), BF16),
        sds((b, FOX_HEADS, s, LANES), BF16), sds((b, FOX_HEADS, s, LANES), BF16),
        sds((b, FOX_WIDTH, s), BF16), sds((b, s, FOX_WIDTH), BF16),
        sds((b, s, D_MODEL), BF16), sds((b, s, D_MODEL), BF16),
    ]
    return pl.pallas_call(
        functools.partial(_proj_kernel, tm=tm),
        grid=grid, in_specs=in_specs, out_specs=out_specs, out_shape=out_shape,
        scratch_shapes=[pltpu.VMEM((1, LANES), F32)],
        compiler_params=pltpu.CompilerParams(
            dimension_semantics=("arbitrary", "arbitrary"),
            vmem_limit_bytes=_vmem_limit(56 * 1024 * 1024)),
        name="in_proj",
    )(x, pos_f, scale, shift, gn, invf, bf_rep, w_all, w_vt, e_mat)


def _swa_kernel(sinks_ref, q_ref, kc_ref, kp_ref, vtc_ref, vtp_ref, bias_ref, sz_ref, o_ref,
                kband, vtband, *, tq):
    i = pl.program_id(1)
    nblk = tq // WINDOW
    nhq = SWA_Q_HEADS // SWA_KV_HEADS
    kband[0:WINDOW] = kp_ref[0]
    kband[WINDOW:] = kc_ref[0]
    vtband[:, 0:WINDOW] = vtp_ref[0]
    vtband[:, WINDOW:] = vtc_ref[0]

    cols = nhq * WINDOW
    first_plane = jnp.where(i > 0, 0, 1)
    headid = lax.broadcasted_iota(jnp.int32, (1, cols), 1) // WINDOW
    low = lax.broadcasted_iota(jnp.int32, (WINDOW, LANES), 1) < HEAD_DIM
    ones = jnp.ones((_ONES_ROWS, 2 * WINDOW), BF16)
    nt = (((1,), (1,)), ((), ()))

    def logits(r, g):
        kb = kband[r * WINDOW:(r + 2) * WINDOW, g * LANES:(g + 1) * LANES]
        qs = []
        for c in range(2):
            a = (2 * g + c) * LANES
            qc = q_ref[0, r * WINDOW:(r + 1) * WINDOW, a:a + LANES]
            qs.append(jnp.where(low, qc, jnp.zeros_like(qc)))
            qs.append(jnp.where(low, jnp.zeros_like(qc), qc))
        qst = jnp.concatenate(qs, axis=0)
        s = lax.dot_general(kb, qst, nt, preferred_element_type=F32)
        return s + bias_ref[first_plane if r == 0 else 0]

    def attend(r, g, s):
        sink = jnp.zeros((1, cols), F32)
        for k in range(nhq):
            sink = jnp.where(headid == k, sinks_ref[g * nhq + k] * LOG2E, sink)
        m = jnp.maximum(jnp.max(s, axis=0, keepdims=True), sink)
        p = jnp.exp2(s - m).astype(BF16)
        vt = vtband[g * HEAD_DIM:(g + 1) * HEAD_DIM, r * WINDOW:(r + 2) * WINDOW]
        pv = jnp.dot(jnp.concatenate([vt, ones], axis=0), p, preferred_element_type=F32)
        den = pv[HEAD_DIM:HEAD_DIM + 1] + jnp.exp2(sink - m)
        o_t = pv[0:HEAD_DIM] / den
        for c in range(2):
            a = (2 * g + c) * LANES
            pair = jnp.concatenate([o_t[:, 2 * c * WINDOW:(2 * c + 1) * WINDOW],
                                    o_t[:, (2 * c + 1) * WINDOW:(2 * c + 2) * WINDOW]], axis=0)
            gate = sz_ref[0, r * WINDOW:(r + 1) * WINDOW, a:a + LANES].astype(F32)
            o_ref[0, r * WINDOW:(r + 1) * WINDOW, a:a + LANES] = (pair.T * gate).astype(BF16)

    work = [(r, g) for r in range(nblk) for g in range(SWA_KV_HEADS)]
    s_next = logits(*work[0])
    for n, (r, g) in enumerate(work):
        s_cur = s_next
        if n + 1 < len(work):
            s_next = logits(*work[n + 1])
        attend(r, g, s_cur)


def _swa(sinks, qa, ka, vat, sza, *, tq):
    b, s, _ = qa.shape
    kvw = 2 * SWA_KV_WIDTH
    per = tq // WINDOW
    cur = lambda bi, i: (bi, i, 0)
    prev = lambda bi, i: (bi, jnp.maximum(i * per - 1, 0), 0)
    cur_t = lambda bi, i: (bi, 0, i)
    prev_t = lambda bi, i: (bi, 0, jnp.maximum(i * per - 1, 0))
    key = np.arange(2 * WINDOW)[:, None]
    qt = np.tile(np.arange(WINDOW), SWA_Q_HEADS // SWA_KV_HEADS)[None, :]
    band = np.where(key < WINDOW, key > qt, (key - WINDOW) <= qt)
    planes = np.stack([band, band & (key >= WINDOW)])
    bias = jnp.asarray(np.where(planes, 0.0, -np.inf).astype(np.float32))
    return pl.pallas_call(
        functools.partial(_swa_kernel, tq=tq),
        grid=(b, s // tq),
        in_specs=[pl.BlockSpec(memory_space=pltpu.SMEM),
                  pl.BlockSpec((1, tq, SWA_WIDTH), cur),
                  pl.BlockSpec((1, tq, kvw), cur),
                  pl.BlockSpec((1, WINDOW, kvw), prev),
                  pl.BlockSpec((1, SWA_KV_WIDTH, tq), cur_t),
                  pl.BlockSpec((1, SWA_KV_WIDTH, WINDOW), prev_t),
                  pl.BlockSpec(bias.shape, lambda bi, i: (0, 0, 0), pipeline_mode=pl.Buffered(1)),
                  pl.BlockSpec((1, tq, SWA_WIDTH), cur)],
        out_specs=pl.BlockSpec((1, tq, SWA_WIDTH), cur),
        out_shape=jax.ShapeDtypeStruct((b, s, SWA_WIDTH), BF16),
        scratch_shapes=[pltpu.VMEM((tq + WINDOW, kvw), BF16),
                        pltpu.VMEM((SWA_KV_WIDTH, tq + WINDOW), BF16)],
        compiler_params=pltpu.CompilerParams(dimension_semantics=("arbitrary", "arbitrary")),
        name="swa_attn",
    )(sinks, qa, ka, ka, vat, vat, bias, sza)


_ACC_ROWS = HEAD_DIM + _ONES_ROWS
_PIPE = 2
_FOX_UNROLL = 2


def _fox_schedule(n_rows):
    phases = [[(i, i, i) for i in range(n_rows)],
              [(i, j, i) for i in range(n_rows) for j in range(i)]]
    dummy = (0, 0, n_rows)
    tbl, spans = [], []
    for seq in phases:
        n_steps = len(seq) + _PIPE
        n_steps += -n_steps % _FOX_UNROLL
        spans.append((len(tbl), n_steps))
        tbl += [dummy] * _PIPE + seq + [dummy] * (n_steps - len(seq))
    return np.asarray(tbl, np.int32).T.copy(), spans


def _fox_kernel(tbl_ref, q_ref, k_ref, vt_ref, bias_ref, sz_ref, o_ref,
                s0, s1, mx0, mx1, p0, p1, a0, a1, m_ref, acc_ref, *, t, n_rows, spans):
    nt = (((1,), (1,)), ((), ()))
    g = t // 8
    for ref in (s0, s1, mx0, mx1, p0, p1, a0, a1):
        ref[...] = jnp.zeros_like(ref)
    m_ref[n_rows] = jnp.zeros_like(m_ref[n_rows])
    acc_ref[n_rows] = jnp.zeros_like(acc_ref[n_rows])

    def step(k, diagonal, s_w, mx_w, s_r, mx_r, p_w, a_w, p_r, a_r):
        qrow = pl.multiple_of(tbl_ref[0, k + 2] * t, t)
        krow = pl.multiple_of(tbl_ref[1, k + 2] * t, t)
        zk = jnp.zeros((t, LANES), BF16)
        qq = jnp.concatenate([q_ref[0, 0, pl.ds(qrow, t), :], q_ref[0, 1, pl.ds(qrow, t), :]], axis=1)
        kz = jnp.concatenate([jnp.concatenate([k_ref[0, 0, pl.ds(krow, t), :], zk], axis=1),
                              jnp.concatenate([zk, k_ref[0, 1, pl.ds(krow, t), :]], axis=1)], axis=0)
        s_all = lax.dot_general(kz, qq, nt, preferred_element_type=F32)
        for hh in range(2):
            s = s_all[hh * t:(hh + 1) * t]
            if diagonal:
                s = s + bias_ref[...]
            s_w[hh] = s
            mx_w[hh] = jnp.max(s.reshape(g, 8, t), axis=0)

        srow = tbl_ref[2, k + 1]
        for hh in range(2):
            m_tile = jnp.max(mx_r[hh], axis=0, keepdims=True)
            if diagonal:
                m_new = jnp.broadcast_to(m_tile, (8, t))
            else:
                m_old = m_ref[srow, hh]
                m_new = jnp.maximum(m_old, m_tile)
                a_w[hh] = jnp.exp2(m_old - m_new)
            m_ref[srow, hh] = m_new
            p = jnp.exp2(s_r[hh].reshape(g, 8, t) - m_new[None])
            p_w[hh] = p.reshape(t, t).astype(BF16)

        vcol = pl.multiple_of(tbl_ref[1, k] * t, t)
        arow = tbl_ref[2, k]
        for hh in range(2):
            vt = vt_ref[0, hh * HEAD_DIM:(hh + 1) * HEAD_DIM, pl.ds(vcol, t)]
            lhs = jnp.concatenate([vt, jnp.ones((_ONES_ROWS, t), BF16)], axis=0)
            part = jnp.dot(lhs, p_r[hh], preferred_element_type=F32)
            if diagonal:
                acc_ref[arow, hh] = part
            else:
                old = acc_ref[arow, hh].reshape(_ACC_ROWS // 8, 8, t)
                acc_ref[arow, hh] = (a_r[hh][None] * old).reshape(_ACC_ROWS, t) + part

    assert _FOX_UNROLL % 2 == 0
    for diagonal, (first, n_steps) in zip((True, False), spans):
        def body(kk, carry, diagonal=diagonal, first=first):
            k = first + _FOX_UNROLL * kk
            for u in range(0, _FOX_UNROLL, 2):
                step(k + u, diagonal, s0, mx0, s1, mx1, p1, a1, p0, a0)
                step(k + u + 1, diagonal, s1, mx1, s0, mx0, p0, a0, p1, a1)
            return carry

        lax.fori_loop(0, n_steps // _FOX_UNROLL, body, 0)

    def finish_row(i, carry):
        halves = []
        for hh in range(2):
            a = acc_ref[i, hh]
            halves.append(a[0:HEAD_DIM] / a[HEAD_DIM:HEAD_DIM + 1])
        o_t = jnp.concatenate(halves, axis=0)
        rows = pl.ds(pl.multiple_of(i * t, t), t)
        o_ref[0, rows, :] = (o_t.T * sz_ref[0, rows, :].astype(F32)).astype(BF16)
        return carry

    lax.fori_loop(0, n_rows, finish_row, 0)


def _fox(q_aug, k_aug, vbt, szb, *, t):
    b, _, s, _ = q_aug.shape
    npair = FOX_HEADS // 2
    n_rows = s // t
    tbl, spans = _fox_schedule(n_rows)
    tri = np.triu(np.ones((t, t), bool))
    bias = jnp.asarray(np.where(tri, 0.0, -np.inf).astype(np.float32))
    pair = lambda bi, p: (bi, p, 0, 0)
    stat = pltpu.VMEM((2, 8, t), F32)
    return pl.pallas_call(
        functools.partial(_fox_kernel, t=t, n_rows=n_rows, spans=tuple(spans)),
        grid=(b, npair),
        in_specs=[pl.BlockSpec(memory_space=pltpu.SMEM),
                  pl.BlockSpec((1, 2, s, LANES), pair),
                  pl.BlockSpec((1, 2, s, LANES), pair),
                  pl.BlockSpec((1, 2 * HEAD_DIM, s), lambda bi, p: (bi, p, 0)),
                  pl.BlockSpec((t, t), lambda bi, p: (0, 0), pipeline_mode=pl.Buffered(1)),
                  pl.BlockSpec((1, s, LANES), lambda bi, p: (bi, 0, p))],
        out_specs=pl.BlockSpec((1, s, LANES), lambda bi, p: (bi, 0, p)),
        out_shape=jax.ShapeDtypeStruct((b, s, FOX_WIDTH), BF16),
        scratch_shapes=[pltpu.VMEM((2, t, t), F32), pltpu.VMEM((2, t, t), F32),
                        stat, stat,
                        pltpu.VMEM((2, t, t), BF16), pltpu.VMEM((2, t, t), BF16),
                        stat, stat,
                        pltpu.VMEM((n_rows + 1, 2, 8, t), F32),
                        pltpu.VMEM((n_rows + 1, 2, _ACC_ROWS, t), F32)],
        compiler_params=pltpu.CompilerParams(
            dimension_semantics=("arbitrary", "arbitrary"),
            vmem_limit_bytes=_vmem_limit(56 * 1024 * 1024)),
        name="fox_attn",
    )(jnp.asarray(tbl), q_aug, k_aug, vbt, bias, szb)


def _out_kernel(x_ref, ga_ref, gb_ref, sga_ref, sgb_ref, gate_ref,
                woa_ref, wob_ref, wout_ref, gf_ref, o_ref):
    ya = jnp.dot(ga_ref[0], woa_ref[...], preferred_element_type=F32)
    yb = jnp.dot(gb_ref[0], wob_ref[...], preferred_element_type=F32)
    merged = sga_ref[0].astype(F32) * ya + sgb_ref[0].astype(F32) * yb
    d = jnp.dot(merged.astype(BF16), wout_ref[...], preferred_element_type=F32)
    xo = x_ref[0] + gate_ref[0] * d
    ms = jnp.mean(xo * xo, axis=-1, keepdims=True)
    o_ref[0] = xo * lax.rsqrt(ms + NORM_EPS) * gf_ref[...]


def _out(x, gated_a, gated_b, sga, sgb, gate, woa, wob, wout, gf, *, tm):
    b, s, _ = x.shape
    row = lambda bi, i: (bi, i, 0)
    per_b = lambda bi, i: (bi, 0, 0)
    const2 = lambda bi, i: (0, 0)
    once = pl.Buffered(1)
    return pl.pallas_call(
        _out_kernel,
        grid=(b, s // tm),
        in_specs=[pl.BlockSpec((1, tm, D_MODEL), row),
                  pl.BlockSpec((1, tm, SWA_WIDTH), row), pl.BlockSpec((1, tm, FOX_WIDTH), row),
                  pl.BlockSpec((1, tm, D_MODEL), row), pl.BlockSpec((1, tm, D_MODEL), row),
                  pl.BlockSpec((1, 1, D_MODEL), per_b),
                  pl.BlockSpec((SWA_WIDTH, D_MODEL), const2, pipeline_mode=once),
                  pl.BlockSpec((FOX_WIDTH, D_MODEL), const2, pipeline_mode=once),
                  pl.BlockSpec((D_MODEL, D_MODEL), const2, pipeline_mode=once),
                  pl.BlockSpec((1, D_MODEL), const2)],
        out_specs=pl.BlockSpec((1, tm, D_MODEL), row),
        out_shape=jax.ShapeDtypeStruct((b, s, D_MODEL), F32),
        compiler_params=pltpu.CompilerParams(
            dimension_semantics=("arbitrary", "arbitrary"),
            vmem_limit_bytes=_vmem_limit(48 * 1024 * 1024)),
        name="out_proj",
    )(x, gated_a, gated_b, sga, sgb, gate, woa, wob, wout, gf)


def _rot_cols(w, nheads):
    w3 = w.reshape(w.shape[0], nheads, HEAD_DIM)
    return jnp.concatenate([-w3[..., HALF:], w3[..., :HALF]], axis=-1).reshape(w.shape)


def _dup_heads(w, nheads):
    w3 = w.reshape(w.shape[0], nheads, 1, HEAD_DIM)
    return jnp.broadcast_to(w3, (w.shape[0], nheads, 2, HEAD_DIM)).reshape(w.shape[0], 2 * nheads * HEAD_DIM)


def _layout_w_in(w):
    o = np.cumsum([0, SWA_WIDTH, SWA_KV_WIDTH, SWA_KV_WIDTH, SWA_WIDTH, FOX_WIDTH, FOX_WIDTH, FOX_WIDTH,
                   FOX_HEADS, FOX_WIDTH, D_MODEL, D_MODEL]).tolist()
    qa, ka, va, za, qb, kb, vb, fb, zb, ga, gb = [w[:, o[k]:o[k + 1]] for k in range(11)]
    cols = [qa, _rot_cols(qa, SWA_Q_HEADS),
            _dup_heads(ka, SWA_KV_HEADS), _dup_heads(_rot_cols(ka, SWA_KV_HEADS), SWA_KV_HEADS),
            za, qb, kb, jnp.tile(fb, (1, LANES // FOX_HEADS)), zb, ga, gb]
    w_all = jnp.concatenate(cols, axis=1).astype(BF16)
    assert w_all.shape[1] == _PROJ_COLS
    return w_all, jnp.concatenate([va, vb], axis=1).T.astype(BF16)


def kernel(x, c, positions, w_ada, b_ada, g_norm, w_in, b_f, sinks, w_o_swa, w_o_fox, w_out, g_final):
    b, s, _ = x.shape
    depth = w_in.shape[0]
    assert depth == 1, "the output stage fuses the final RMSNorm into the single layer"
    inv_freq = ROPE_THETA ** (-jnp.arange(0, HEAD_DIM, 2, dtype=F32) / HEAD_DIM)
    invf = jnp.broadcast_to(inv_freq[:, None], (HALF, _PROJ_TM))
    pos_f = positions.astype(F32)[:, None, :]
    e_mat = jnp.asarray(_aug_placement(), BF16)
    c_pad = jnp.zeros((8, D_MODEL), F32).at[:b].set(c)
    for l in range(depth):
        ada = _ada(c_pad, w_ada[l], b_ada[l][None, :])[:b]
        shift, scale, gate = [ada[:, None, k * D_MODEL:(k + 1) * D_MODEL] for k in range(3)]
        w_all, w_vt = _layout_w_in(w_in[l])
        bf_rep = jnp.tile(b_f[l].astype(F32), LANES // FOX_HEADS)[None, :]
        qa, ka, vat, sza, q_aug, k_aug, vbt, szb, sga, sgb = _proj(
            x, pos_f, scale, shift, g_norm[l][None, :], invf, bf_rep, w_all, w_vt, e_mat, tm=_PROJ_TM)
        gated_a = _swa(sinks[l].astype(F32), qa, ka, vat, sza, tq=_SWA_TQ)
        gated_b = _fox(q_aug, k_aug, vbt, szb, t=_FOX_T)
        x = _out(x, gated_a, gated_b, sga, sgb, gate,
                 w_o_swa[l].astype(BF16), w_o_fox[l].astype(BF16), w_out[l].astype(BF16),
                 g_final[None, :], tm=_OUT_TM)
    return x
```

```python
import functools
import math

import numpy as np
import jax
import jax.numpy as jnp
from jax import lax
from jax.experimental import pallas as pl
from jax.experimental.pallas import tpu as pltpu

D_MODEL = 1024
HEAD_DIM = 64
HALF = HEAD_DIM // 2
SWA_Q_HEADS = 8
SWA_KV_HEADS = 2
SWA_WIDTH = SWA_Q_HEADS * HEAD_DIM
SWA_KV_WIDTH = SWA_KV_HEADS * HEAD_DIM
FOX_HEADS = 8
FOX_WIDTH = FOX_HEADS * HEAD_DIM
WINDOW = 128
ROPE_THETA = 10000.0
NORM_EPS = 1e-6
QK_SCALE = HEAD_DIM ** -0.5
LOG2E = math.log2(math.e)

LANES = 128
V7X_VMEM_BYTES = 64 * 1024 * 1024

_PROJ_TM = 512
_SWA_TQ = 512
_FOX_T = 512
_OUT_TM = 1024

F32 = jnp.float32
BF16 = jnp.bfloat16

_QA, _KA, _ZA = 0, 1024, 1536
_QB, _KB, _FB, _ZB, _GA, _GB = 2048, 2560, 3072, 3200, 3712, 4736
_PROJ_COLS = 5760

_ONE_LANE = 24

_SWA_AHEAD = 2
_ONES_ROWS = 16


def _vmem_limit(nbytes):
    return int(min(nbytes, V7X_VMEM_BYTES - 8 * 1024 * 1024))


def _split3(v):
    hi = v.astype(BF16)
    r1 = v - hi.astype(F32)
    mid = r1.astype(BF16)
    lo = (r1 - mid.astype(F32)).astype(BF16)
    return hi, mid, lo


def _ada_kernel(c_ref, w_ref, b_ref, o_ref):
    o_ref[...] = jnp.dot(c_ref[...], w_ref[...], preferred_element_type=F32,
                         precision=lax.Precision.HIGHEST) + b_ref[...]


def _ada(c_pad, w_ada, b_ada):
    rows = c_pad.shape[0]
    n = w_ada.shape[1]
    nblk = n // D_MODEL
    return pl.pallas_call(
        _ada_kernel,
        grid=(nblk,),
        in_specs=[pl.BlockSpec((rows, D_MODEL), lambda j: (0, 0)),
                  pl.BlockSpec((D_MODEL, D_MODEL), lambda j: (0, j)),
                  pl.BlockSpec((1, D_MODEL), lambda j: (0, j))],
        out_specs=pl.BlockSpec((rows, D_MODEL), lambda j: (0, j)),
        out_shape=jax.ShapeDtypeStruct((rows, n), F32),
        name="ada_mod",
    )(c_pad, w_ada, b_ada)


def _sigmoid(z):
    return 0.5 * jnp.tanh(0.5 * z) + 0.5


def _proj_kernel(x_ref, pos_ref, scale_ref, shift_ref, gn_ref, invf_ref, bf_ref, w_ref, wvt_ref, e_ref,
                 qa_ref, ka_ref, vat_ref, sza_ref, qaug_ref, kaug_ref, vbt_ref, szb_ref,
                 sga_ref, sgb_ref, carry_ref, *, tm):
    i = pl.program_id(1)

    @pl.when(i == 0)
    def _():
        carry_ref[...] = jnp.zeros_like(carry_ref)

    x = x_ref[0]
    ms = jnp.mean(x * x, axis=-1, keepdims=True)
    y = x * lax.rsqrt(ms + NORM_EPS) * gn_ref[...]
    h = y * (1.0 + scale_ref[0]) + shift_ref[0]
    hb = h.astype(BF16)

    def proj(off, n):
        return jnp.dot(hb, w_ref[:, off:off + n], preferred_element_type=F32)

    lane = lax.broadcasted_iota(jnp.int32, (tm, LANES), 1)
    low = lane < HEAD_DIM


    zf = proj(_FB, LANES) + bf_ref[...]
    logf = jnp.minimum(zf, 0.0) - jnp.log1p(jnp.exp(-jnp.abs(zf)))
    grp = lane // FOX_HEADS
    hi, mid, lo = _split3(logf)
    zero = jnp.zeros((tm, LANES), F32)
    parts = jnp.where(grp == 0, hi.astype(F32),
                      jnp.where(grp == 1, mid.astype(F32),
                                jnp.where(grp == 2, lo.astype(F32), zero))).astype(BF16)
    tri = (lax.broadcasted_iota(jnp.int32, (tm, tm), 0)
           >= lax.broadcasted_iota(jnp.int32, (tm, tm), 1)).astype(BF16)
    sga_ref[0] = _sigmoid(proj(_GA, D_MODEL)).astype(BF16)
    rsum = jnp.dot(tri, parts, preferred_element_type=F32)
    c0 = rsum + pltpu.roll(rsum, LANES - FOX_HEADS, 1) + pltpu.roll(rsum, LANES - 2 * FOX_HEADS, 1)
    c0 = jnp.where(grp == 0, c0, zero)
    cl = c0 + pltpu.roll(c0, FOX_HEADS, 1) + pltpu.roll(c0, 2 * FOX_HEADS, 1)
    cum = cl + carry_ref[...]
    carry_ref[...] = cum[tm - 1:tm, :]

    hi, mid, lo = _split3(cum * LOG2E)
    one = jnp.ones((tm, LANES), F32)
    carrier = jnp.where(grp == 0, hi.astype(F32),
                        jnp.where(grp == 1, mid.astype(F32),
                                  jnp.where(grp == 2, lo.astype(F32),
                                            jnp.where(lane == _ONE_LANE, one, zero)))).astype(BF16)
    sgb_ref[0] = _sigmoid(proj(_GB, D_MODEL)).astype(BF16)
    aug = jnp.dot(carrier, e_ref[...], preferred_element_type=F32)

    z = proj(_ZA, SWA_WIDTH)
    sza_ref[0] = (z * _sigmoid(z)).astype(BF16)
    z = proj(_ZB, FOX_WIDTH)
    szb_ref[0] = (z * _sigmoid(z)).astype(BF16)

    ang = invf_ref[...] * pos_ref[0]
    cos = jnp.tile(jnp.cos(ang), (LANES // HALF, 1)).T
    sin = jnp.tile(jnp.sin(ang), (LANES // HALF, 1)).T
    r = proj(_QA, 2 * SWA_WIDTH)
    for c in range(SWA_WIDTH // LANES):
        a = c * LANES
        t = r[:, a:a + LANES] * cos + r[:, SWA_WIDTH + a:SWA_WIDTH + a + LANES] * sin
        qa_ref[0, :, a:a + LANES] = (t * (QK_SCALE * LOG2E)).astype(BF16)
    r = proj(_KA, 4 * SWA_KV_WIDTH)
    for c in range(2 * SWA_KV_WIDTH // LANES):
        a = c * LANES
        t = r[:, a:a + LANES] * cos + r[:, 2 * SWA_KV_WIDTH + a:2 * SWA_KV_WIDTH + a + LANES] * sin
        ka_ref[0, :, a:a + LANES] = t.astype(BF16)

    rq = proj(_QB, FOX_WIDTH)
    rk = proj(_KB, FOX_WIDTH)
    for hd in range(FOX_HEADS):
        a = (hd // 2) * LANES
        gq = aug[:, a:a + LANES]
        gk = aug[:, FOX_WIDTH + a:FOX_WIDTH + a + LANES]
        qv = rq[:, a:a + LANES] * (QK_SCALE * LOG2E)
        kv = rk[:, a:a + LANES]
        if hd % 2 == 0:
            qaug_ref[0, hd] = jnp.where(low, qv, gq).astype(BF16)
            kaug_ref[0, hd] = jnp.where(low, kv, gk).astype(BF16)
        else:
            qaug_ref[0, hd] = jnp.where(low, gq, qv).astype(BF16)
            kaug_ref[0, hd] = jnp.where(low, gk, kv).astype(BF16)

    vt = lax.dot_general(wvt_ref[...], hb, (((1,), (1,)), ((), ())), preferred_element_type=F32)
    vat_ref[0] = vt[0:SWA_KV_WIDTH].astype(BF16)
    vbt_ref[0] = vt[SWA_KV_WIDTH:].astype(BF16)


def _aug_placement():
    e = np.zeros((LANES, 2 * FOX_HEADS * HEAD_DIM), np.float32)
    for hd in range(FOX_HEADS):
        base = (hd // 2) * LANES + (HEAD_DIM if hd % 2 == 0 else 0)
        kbase = FOX_HEADS * HEAD_DIM + base
        for part in range(3):
            e[part * FOX_HEADS + hd, base + part] = 1.0
            e[_ONE_LANE, base + 3 + part] = 1.0
            e[_ONE_LANE, kbase + part] = 1.0
            e[part * FOX_HEADS + hd, kbase + 3 + part] = -1.0
    return e


def _proj(x, pos_f, scale, shift, gn, invf, bf_rep, w_all, w_vt, e_mat, *, tm):
    b, s, _ = x.shape
    grid = (b, s // tm)
    row = lambda bi, i: (bi, i, 0)
    per_b = lambda bi, i: (bi, 0, 0)
    const2 = lambda bi, i: (0, 0)
    once = pl.Buffered(1)
    in_specs = [
        pl.BlockSpec((1, tm, D_MODEL), row),
        pl.BlockSpec((1, 1, tm), lambda bi, i: (bi, 0, i)),
        pl.BlockSpec((1, 1, D_MODEL), per_b),
        pl.BlockSpec((1, 1, D_MODEL), per_b),
        pl.BlockSpec((1, D_MODEL), const2),
        pl.BlockSpec((HALF, tm), const2),
        pl.BlockSpec((1, LANES), const2),
        pl.BlockSpec((D_MODEL, _PROJ_COLS), const2, pipeline_mode=once),
        pl.BlockSpec((SWA_KV_WIDTH + FOX_WIDTH, D_MODEL), const2, pipeline_mode=once),
        pl.BlockSpec((LANES, 2 * FOX_WIDTH), const2, pipeline_mode=once),
    ]
    head4 = lambda bi, i: (bi, 0, i, 0)
    out_specs = [
        pl.BlockSpec((1, tm, SWA_WIDTH), row),
        pl.BlockSpec((1, tm, 2 * SWA_KV_WIDTH), row),
        pl.BlockSpec((1, SWA_KV_WIDTH, tm), lambda bi, i: (bi, 0, i)),
        pl.BlockSpec((1, tm, SWA_WIDTH), row),
        pl.BlockSpec((1, FOX_HEADS, tm, LANES), head4),
        pl.BlockSpec((1, FOX_HEADS, tm, LANES), head4),
        pl.BlockSpec((1, FOX_WIDTH, tm), lambda bi, i: (bi, 0, i)),
        pl.BlockSpec((1, tm, FOX_WIDTH), row),
        pl.BlockSpec((1, tm, D_MODEL), row),
        pl.BlockSpec((1, tm, D_MODEL), row),
    ]
    sds = jax.ShapeDtypeStruct
    out_shape = [
        sds((b, s, SWA_WIDTH), BF16), sds((b, s, 2 * SWA_KV_WIDTH), BF16),
        sds((b, SWA_KV_WIDTH, s), BF16), sds((b, s, SWA_WIDTH), BF16),
        sds((b, FOX_HEADS, s, LANES), BF16), sds((b, FOX_HEADS, s, LANES), BF16),
        sds((b, FOX_WIDTH, s), BF16), sds((b, s, FOX_WIDTH), BF16),
        sds((b, s, D_MODEL), BF16), sds((b, s, D_MODEL), BF16),
    ]
    return pl.pallas_call(
        functools.partial(_proj_kernel, tm=tm),
        grid=grid, in_specs=in_specs, out_specs=out_specs, out_shape=out_shape,
        scratch_shapes=[pltpu.VMEM((1, LANES), F32)],
        compiler_params=pltpu.CompilerParams(
            dimension_semantics=("arbitrary", "arbitrary"),
            vmem_limit_bytes=_vmem_limit(56 * 1024 * 1024)),
        name="in_proj",
    )(x, pos_f, scale, shift, gn, invf, bf_rep, w_all, w_vt, e_mat)


def _swa_kernel(sinks_ref, q_ref, kc_ref, kp_ref, vtc_ref, vtp_ref, bias_ref, sz_ref, o_ref,
                kband, vtband, *, tq):
    i = pl.program_id(1)
    nblk = tq // WINDOW
    nhq = SWA_Q_HEADS // SWA_KV_HEADS
    kband[0:WINDOW] = kp_ref[0]
    kband[WINDOW:] = kc_ref[0]
    vtband[:, 0:WINDOW] = vtp_ref[0]
    vtband[:, WINDOW:] = vtc_ref[0]

    cols = nhq * WINDOW
    first_plane = jnp.where(i > 0, 0, 1)
    headid = lax.broadcasted_iota(jnp.int32, (1, cols), 1) // WINDOW
    low = lax.broadcasted_iota(jnp.int32, (WINDOW, LANES), 1) < HEAD_DIM
    ones = jnp.ones((_ONES_ROWS, 2 * WINDOW), BF16)
    nt = (((1,), (1,)), ((), ()))

    def logits(r, g):
        kb = kband[r * WINDOW:(r + 2) * WINDOW, g * LANES:(g + 1) * LANES]
        qs = []
        for c in range(2):
            a = (2 * g + c) * LANES
            qc = q_ref[0, r * WINDOW:(r + 1) * WINDOW, a:a + LANES]
            qs.append(jnp.where(low, qc, jnp.zeros_like(qc)))
            qs.append(jnp.where(low, jnp.zeros_like(qc), qc))
        qst = jnp.concatenate(qs, axis=0)
        s = lax.dot_general(kb, qst, nt, preferred_element_type=F32)
        return s + bias_ref[first_plane if r == 0 else 0]

    def attend(r, g, s):
        sink = jnp.zeros((1, cols), F32)
        for k in range(nhq):
            sink = jnp.where(headid == k, sinks_ref[g * nhq + k] * LOG2E, sink)
        m = jnp.maximum(jnp.max(s, axis=0, keepdims=True), sink)
        p = jnp.exp2(s - m).astype(BF16)
        vt = vtband[g * HEAD_DIM:(g + 1) * HEAD_DIM, r * WINDOW:(r + 2) * WINDOW]
        pv = jnp.dot(jnp.concatenate([vt, ones], axis=0), p, preferred_element_type=F32)
        den = pv[HEAD_DIM:HEAD_DIM + 1] + jnp.exp2(sink - m)
        o_t = pv[0:HEAD_DIM] / den
        for c in range(2):
            a = (2 * g + c) * LANES
            pair = jnp.concatenate([o_t[:, 2 * c * WINDOW:(2 * c + 1) * WINDOW],
                                    o_t[:, (2 * c + 1) * WINDOW:(2 * c + 2) * WINDOW]], axis=0)
            gate = sz_ref[0, r * WINDOW:(r + 1) * WINDOW, a:a + LANES].astype(F32)
            o_ref[0, r * WINDOW:(r + 1) * WINDOW, a:a + LANES] = (pair.T * gate).astype(BF16)

    work = [(r, g) for r in range(nblk) for g in range(SWA_KV_HEADS)]
    pending = [logits(*w) for w in work[:_SWA_AHEAD]]
    for n, (r, g) in enumerate(work):
        if n + _SWA_AHEAD < len(work):
            pending.append(logits(*work[n + _SWA_AHEAD]))
        attend(r, g, pending.pop(0))


def _swa(sinks, qa, ka, vat, sza, *, tq):
    b, s, _ = qa.shape
    kvw = 2 * SWA_KV_WIDTH
    per = tq // WINDOW
    cur = lambda bi, i: (bi, i, 0)
    prev = lambda bi, i: (bi, jnp.maximum(i * per - 1, 0), 0)
    cur_t = lambda bi, i: (bi, 0, i)
    prev_t = lambda bi, i: (bi, 0, jnp.maximum(i * per - 1, 0))
    key = np.arange(2 * WINDOW)[:, None]
    qt = np.tile(np.arange(WINDOW), SWA_Q_HEADS // SWA_KV_HEADS)[None, :]
    band = np.where(key < WINDOW, key > qt, (key - WINDOW) <= qt)
    planes = np.stack([band, band & (key >= WINDOW)])
    bias = jnp.asarray(np.where(planes, 0.0, -np.inf).astype(np.float32))
    return pl.pallas_call(
        functools.partial(_swa_kernel, tq=tq),
        grid=(b, s // tq),
        in_specs=[pl.BlockSpec(memory_space=pltpu.SMEM),
                  pl.BlockSpec((1, tq, SWA_WIDTH), cur),
                  pl.BlockSpec((1, tq, kvw), cur),
                  pl.BlockSpec((1, WINDOW, kvw), prev),
                  pl.BlockSpec((1, SWA_KV_WIDTH, tq), cur_t),
                  pl.BlockSpec((1, SWA_KV_WIDTH, WINDOW), prev_t),
                  pl.BlockSpec(bias.shape, lambda bi, i: (0, 0, 0), pipeline_mode=pl.Buffered(1)),
                  pl.BlockSpec((1, tq, SWA_WIDTH), cur)],
        out_specs=pl.BlockSpec((1, tq, SWA_WIDTH), cur),
        out_shape=jax.ShapeDtypeStruct((b, s, SWA_WIDTH), BF16),
        scratch_shapes=[pltpu.VMEM((tq + WINDOW, kvw), BF16),
                        pltpu.VMEM((SWA_KV_WIDTH, tq + WINDOW), BF16)],
        compiler_params=pltpu.CompilerParams(dimension_semantics=("arbitrary", "arbitrary")),
        name="swa_attn",
    )(sinks, qa, ka, ka, vat, vat, bias, sza)


_ACC_ROWS = HEAD_DIM + _ONES_ROWS
_PIPE = 2
_FOX_UNROLL = 2


def _fox_schedule(n_rows):
    phases = [[(i, i, i) for i in range(n_rows)],
              [(i, j, i) for i in range(n_rows) for j in range(i)]]
    dummy = (0, 0, n_rows)
    tbl, spans = [], []
    for seq in phases:
        n_steps = len(seq) + _PIPE
        n_steps += -n_steps % _FOX_UNROLL
        spans.append((len(tbl), n_steps))
        tbl += [dummy] * _PIPE + seq + [dummy] * (n_steps - len(seq))
    return np.asarray(tbl, np.int32).T.copy(), spans


def _fox_kernel(tbl_ref, q_ref, k_ref, vt_ref, bias_ref, sz_ref, o_ref,
                s0, s1, mx0, mx1, p0, p1, a0, a1, m_ref, acc_ref, *, t, n_rows, spans):
    nt = (((1,), (1,)), ((), ()))
    g = t // 8
    @pl.when(jnp.logical_and(pl.program_id(0) == 0, pl.program_id(1) == 0))
    def _():
        for ref in (s0, s1, mx0, mx1, p0, p1, a0, a1):
            ref[...] = jnp.zeros_like(ref)
        m_ref[n_rows] = jnp.zeros_like(m_ref[n_rows])
        acc_ref[n_rows] = jnp.zeros_like(acc_ref[n_rows])

    def step(k, diagonal, s_w, mx_w, s_r, mx_r, p_w, a_w, p_r, a_r):
        qrow = pl.multiple_of(tbl_ref[0, k + 2] * t, t)
        krow = pl.multiple_of(tbl_ref[1, k + 2] * t, t)
        for hh in range(2):
            s = lax.dot_general(k_ref[0, hh, pl.ds(krow, t), :], q_ref[0, hh, pl.ds(qrow, t), :], nt,
                                preferred_element_type=F32)
            if diagonal:
                s = s + bias_ref[...]
            s_w[hh] = s
            mx_w[hh] = jnp.max(s.reshape(g, 8, t), axis=0)

        srow = tbl_ref[2, k + 1]
        for hh in range(2):
            m_tile = jnp.max(mx_r[hh], axis=0, keepdims=True)
            if diagonal:
                m_new = jnp.broadcast_to(m_tile, (8, t))
            else:
                m_old = m_ref[srow, hh]
                m_new = jnp.maximum(m_old, m_tile)
                a_w[hh] = jnp.exp2(m_old - m_new)
            m_ref[srow, hh] = m_new
            p = jnp.exp2(s_r[hh].reshape(g, 8, t) - m_new[None])
            p_w[hh] = p.reshape(t, t).astype(BF16)

        vcol = pl.multiple_of(tbl_ref[1, k] * t, t)
        arow = tbl_ref[2, k]
        for hh in range(2):
            vt = vt_ref[0, hh * HEAD_DIM:(hh + 1) * HEAD_DIM, pl.ds(vcol, t)]
            lhs = jnp.concatenate([vt, jnp.ones((_ONES_ROWS, t), BF16)], axis=0)
            part = jnp.dot(lhs, p_r[hh], preferred_element_type=F32)
            if diagonal:
                acc_ref[arow, hh] = part
            else:
                old = acc_ref[arow, hh].reshape(_ACC_ROWS // 8, 8, t)
                acc_ref[arow, hh] = (a_r[hh][None] * old).reshape(_ACC_ROWS, t) + part

    assert _FOX_UNROLL % 2 == 0
    for diagonal, (first, n_steps) in zip((True, False), spans):
        def body(kk, carry, diagonal=diagonal, first=first):
            k = first + _FOX_UNROLL * kk
            for u in range(0, _FOX_UNROLL, 2):
                step(k + u, diagonal, s0, mx0, s1, mx1, p1, a1, p0, a0)
                step(k + u + 1, diagonal, s1, mx1, s0, mx0, p0, a0, p1, a1)
            return carry

        lax.fori_loop(0, n_steps // _FOX_UNROLL, body, 0)

    def finish_row(i, carry):
        halves = []
        for hh in range(2):
            a = acc_ref[i, hh]
            halves.append(a[0:HEAD_DIM] / a[HEAD_DIM:HEAD_DIM + 1])
        o_t = jnp.concatenate(halves, axis=0)
        rows = pl.ds(pl.multiple_of(i * t, t), t)
        o_ref[0, rows, :] = (o_t.T * sz_ref[0, rows, :].astype(F32)).astype(BF16)
        return carry

    lax.fori_loop(0, n_rows, finish_row, 0)


def _fox(q_aug, k_aug, vbt, szb, *, t):
    b, _, s, _ = q_aug.shape
    npair = FOX_HEADS // 2
    n_rows = s // t
    tbl, spans = _fox_schedule(n_rows)
    tri = np.triu(np.ones((t, t), bool))
    bias = jnp.asarray(np.where(tri, 0.0, -np.inf).astype(np.float32))
    pair = lambda bi, p: (bi, p, 0, 0)
    stat = pltpu.VMEM((2, 8, t), F32)
    return pl.pallas_call(
        functools.partial(_fox_kernel, t=t, n_rows=n_rows, spans=tuple(spans)),
        grid=(b, npair),
        in_specs=[pl.BlockSpec(memory_space=pltpu.SMEM),
                  pl.BlockSpec((1, 2, s, LANES), pair),
                  pl.BlockSpec((1, 2, s, LANES), pair),
                  pl.BlockSpec((1, 2 * HEAD_DIM, s), lambda bi, p: (bi, p, 0)),
                  pl.BlockSpec((t, t), lambda bi, p: (0, 0), pipeline_mode=pl.Buffered(1)),
                  pl.BlockSpec((1, s, LANES), lambda bi, p: (bi, 0, p))],
        out_specs=pl.BlockSpec((1, s, LANES), lambda bi, p: (bi, 0, p)),
        out_shape=jax.ShapeDtypeStruct((b, s, FOX_WIDTH), BF16),
        scratch_shapes=[pltpu.VMEM((2, t, t), F32), pltpu.VMEM((2, t, t), F32),
                        stat, stat,
                        pltpu.VMEM((2, t, t), BF16), pltpu.VMEM((2, t, t), BF16),
                        stat, stat,
                        pltpu.VMEM((n_rows + 1, 2, 8, t), F32),
                        pltpu.VMEM((n_rows + 1, 2, _ACC_ROWS, t), F32)],
        compiler_params=pltpu.CompilerParams(
            dimension_semantics=("arbitrary", "arbitrary"),
            vmem_limit_bytes=_vmem_limit(56 * 1024 * 1024)),
        name="fox_attn",
    )(jnp.asarray(tbl), q_aug, k_aug, vbt, bias, szb)


def _out_kernel(x_ref, ga_ref, gb_ref, sga_ref, sgb_ref, gate_ref,
                woa_ref, wob_ref, wout_ref, gf_ref, o_ref):
    ya = jnp.dot(ga_ref[0], woa_ref[...], preferred_element_type=F32)
    yb = jnp.dot(gb_ref[0], wob_ref[...], preferred_element_type=F32)
    merged = sga_ref[0].astype(F32) * ya + sgb_ref[0].astype(F32) * yb
    d = jnp.dot(merged.astype(BF16), wout_ref[...], preferred_element_type=F32)
    xo = x_ref[0] + gate_ref[0] * d
    ms = jnp.mean(xo * xo, axis=-1, keepdims=True)
    o_ref[0] = xo * lax.rsqrt(ms + NORM_EPS) * gf_ref[...]


def _out(x, gated_a, gated_b, sga, sgb, gate, woa, wob, wout, gf, *, tm):
    b, s, _ = x.shape
    row = lambda bi, i: (bi, i, 0)
    per_b = lambda bi, i: (bi, 0, 0)
    const2 = lambda bi, i: (0, 0)
    once = pl.Buffered(1)
    return pl.pallas_call(
        _out_kernel,
        grid=(b, s // tm),
        in_specs=[pl.BlockSpec((1, tm, D_MODEL), row),
                  pl.BlockSpec((1, tm, SWA_WIDTH), row), pl.BlockSpec((1, tm, FOX_WIDTH), row),
                  pl.BlockSpec((1, tm, D_MODEL), row), pl.BlockSpec((1, tm, D_MODEL), row),
                  pl.BlockSpec((1, 1, D_MODEL), per_b),
                  pl.BlockSpec((SWA_WIDTH, D_MODEL), const2, pipeline_mode=once),
                  pl.BlockSpec((FOX_WIDTH, D_MODEL), const2, pipeline_mode=once),
                  pl.BlockSpec((D_MODEL, D_MODEL), const2, pipeline_mode=once),
                  pl.BlockSpec((1, D_MODEL), const2)],
        out_specs=pl.BlockSpec((1, tm, D_MODEL), row),
        out_shape=jax.ShapeDtypeStruct((b, s, D_MODEL), F32),
        compiler_params=pltpu.CompilerParams(
            dimension_semantics=("arbitrary", "arbitrary"),
            vmem_limit_bytes=_vmem_limit(48 * 1024 * 1024)),
        name="out_proj",
    )(x, gated_a, gated_b, sga, sgb, gate, woa, wob, wout, gf)


def _rot_cols(w, nheads):
    w3 = w.reshape(w.shape[0], nheads, HEAD_DIM)
    return jnp.concatenate([-w3[..., HALF:], w3[..., :HALF]], axis=-1).reshape(w.shape)


def _dup_heads(w, nheads):
    w3 = w.reshape(w.shape[0], nheads, 1, HEAD_DIM)
    return jnp.broadcast_to(w3, (w.shape[0], nheads, 2, HEAD_DIM)).reshape(w.shape[0], 2 * nheads * HEAD_DIM)


def _layout_w_in(w):
    o = np.cumsum([0, SWA_WIDTH, SWA_KV_WIDTH, SWA_KV_WIDTH, SWA_WIDTH, FOX_WIDTH, FOX_WIDTH, FOX_WIDTH,
                   FOX_HEADS, FOX_WIDTH, D_MODEL, D_MODEL]).tolist()
    qa, ka, va, za, qb, kb, vb, fb, zb, ga, gb = [w[:, o[k]:o[k + 1]] for k in range(11)]
    cols = [qa, _rot_cols(qa, SWA_Q_HEADS),
            _dup_heads(ka, SWA_KV_HEADS), _dup_heads(_rot_cols(ka, SWA_KV_HEADS), SWA_KV_HEADS),
            za, qb, kb, jnp.tile(fb, (1, LANES // FOX_HEADS)), zb, ga, gb]
    w_all = jnp.concatenate(cols, axis=1).astype(BF16)
    assert w_all.shape[1] == _PROJ_COLS
    return w_all, jnp.concatenate([va, vb], axis=1).T.astype(BF16)


def kernel(x, c, positions, w_ada, b_ada, g_norm, w_in, b_f, sinks, w_o_swa, w_o_fox, w_out, g_final):
    b, s, _ = x.shape
    depth = w_in.shape[0]
    assert depth == 1, "the output stage fuses the final RMSNorm into the single layer"
    inv_freq = ROPE_THETA ** (-jnp.arange(0, HEAD_DIM, 2, dtype=F32) / HEAD_DIM)
    invf = jnp.broadcast_to(inv_freq[:, None], (HALF, _PROJ_TM))
    pos_f = positions.astype(F32)[:, None, :]
    e_mat = jnp.asarray(_aug_placement(), BF16)
    c_pad = jnp.zeros((8, D_MODEL), F32).at[:b].set(c)
    for l in range(depth):
        ada = _ada(c_pad, w_ada[l], b_ada[l][None, :])[:b]
        shift, scale, gate = [ada[:, None, k * D_MODEL:(k + 1) * D_MODEL] for k in range(3)]
        w_all, w_vt = _layout_w_in(w_in[l])
        bf_rep = jnp.tile(b_f[l].astype(F32), LANES // FOX_HEADS)[None, :]
        qa, ka, vat, sza, q_aug, k_aug, vbt, szb, sga, sgb = _proj(
            x, pos_f, scale, shift, g_norm[l][None, :], invf, bf_rep, w_all, w_vt, e_mat, tm=_PROJ_TM)
        gated_a = _swa(sinks[l].astype(F32), qa, ka, vat, sza, tq=_SWA_TQ)
        gated_b = _fox(q_aug, k_aug, vbt, szb, t=_FOX_T)
        x = _out(x, gated_a, gated_b, sga, sgb, gate,
                 w_o_swa[l].astype(BF16), w_o_fox[l].astype(BF16), w_out[l].astype(BF16),
                 g_final[None, :], tm=_OUT_TM)
    return x
```

```python
import functools
import math

import numpy as np
import jax
import jax.numpy as jnp
from jax import lax
from jax.experimental import pallas as pl
from jax.experimental.pallas import tpu as pltpu

D_MODEL = 1024
HEAD_DIM = 64
HALF = HEAD_DIM // 2
SWA_Q_HEADS = 8
SWA_KV_HEADS = 2
SWA_WIDTH = SWA_Q_HEADS * HEAD_DIM
SWA_KV_WIDTH = SWA_KV_HEADS * HEAD_DIM
FOX_HEADS = 8
FOX_WIDTH = FOX_HEADS * HEAD_DIM
WINDOW = 128
ROPE_THETA = 10000.0
NORM_EPS = 1e-6
QK_SCALE = HEAD_DIM ** -0.5
LOG2E = math.log2(math.e)

LANES = 128
V7X_VMEM_BYTES = 64 * 1024 * 1024

_PROJ_TM = 512
_SWA_TQ = 1024
_FOX_T = 512
_OUT_TM = 1024

F32 = jnp.float32
BF16 = jnp.bfloat16

_QA, _KA, _ZA = 0, 1024, 1536
_QB, _KB, _FB, _ZB, _GA, _GB = 2048, 2560, 3072, 3200, 3712, 4736
_PROJ_COLS = 5760

_ONE_LANE = 24

_SWA_AHEAD = 2
_ONES_ROWS = 16


def _vmem_limit(nbytes):
    return int(min(nbytes, V7X_VMEM_BYTES - 8 * 1024 * 1024))


def _split3(v):
    hi = v.astype(BF16)
    r1 = v - hi.astype(F32)
    mid = r1.astype(BF16)
    lo = (r1 - mid.astype(F32)).astype(BF16)
    return hi, mid, lo


def _ada_kernel(c_ref, w_ref, b_ref, o_ref):
    o_ref[...] = jnp.dot(c_ref[...], w_ref[...], preferred_element_type=F32,
                         precision=lax.Precision.HIGHEST) + b_ref[...]


def _ada(c_pad, w_ada, b_ada):
    rows = c_pad.shape[0]
    n = w_ada.shape[1]
    nblk = n // D_MODEL
    return pl.pallas_call(
        _ada_kernel,
        grid=(nblk,),
        in_specs=[pl.BlockSpec((rows, D_MODEL), lambda j: (0, 0)),
                  pl.BlockSpec((D_MODEL, D_MODEL), lambda j: (0, j)),
                  pl.BlockSpec((1, D_MODEL), lambda j: (0, j))],
        out_specs=pl.BlockSpec((rows, D_MODEL), lambda j: (0, j)),
        out_shape=jax.ShapeDtypeStruct((rows, n), F32),
        name="ada_mod",
    )(c_pad, w_ada, b_ada)


def _sigmoid(z):
    return 0.5 * jnp.tanh(0.5 * z) + 0.5


def _proj_kernel(x_ref, pos_ref, scale_ref, shift_ref, gn_ref, invf_ref, bf_ref, w_ref, wvt_ref, e_ref,
                 qa_ref, ka_ref, vat_ref, sza_ref, qaug_ref, kaug_ref, vbt_ref, szb_ref,
                 sga_ref, sgb_ref, carry_ref, *, tm):
    i = pl.program_id(1)

    @pl.when(i == 0)
    def _():
        carry_ref[...] = jnp.zeros_like(carry_ref)

    x = x_ref[0]
    ms = jnp.mean(x * x, axis=-1, keepdims=True)
    y = x * lax.rsqrt(ms + NORM_EPS) * gn_ref[...]
    h = y * (1.0 + scale_ref[0]) + shift_ref[0]
    hb = h.astype(BF16)

    def proj(off, n):
        return jnp.dot(hb, w_ref[:, off:off + n], preferred_element_type=F32)

    lane = lax.broadcasted_iota(jnp.int32, (tm, LANES), 1)
    low = lane < HEAD_DIM


    zf = proj(_FB, LANES) + bf_ref[...]
    logf = jnp.minimum(zf, 0.0) - jnp.log1p(jnp.exp(-jnp.abs(zf)))
    grp = lane // FOX_HEADS
    hi, mid, lo = _split3(logf)
    zero = jnp.zeros((tm, LANES), F32)
    parts = jnp.where(grp == 0, hi.astype(F32),
                      jnp.where(grp == 1, mid.astype(F32),
                                jnp.where(grp == 2, lo.astype(F32), zero))).astype(BF16)
    tri = (lax.broadcasted_iota(jnp.int32, (tm, tm), 0)
           >= lax.broadcasted_iota(jnp.int32, (tm, tm), 1)).astype(BF16)
    sga_ref[0] = _sigmoid(proj(_GA, D_MODEL)).astype(BF16)
    rsum = jnp.dot(tri, parts, preferred_element_type=F32)
    c0 = rsum + pltpu.roll(rsum, LANES - FOX_HEADS, 1) + pltpu.roll(rsum, LANES - 2 * FOX_HEADS, 1)
    c0 = jnp.where(grp == 0, c0, zero)
    cl = c0 + pltpu.roll(c0, FOX_HEADS, 1) + pltpu.roll(c0, 2 * FOX_HEADS, 1)
    cum = cl + carry_ref[...]
    carry_ref[...] = cum[tm - 1:tm, :]

    hi, mid, lo = _split3(cum * LOG2E)
    one = jnp.ones((tm, LANES), F32)
    carrier = jnp.where(grp == 0, hi.astype(F32),
                        jnp.where(grp == 1, mid.astype(F32),
                                  jnp.where(grp == 2, lo.astype(F32),
                                            jnp.where(lane == _ONE_LANE, one, zero)))).astype(BF16)
    sgb_ref[0] = _sigmoid(proj(_GB, D_MODEL)).astype(BF16)
    aug = jnp.dot(carrier, e_ref[...], preferred_element_type=F32)

    z = proj(_ZA, SWA_WIDTH)
    sza_ref[0] = (z * _sigmoid(z)).astype(BF16)
    z = proj(_ZB, FOX_WIDTH)
    szb_ref[0] = (z * _sigmoid(z)).astype(BF16)

    ang = invf_ref[...] * pos_ref[0]
    cos = jnp.tile(jnp.cos(ang), (LANES // HALF, 1)).T
    sin = jnp.tile(jnp.sin(ang), (LANES // HALF, 1)).T
    r = proj(_QA, 2 * SWA_WIDTH)
    for c in range(SWA_WIDTH // LANES):
        a = c * LANES
        t = r[:, a:a + LANES] * cos + r[:, SWA_WIDTH + a:SWA_WIDTH + a + LANES] * sin
        qa_ref[0, :, a:a + LANES] = (t * (QK_SCALE * LOG2E)).astype(BF16)
    r = proj(_KA, 4 * SWA_KV_WIDTH)
    for c in range(2 * SWA_KV_WIDTH // LANES):
        a = c * LANES
        t = r[:, a:a + LANES] * cos + r[:, 2 * SWA_KV_WIDTH + a:2 * SWA_KV_WIDTH + a + LANES] * sin
        ka_ref[0, :, a:a + LANES] = t.astype(BF16)

    rq = proj(_QB, FOX_WIDTH)
    rk = proj(_KB, FOX_WIDTH)
    for hd in range(FOX_HEADS):
        a = (hd // 2) * LANES
        gq = aug[:, a:a + LANES]
        gk = aug[:, FOX_WIDTH + a:FOX_WIDTH + a + LANES]
        qv = rq[:, a:a + LANES] * (QK_SCALE * LOG2E)
        kv = rk[:, a:a + LANES]
        if hd % 2 == 0:
            qaug_ref[0, hd] = jnp.where(low, qv, gq).astype(BF16)
            kaug_ref[0, hd] = jnp.where(low, kv, gk).astype(BF16)
        else:
            qaug_ref[0, hd] = jnp.where(low, gq, qv).astype(BF16)
            kaug_ref[0, hd] = jnp.where(low, gk, kv).astype(BF16)

    vt = lax.dot_general(wvt_ref[...], hb, (((1,), (1,)), ((), ())), preferred_element_type=F32)
    vat_ref[0] = vt[0:SWA_KV_WIDTH].astype(BF16)
    vbt_ref[0] = vt[SWA_KV_WIDTH:].astype(BF16)


def _aug_placement():
    e = np.zeros((LANES, 2 * FOX_HEADS * HEAD_DIM), np.float32)
    for hd in range(FOX_HEADS):
        base = (hd // 2) * LANES + (HEAD_DIM if hd % 2 == 0 else 0)
        kbase = FOX_HEADS * HEAD_DIM + base
        for part in range(3):
            e[part * FOX_HEADS + hd, base + part] = 1.0
            e[_ONE_LANE, base + 3 + part] = 1.0
            e[_ONE_LANE, kbase + part] = 1.0
            e[part * FOX_HEADS + hd, kbase + 3 + part] = -1.0
    return e


def _proj(x, pos_f, scale, shift, gn, invf, bf_rep, w_all, w_vt, e_mat, *, tm):
    b, s, _ = x.shape
    grid = (b, s // tm)
    row = lambda bi, i: (bi, i, 0)
    per_b = lambda bi, i: (bi, 0, 0)
    const2 = lambda bi, i: (0, 0)
    once = pl.Buffered(1)
    in_specs = [
        pl.BlockSpec((1, tm, D_MODEL), row),
        pl.BlockSpec((1, 1, tm), lambda bi, i: (bi, 0, i)),
        pl.BlockSpec((1, 1, D_MODEL), per_b),
        pl.BlockSpec((1, 1, D_MODEL), per_b),
        pl.BlockSpec((1, D_MODEL), const2),
        pl.BlockSpec((HALF, tm), const2),
        pl.BlockSpec((1, LANES), const2),
        pl.BlockSpec((D_MODEL, _PROJ_COLS), const2, pipeline_mode=once),
        pl.BlockSpec((SWA_KV_WIDTH + FOX_WIDTH, D_MODEL), const2, pipeline_mode=once),
        pl.BlockSpec((LANES, 2 * FOX_WIDTH), const2, pipeline_mode=once),
    ]
    head4 = lambda bi, i: (bi, 0, i, 0)
    out_specs = [
        pl.BlockSpec((1, tm, SWA_WIDTH), row),
        pl.BlockSpec((1, tm, 2 * SWA_KV_WIDTH), row),
        pl.BlockSpec((1, SWA_KV_WIDTH, tm), lambda bi, i: (bi, 0, i)),
        pl.BlockSpec((1, tm, SWA_WIDTH), row),
        pl.BlockSpec((1, FOX_HEADS, tm, LANES), head4),
        pl.BlockSpec((1, FOX_HEADS, tm, LANES), head4),
        pl.BlockSpec((1, FOX_WIDTH, tm), lambda bi, i: (bi, 0, i)),
        pl.BlockSpec((1, tm, FOX_WIDTH), row),
        pl.BlockSpec((1, tm, D_MODEL), row),
        pl.BlockSpec((1, tm, D_MODEL), row),
    ]
    sds = jax.ShapeDtypeStruct
    out_shape = [
        sds((b, s, SWA_WIDTH), BF16), sds((b, s, 2 * SWA_KV_WIDTH), BF16),
        sds((b, SWA_KV_WIDTH, s), BF16), sds((b, s, SWA_WIDTH), BF16),
        sds((b, FOX_HEADS, s, LANES), BF16), sds((b, FOX_HEADS, s, LANES), BF16),
        sds((b, FOX_WIDTH, s), BF16), sds((b, s, FOX_WIDTH), BF16),
        sds((b, s, D_MODEL), BF16), sds((b, s, D_MODEL), BF16),
    ]
    return pl.pallas_call(
        functools.partial(_proj_kernel, tm=tm),
        grid=grid, in_specs=in_specs, out_specs=out_specs, out_shape=out_shape,
        scratch_shapes=[pltpu.VMEM((1, LANES), F32)],
        compiler_params=pltpu.CompilerParams(
            dimension_semantics=("arbitrary", "arbitrary"),
            vmem_limit_bytes=_vmem_limit(56 * 1024 * 1024)),
        name="in_proj",
    )(x, pos_f, scale, shift, gn, invf, bf_rep, w_all, w_vt, e_mat)


def _swa_kernel(sinks_ref, q_ref, kc_ref, kp_ref, vtc_ref, vtp_ref, bias_ref, sz_ref, o_ref,
                kband, vtband, *, tq):
    i = pl.program_id(1)
    nblk = tq // WINDOW
    nhq = SWA_Q_HEADS // SWA_KV_HEADS
    kband[0:WINDOW] = kp_ref[0]
    kband[WINDOW:] = kc_ref[0]
    vtband[:, 0:WINDOW] = vtp_ref[0]
    vtband[:, WINDOW:] = vtc_ref[0]

    cols = nhq * WINDOW
    first_plane = jnp.where(i > 0, 0, 1)
    headid = lax.broadcasted_iota(jnp.int32, (1, cols), 1) // WINDOW
    low = lax.broadcasted_iota(jnp.int32, (WINDOW, LANES), 1) < HEAD_DIM
    ones = jnp.ones((_ONES_ROWS, 2 * WINDOW), BF16)
    nt = (((1,), (1,)), ((), ()))

    def logits(r, g):
        kb = kband[r * WINDOW:(r + 2) * WINDOW, g * LANES:(g + 1) * LANES]
        qs = []
        for c in range(2):
            a = (2 * g + c) * LANES
            qc = q_ref[0, r * WINDOW:(r + 1) * WINDOW, a:a + LANES]
            qs.append(jnp.where(low, qc, jnp.zeros_like(qc)))
            qs.append(jnp.where(low, jnp.zeros_like(qc), qc))
        qst = jnp.concatenate(qs, axis=0)
        s = lax.dot_general(kb, qst, nt, preferred_element_type=F32)
        return s + bias_ref[first_plane if r == 0 else 0]

    def attend(r, g, s):
        sink = jnp.zeros((1, cols), F32)
        for k in range(nhq):
            sink = jnp.where(headid == k, sinks_ref[g * nhq + k] * LOG2E, sink)
        m = jnp.maximum(jnp.max(s, axis=0, keepdims=True), sink)
        p = jnp.exp2(s - m).astype(BF16)
        vt = vtband[g * HEAD_DIM:(g + 1) * HEAD_DIM, r * WINDOW:(r + 2) * WINDOW]
        pv = jnp.dot(jnp.concatenate([vt, ones], axis=0), p, preferred_element_type=F32)
        den = pv[HEAD_DIM:HEAD_DIM + 1] + jnp.exp2(sink - m)
        o_t = pv[0:HEAD_DIM] / den
        for c in range(2):
            a = (2 * g + c) * LANES
            pair = jnp.concatenate([o_t[:, 2 * c * WINDOW:(2 * c + 1) * WINDOW],
                                    o_t[:, (2 * c + 1) * WINDOW:(2 * c + 2) * WINDOW]], axis=0)
            gate = sz_ref[0, r * WINDOW:(r + 1) * WINDOW, a:a + LANES].astype(F32)
            o_ref[0, r * WINDOW:(r + 1) * WINDOW, a:a + LANES] = (pair.T * gate).astype(BF16)

    work = [(r, g) for r in range(nblk) for g in range(SWA_KV_HEADS)]
    pending = [logits(*w) for w in work[:_SWA_AHEAD]]
    for n, (r, g) in enumerate(work):
        if n + _SWA_AHEAD < len(work):
            pending.append(logits(*work[n + _SWA_AHEAD]))
        attend(r, g, pending.pop(0))


def _swa(sinks, qa, ka, vat, sza, *, tq):
    b, s, _ = qa.shape
    kvw = 2 * SWA_KV_WIDTH
    per = tq // WINDOW
    cur = lambda bi, i: (bi, i, 0)
    prev = lambda bi, i: (bi, jnp.maximum(i * per - 1, 0), 0)
    cur_t = lambda bi, i: (bi, 0, i)
    prev_t = lambda bi, i: (bi, 0, jnp.maximum(i * per - 1, 0))
    key = np.arange(2 * WINDOW)[:, None]
    qt = np.tile(np.arange(WINDOW), SWA_Q_HEADS // SWA_KV_HEADS)[None, :]
    band = np.where(key < WINDOW, key > qt, (key - WINDOW) <= qt)
    planes = np.stack([band, band & (key >= WINDOW)])
    bias = jnp.asarray(np.where(planes, 0.0, -np.inf).astype(np.float32))
    return pl.pallas_call(
        functools.partial(_swa_kernel, tq=tq),
        grid=(b, s // tq),
        in_specs=[pl.BlockSpec(memory_space=pltpu.SMEM),
                  pl.BlockSpec((1, tq, SWA_WIDTH), cur),
                  pl.BlockSpec((1, tq, kvw), cur),
                  pl.BlockSpec((1, WINDOW, kvw), prev),
                  pl.BlockSpec((1, SWA_KV_WIDTH, tq), cur_t),
                  pl.BlockSpec((1, SWA_KV_WIDTH, WINDOW), prev_t),
                  pl.BlockSpec(bias.shape, lambda bi, i: (0, 0, 0), pipeline_mode=pl.Buffered(1)),
                  pl.BlockSpec((1, tq, SWA_WIDTH), cur)],
        out_specs=pl.BlockSpec((1, tq, SWA_WIDTH), cur),
        out_shape=jax.ShapeDtypeStruct((b, s, SWA_WIDTH), BF16),
        scratch_shapes=[pltpu.VMEM((tq + WINDOW, kvw), BF16),
                        pltpu.VMEM((SWA_KV_WIDTH, tq + WINDOW), BF16)],
        compiler_params=pltpu.CompilerParams(dimension_semantics=("arbitrary", "arbitrary")),
        name="swa_attn",
    )(sinks, qa, ka, ka, vat, vat, bias, sza)


_ACC_ROWS = HEAD_DIM + _ONES_ROWS
_PIPE = 2
_FOX_UNROLL = 2


def _fox_schedule(n_rows):
    phases = [[(i, i, i) for i in range(n_rows)],
              [(i, j, i) for i in range(n_rows) for j in range(i)]]
    dummy = (0, 0, n_rows)
    tbl, spans = [], []
    for seq in phases:
        n_steps = len(seq) + _PIPE
        n_steps += -n_steps % _FOX_UNROLL
        spans.append((len(tbl), n_steps))
        tbl += [dummy] * _PIPE + seq + [dummy] * (n_steps - len(seq))
    return np.asarray(tbl, np.int32).T.copy(), spans


def _fox_kernel(tbl_ref, q_ref, k_ref, vt_ref, bias_ref, sz_ref, o_ref,
                s0, s1, mx0, mx1, p0, p1, a0, a1, m_ref, acc_ref, *, t, n_rows, spans):
    nt = (((1,), (1,)), ((), ()))
    g = t // 8
    @pl.when(jnp.logical_and(pl.program_id(0) == 0, pl.program_id(1) == 0))
    def _():
        for ref in (s0, s1, mx0, mx1, p0, p1, a0, a1):
            ref[...] = jnp.zeros_like(ref)
        m_ref[n_rows] = jnp.zeros_like(m_ref[n_rows])
        acc_ref[n_rows] = jnp.zeros_like(acc_ref[n_rows])

    def step(k, diagonal, s_w, mx_w, s_r, mx_r, p_w, a_w, p_r, a_r):
        qrow = pl.multiple_of(tbl_ref[0, k + 2] * t, t)
        krow = pl.multiple_of(tbl_ref[1, k + 2] * t, t)
        for hh in range(2):
            s = lax.dot_general(k_ref[0, hh, pl.ds(krow, t), :], q_ref[0, hh, pl.ds(qrow, t), :], nt,
                                preferred_element_type=F32)
            if diagonal:
                s = s + bias_ref[...]
            s_w[hh] = s
            mx_w[hh] = jnp.max(s.reshape(g, 8, t), axis=0)

        srow = tbl_ref[2, k + 1]
        for hh in range(2):
            m_tile = jnp.max(mx_r[hh], axis=0, keepdims=True)
            if diagonal:
                m_new = jnp.broadcast_to(m_tile, (8, t))
            else:
                m_old = m_ref[srow, hh]
                m_new = jnp.maximum(m_old, m_tile)
                a_w[hh] = jnp.exp2(m_old - m_new)
            m_ref[srow, hh] = m_new
            p = jnp.exp2(s_r[hh].reshape(g, 8, t) - m_new[None])
            p_w[hh] = p.reshape(t, t).astype(BF16)

        vcol = pl.multiple_of(tbl_ref[1, k] * t, t)
        arow = tbl_ref[2, k]
        for hh in range(2):
            vt = vt_ref[0, hh * HEAD_DIM:(hh + 1) * HEAD_DIM, pl.ds(vcol, t)]
            lhs = jnp.concatenate([vt, jnp.ones((_ONES_ROWS, t), BF16)], axis=0)
            part = jnp.dot(lhs, p_r[hh], preferred_element_type=F32)
            if diagonal:
                acc_ref[arow, hh] = part
            else:
                old = acc_ref[arow, hh].reshape(_ACC_ROWS // 8, 8, t)
                acc_ref[arow, hh] = (a_r[hh][None] * old).reshape(_ACC_ROWS, t) + part

    assert _FOX_UNROLL % 2 == 0
    for diagonal, (first, n_steps) in zip((True, False), spans):
        def body(kk, carry, diagonal=diagonal, first=first):
            k = first + _FOX_UNROLL * kk
            for u in range(0, _FOX_UNROLL, 2):
                step(k + u, diagonal, s0, mx0, s1, mx1, p1, a1, p0, a0)
                step(k + u + 1, diagonal, s1, mx1, s0, mx0, p0, a0, p1, a1)
            return carry

        lax.fori_loop(0, n_steps // _FOX_UNROLL, body, 0)

    def finish_row(i, carry):
        halves = []
        for hh in range(2):
            a = acc_ref[i, hh]
            halves.append(a[0:HEAD_DIM] / a[HEAD_DIM:HEAD_DIM + 1])
        o_t = jnp.concatenate(halves, axis=0)
        rows = pl.ds(pl.multiple_of(i * t, t), t)
        o_ref[0, rows, :] = (o_t.T * sz_ref[0, rows, :].astype(F32)).astype(BF16)
        return carry

    lax.fori_loop(0, n_rows, finish_row, 0)


def _fox(q_aug, k_aug, vbt, szb, *, t):
    b, _, s, _ = q_aug.shape
    npair = FOX_HEADS // 2
    n_rows = s // t
    tbl, spans = _fox_schedule(n_rows)
    tri = np.triu(np.ones((t, t), bool))
    bias = jnp.asarray(np.where(tri, 0.0, -np.inf).astype(np.float32))
    pair = lambda bi, p: (bi, p, 0, 0)
    stat = pltpu.VMEM((2, 8, t), F32)
    return pl.pallas_call(
        functools.partial(_fox_kernel, t=t, n_rows=n_rows, spans=tuple(spans)),
        grid=(b, npair),
        in_specs=[pl.BlockSpec(memory_space=pltpu.SMEM),
                  pl.BlockSpec((1, 2, s, LANES), pair),
                  pl.BlockSpec((1, 2, s, LANES), pair),
                  pl.BlockSpec((1, 2 * HEAD_DIM, s), lambda bi, p: (bi, p, 0)),
                  pl.BlockSpec((t, t), lambda bi, p: (0, 0), pipeline_mode=pl.Buffered(1)),
                  pl.BlockSpec((1, s, LANES), lambda bi, p: (bi, 0, p))],
        out_specs=pl.BlockSpec((1, s, LANES), lambda bi, p: (bi, 0, p)),
        out_shape=jax.ShapeDtypeStruct((b, s, FOX_WIDTH), BF16),
        scratch_shapes=[pltpu.VMEM((2, t, t), F32), pltpu.VMEM((2, t, t), F32),
                        stat, stat,
                        pltpu.VMEM((2, t, t), BF16), pltpu.VMEM((2, t, t), BF16),
                        stat, stat,
                        pltpu.VMEM((n_rows + 1, 2, 8, t), F32),
                        pltpu.VMEM((n_rows + 1, 2, _ACC_ROWS, t), F32)],
        compiler_params=pltpu.CompilerParams(
            dimension_semantics=("arbitrary", "arbitrary"),
            vmem_limit_bytes=_vmem_limit(56 * 1024 * 1024)),
        name="fox_attn",
    )(jnp.asarray(tbl), q_aug, k_aug, vbt, bias, szb)


def _out_kernel(x_ref, ga_ref, gb_ref, sga_ref, sgb_ref, gate_ref,
                woa_ref, wob_ref, wout_ref, gf_ref, o_ref):
    ya = jnp.dot(ga_ref[0], woa_ref[...], preferred_element_type=F32)
    yb = jnp.dot(gb_ref[0], wob_ref[...], preferred_element_type=F32)
    merged = sga_ref[0].astype(F32) * ya + sgb_ref[0].astype(F32) * yb
    d = jnp.dot(merged.astype(BF16), wout_ref[...], preferred_element_type=F32)
    xo = x_ref[0] + gate_ref[0] * d
    ms = jnp.mean(xo * xo, axis=-1, keepdims=True)
    o_ref[0] = xo * lax.rsqrt(ms + NORM_EPS) * gf_ref[...]


def _out(x, gated_a, gated_b, sga, sgb, gate, woa, wob, wout, gf, *, tm):
    b, s, _ = x.shape
    row = lambda bi, i: (bi, i, 0)
    per_b = lambda bi, i: (bi, 0, 0)
    const2 = lambda bi, i: (0, 0)
    once = pl.Buffered(1)
    return pl.pallas_call(
        _out_kernel,
        grid=(b, s // tm),
        in_specs=[pl.BlockSpec((1, tm, D_MODEL), row),
                  pl.BlockSpec((1, tm, SWA_WIDTH), row), pl.BlockSpec((1, tm, FOX_WIDTH), row),
                  pl.BlockSpec((1, tm, D_MODEL), row), pl.BlockSpec((1, tm, D_MODEL), row),
                  pl.BlockSpec((1, 1, D_MODEL), per_b),
                  pl.BlockSpec((SWA_WIDTH, D_MODEL), const2, pipeline_mode=once),
                  pl.BlockSpec((FOX_WIDTH, D_MODEL), const2, pipeline_mode=once),
                  pl.BlockSpec((D_MODEL, D_MODEL), const2, pipeline_mode=once),
                  pl.BlockSpec((1, D_MODEL), const2)],
        out_specs=pl.BlockSpec((1, tm, D_MODEL), row),
        out_shape=jax.ShapeDtypeStruct((b, s, D_MODEL), F32),
        compiler_params=pltpu.CompilerParams(
            dimension_semantics=("arbitrary", "arbitrary"),
            vmem_limit_bytes=_vmem_limit(48 * 1024 * 1024)),
        name="out_proj",
    )(x, gated_a, gated_b, sga, sgb, gate, woa, wob, wout, gf)


def _rot_cols(w, nheads):
    w3 = w.reshape(w.shape[0], nheads, HEAD_DIM)
    return jnp.concatenate([-w3[..., HALF:], w3[..., :HALF]], axis=-1).reshape(w.shape)


def _dup_heads(w, nheads):
    w3 = w.reshape(w.shape[0], nheads, 1, HEAD_DIM)
    return jnp.broadcast_to(w3, (w.shape[0], nheads, 2, HEAD_DIM)).reshape(w.shape[0], 2 * nheads * HEAD_DIM)


def _layout_w_in(w):
    o = np.cumsum([0, SWA_WIDTH, SWA_KV_WIDTH, SWA_KV_WIDTH, SWA_WIDTH, FOX_WIDTH, FOX_WIDTH, FOX_WIDTH,
                   FOX_HEADS, FOX_WIDTH, D_MODEL, D_MODEL]).tolist()
    qa, ka, va, za, qb, kb, vb, fb, zb, ga, gb = [w[:, o[k]:o[k + 1]] for k in range(11)]
    cols = [qa, _rot_cols(qa, SWA_Q_HEADS),
            _dup_heads(ka, SWA_KV_HEADS), _dup_heads(_rot_cols(ka, SWA_KV_HEADS), SWA_KV_HEADS),
            za, qb, kb, jnp.tile(fb, (1, LANES // FOX_HEADS)), zb, ga, gb]
    w_all = jnp.concatenate(cols, axis=1).astype(BF16)
    assert w_all.shape[1] == _PROJ_COLS
    return w_all, jnp.concatenate([va, vb], axis=1).T.astype(BF16)


def kernel(x, c, positions, w_ada, b_ada, g_norm, w_in, b_f, sinks, w_o_swa, w_o_fox, w_out, g_final):
    b, s, _ = x.shape
    depth = w_in.shape[0]
    assert depth == 1, "the output stage fuses the final RMSNorm into the single layer"
    inv_freq = ROPE_THETA ** (-jnp.arange(0, HEAD_DIM, 2, dtype=F32) / HEAD_DIM)
    invf = jnp.broadcast_to(inv_freq[:, None], (HALF, _PROJ_TM))
    pos_f = positions.astype(F32)[:, None, :]
    e_mat = jnp.asarray(_aug_placement(), BF16)
    c_pad = jnp.zeros((8, D_MODEL), F32).at[:b].set(c)
    for l in range(depth):
        ada = _ada(c_pad, w_ada[l], b_ada[l][None, :])[:b]
        shift, scale, gate = [ada[:, None, k * D_MODEL:(k + 1) * D_MODEL] for k in range(3)]
        w_all, w_vt = _layout_w_in(w_in[l])
        bf_rep = jnp.tile(b_f[l].astype(F32), LANES // FOX_HEADS)[None, :]
        qa, ka, vat, sza, q_aug, k_aug, vbt, szb, sga, sgb = _proj(
            x, pos_f, scale, shift, g_norm[l][None, :], invf, bf_rep, w_all, w_vt, e_mat, tm=_PROJ_TM)
        gated_a = _swa(sinks[l].astype(F32), qa, ka, vat, sza, tq=_SWA_TQ)
        gated_b = _fox(q_aug, k_aug, vbt, szb, t=_FOX_T)
        x = _out(x, gated_a, gated_b, sga, sgb, gate,
                 w_o_swa[l].astype(BF16), w_o_fox[l].astype(BF16), w_out[l].astype(BF16),
                 g_final[None, :], tm=_OUT_TM)
    return x
```

```python
import functools
import math

import numpy as np
import jax
import jax.numpy as jnp
from jax import lax
from jax.experimental import pallas as pl
from jax.experimental.pallas import tpu as pltpu

D_MODEL = 1024
HEAD_DIM = 64
HALF = HEAD_DIM // 2
SWA_Q_HEADS = 8
SWA_KV_HEADS = 2
SWA_WIDTH = SWA_Q_HEADS * HEAD_DIM
SWA_KV_WIDTH = SWA_KV_HEADS * HEAD_DIM
FOX_HEADS = 8
FOX_WIDTH = FOX_HEADS * HEAD_DIM
WINDOW = 128
ROPE_THETA = 10000.0
NORM_EPS = 1e-6
QK_SCALE = HEAD_DIM ** -0.5
LOG2E = math.log2(math.e)

LANES = 128
SUBLANES = 8
MIB = 1024 * 1024
V7X_VMEM_BYTES = 64 * MIB

_PROJ_TM = 512
_SWA_TQ = 1024
_FOX_T = 512
_OUT_TM = 1024

F32 = jnp.float32
BF16 = jnp.bfloat16

_QA, _KA, _ZA = 0, 1024, 1536
_QB, _KB, _FB, _ZB, _GA, _GB = 2048, 2560, 3072, 3200, 3712, 4736
_PROJ_COLS = 5760

_ONE_LANE = 24

_SWA_AHEAD = 2
_ONES_ROWS = 16


_PROJ_VMEM = 56 * MIB
_FOX_VMEM = 56 * MIB
_OUT_VMEM = 48 * MIB
assert max(_PROJ_VMEM, _FOX_VMEM, _OUT_VMEM) <= V7X_VMEM_BYTES - 8 * MIB


def _split3(v):
    hi = v.astype(BF16)
    r1 = v - hi.astype(F32)
    mid = r1.astype(BF16)
    lo = (r1 - mid.astype(F32)).astype(BF16)
    return hi, mid, lo


def _ada_kernel(c_ref, w_ref, b_ref, o_ref):
    o_ref[...] = jnp.dot(c_ref[...], w_ref[...], preferred_element_type=F32,
                         precision=lax.Precision.HIGHEST) + b_ref[...]


def _ada(c_pad, w_ada, b_ada):
    rows = c_pad.shape[0]
    n = w_ada.shape[1]
    nblk = n // D_MODEL
    return pl.pallas_call(
        _ada_kernel,
        grid=(nblk,),
        in_specs=[pl.BlockSpec((rows, D_MODEL), lambda j: (0, 0)),
                  pl.BlockSpec((D_MODEL, D_MODEL), lambda j: (0, j)),
                  pl.BlockSpec((1, D_MODEL), lambda j: (0, j))],
        out_specs=pl.BlockSpec((rows, D_MODEL), lambda j: (0, j)),
        out_shape=jax.ShapeDtypeStruct((rows, n), F32),
        name="ada_mod",
    )(c_pad, w_ada, b_ada)


def _sigmoid(z):
    return 0.5 * jnp.tanh(0.5 * z) + 0.5


def _proj_kernel(x_ref, pos_ref, scale_ref, shift_ref, gn_ref, invf_ref, bf_ref, w_ref, wvt_ref, e_ref,
                 qa_ref, ka_ref, vat_ref, sza_ref, qaug_ref, kaug_ref, vbt_ref, szb_ref,
                 sga_ref, sgb_ref, carry_ref, *, tm):
    i = pl.program_id(1)

    @pl.when(i == 0)
    def _():
        carry_ref[...] = jnp.zeros_like(carry_ref)

    x = x_ref[0]
    ms = jnp.mean(x * x, axis=-1, keepdims=True)
    y = x * lax.rsqrt(ms + NORM_EPS) * gn_ref[...]
    h = y * (1.0 + scale_ref[0]) + shift_ref[0]
    hb = h.astype(BF16)

    def proj(off, n):
        return jnp.dot(hb, w_ref[:, off:off + n], preferred_element_type=F32)

    lane = lax.broadcasted_iota(jnp.int32, (tm, LANES), 1)
    low = lane < HEAD_DIM


    zf = proj(_FB, LANES) + bf_ref[...]
    logf = jnp.minimum(zf, 0.0) - jnp.log1p(jnp.exp(-jnp.abs(zf)))
    grp = lane // FOX_HEADS
    hi, mid, lo = _split3(logf)
    zero = jnp.zeros((tm, LANES), F32)
    parts = jnp.where(grp == 0, hi.astype(F32),
                      jnp.where(grp == 1, mid.astype(F32),
                                jnp.where(grp == 2, lo.astype(F32), zero))).astype(BF16)
    tri = (lax.broadcasted_iota(jnp.int32, (tm, tm), 0)
           >= lax.broadcasted_iota(jnp.int32, (tm, tm), 1)).astype(BF16)
    sga_ref[0] = _sigmoid(proj(_GA, D_MODEL)).astype(BF16)
    rsum = jnp.dot(tri, parts, preferred_element_type=F32)
    c0 = rsum + pltpu.roll(rsum, LANES - FOX_HEADS, 1) + pltpu.roll(rsum, LANES - 2 * FOX_HEADS, 1)
    c0 = jnp.where(grp == 0, c0, zero)
    cl = c0 + pltpu.roll(c0, FOX_HEADS, 1) + pltpu.roll(c0, 2 * FOX_HEADS, 1)
    cum = cl + carry_ref[...]
    carry_ref[...] = cum[tm - 1:tm, :]

    hi, mid, lo = _split3(cum * LOG2E)
    one = jnp.ones((tm, LANES), F32)
    carrier = jnp.where(grp == 0, hi.astype(F32),
                        jnp.where(grp == 1, mid.astype(F32),
                                  jnp.where(grp == 2, lo.astype(F32),
                                            jnp.where(lane == _ONE_LANE, one, zero)))).astype(BF16)
    sgb_ref[0] = _sigmoid(proj(_GB, D_MODEL)).astype(BF16)
    aug = jnp.dot(carrier, e_ref[...], preferred_element_type=F32)

    z = proj(_ZA, SWA_WIDTH)
    sza_ref[0] = (z * _sigmoid(z)).astype(BF16)
    z = proj(_ZB, FOX_WIDTH)
    szb_ref[0] = (z * _sigmoid(z)).astype(BF16)

    ang = invf_ref[...] * pos_ref[0]
    cos = jnp.tile(jnp.cos(ang), (LANES // HALF, 1)).T
    sin = jnp.tile(jnp.sin(ang), (LANES // HALF, 1)).T
    r = proj(_QA, 2 * SWA_WIDTH)
    for c in range(SWA_WIDTH // LANES):
        a = c * LANES
        t = r[:, a:a + LANES] * cos + r[:, SWA_WIDTH + a:SWA_WIDTH + a + LANES] * sin
        qa_ref[0, :, a:a + LANES] = (t * (QK_SCALE * LOG2E)).astype(BF16)
    r = proj(_KA, 4 * SWA_KV_WIDTH)
    for c in range(2 * SWA_KV_WIDTH // LANES):
        a = c * LANES
        t = r[:, a:a + LANES] * cos + r[:, 2 * SWA_KV_WIDTH + a:2 * SWA_KV_WIDTH + a + LANES] * sin
        ka_ref[0, :, a:a + LANES] = t.astype(BF16)

    rq = proj(_QB, FOX_WIDTH)
    rk = proj(_KB, FOX_WIDTH)
    for hd in range(FOX_HEADS):
        a = (hd // 2) * LANES
        gq = aug[:, a:a + LANES]
        gk = aug[:, FOX_WIDTH + a:FOX_WIDTH + a + LANES]
        qv = rq[:, a:a + LANES] * (QK_SCALE * LOG2E)
        kv = rk[:, a:a + LANES]
        if hd % 2 == 0:
            qaug_ref[0, hd] = jnp.where(low, qv, gq).astype(BF16)
            kaug_ref[0, hd] = jnp.where(low, kv, gk).astype(BF16)
        else:
            qaug_ref[0, hd] = jnp.where(low, gq, qv).astype(BF16)
            kaug_ref[0, hd] = jnp.where(low, gk, kv).astype(BF16)

    vt = lax.dot_general(wvt_ref[...], hb, (((1,), (1,)), ((), ())), preferred_element_type=F32)
    vat_ref[0] = vt[0:SWA_KV_WIDTH].astype(BF16)
    vbt_ref[0] = vt[SWA_KV_WIDTH:].astype(BF16)


def _aug_placement():
    e = np.zeros((LANES, 2 * FOX_HEADS * HEAD_DIM), np.float32)
    for hd in range(FOX_HEADS):
        base = (hd // 2) * LANES + (HEAD_DIM if hd % 2 == 0 else 0)
        kbase = FOX_HEADS * HEAD_DIM + base
        for part in range(3):
            e[part * FOX_HEADS + hd, base + part] = 1.0
            e[_ONE_LANE, base + 3 + part] = 1.0
            e[_ONE_LANE, kbase + part] = 1.0
            e[part * FOX_HEADS + hd, kbase + 3 + part] = -1.0
    return e


def _proj(x, pos_f, scale, shift, gn, invf, bf_rep, w_all, w_vt, e_mat, *, tm):
    b, s, _ = x.shape
    grid = (b, s // tm)
    row = lambda bi, i: (bi, i, 0)
    per_b = lambda bi, i: (bi, 0, 0)
    const2 = lambda bi, i: (0, 0)
    once = pl.Buffered(1)
    in_specs = [
        pl.BlockSpec((1, tm, D_MODEL), row),
        pl.BlockSpec((1, 1, tm), lambda bi, i: (bi, 0, i)),
        pl.BlockSpec((1, 1, D_MODEL), per_b),
        pl.BlockSpec((1, 1, D_MODEL), per_b),
        pl.BlockSpec((1, D_MODEL), const2),
        pl.BlockSpec((HALF, tm), const2),
        pl.BlockSpec((1, LANES), const2),
        pl.BlockSpec((D_MODEL, _PROJ_COLS), const2, pipeline_mode=once),
        pl.BlockSpec((SWA_KV_WIDTH + FOX_WIDTH, D_MODEL), const2, pipeline_mode=once),
        pl.BlockSpec((LANES, 2 * FOX_WIDTH), const2, pipeline_mode=once),
    ]
    head4 = lambda bi, i: (bi, 0, i, 0)
    out_specs = [
        pl.BlockSpec((1, tm, SWA_WIDTH), row),
        pl.BlockSpec((1, tm, 2 * SWA_KV_WIDTH), row),
        pl.BlockSpec((1, SWA_KV_WIDTH, tm), lambda bi, i: (bi, 0, i)),
        pl.BlockSpec((1, tm, SWA_WIDTH), row),
        pl.BlockSpec((1, FOX_HEADS, tm, LANES), head4),
        pl.BlockSpec((1, FOX_HEADS, tm, LANES), head4),
        pl.BlockSpec((1, FOX_WIDTH, tm), lambda bi, i: (bi, 0, i)),
        pl.BlockSpec((1, tm, FOX_WIDTH), row),
        pl.BlockSpec((1, tm, D_MODEL), row),
        pl.BlockSpec((1, tm, D_MODEL), row),
    ]
    sds = jax.ShapeDtypeStruct
    out_shape = [
        sds((b, s, SWA_WIDTH), BF16), sds((b, s, 2 * SWA_KV_WIDTH), BF16),
        sds((b, SWA_KV_WIDTH, s), BF16), sds((b, s, SWA_WIDTH), BF16),
        sds((b, FOX_HEADS, s, LANES), BF16), sds((b, FOX_HEADS, s, LANES), BF16),
        sds((b, FOX_WIDTH, s), BF16), sds((b, s, FOX_WIDTH), BF16),
        sds((b, s, D_MODEL), BF16), sds((b, s, D_MODEL), BF16),
    ]
    return pl.pallas_call(
        functools.partial(_proj_kernel, tm=tm),
        grid=grid, in_specs=in_specs, out_specs=out_specs, out_shape=out_shape,
        scratch_shapes=[pltpu.VMEM((1, LANES), F32)],
        compiler_params=pltpu.CompilerParams(
            dimension_semantics=("arbitrary", "arbitrary"),
            vmem_limit_bytes=_PROJ_VMEM),
        name="in_proj",
    )(x, pos_f, scale, shift, gn, invf, bf_rep, w_all, w_vt, e_mat)


def _swa_kernel(sinks_ref, q_ref, kc_ref, kp_ref, vtc_ref, vtp_ref, bias_ref, sz_ref, o_ref,
                kband, vtband, *, tq):
    i = pl.program_id(1)
    nblk = tq // WINDOW
    nhq = SWA_Q_HEADS // SWA_KV_HEADS
    kband[0:WINDOW] = kp_ref[0]
    kband[WINDOW:] = kc_ref[0]
    vtband[:, 0:WINDOW] = vtp_ref[0]
    vtband[:, WINDOW:] = vtc_ref[0]

    cols = nhq * WINDOW
    first_plane = jnp.where(i > 0, 0, 1)
    headid = lax.broadcasted_iota(jnp.int32, (1, cols), 1) // WINDOW
    low = lax.broadcasted_iota(jnp.int32, (WINDOW, LANES), 1) < HEAD_DIM
    ones = jnp.ones((_ONES_ROWS, 2 * WINDOW), BF16)
    nt = (((1,), (1,)), ((), ()))

    def logits(r, g):
        kb = kband[r * WINDOW:(r + 2) * WINDOW, g * LANES:(g + 1) * LANES]
        qs = []
        for c in range(2):
            a = (2 * g + c) * LANES
            qc = q_ref[0, r * WINDOW:(r + 1) * WINDOW, a:a + LANES]
            qs.append(jnp.where(low, qc, jnp.zeros_like(qc)))
            qs.append(jnp.where(low, jnp.zeros_like(qc), qc))
        qst = jnp.concatenate(qs, axis=0)
        s = lax.dot_general(kb, qst, nt, preferred_element_type=F32)
        return s + bias_ref[first_plane if r == 0 else 0]

    def attend(r, g, s):
        sink = jnp.zeros((1, cols), F32)
        for k in range(nhq):
            sink = jnp.where(headid == k, sinks_ref[g * nhq + k] * LOG2E, sink)
        m = jnp.maximum(jnp.max(s, axis=0, keepdims=True), sink)
        p = jnp.exp2(s - m).astype(BF16)
        vt = vtband[g * HEAD_DIM:(g + 1) * HEAD_DIM, r * WINDOW:(r + 2) * WINDOW]
        pv = jnp.dot(jnp.concatenate([vt, ones], axis=0), p, preferred_element_type=F32)
        den = pv[HEAD_DIM:HEAD_DIM + 1] + jnp.exp2(sink - m)
        o_t = pv[0:HEAD_DIM] / den
        for c in range(2):
            a = (2 * g + c) * LANES
            pair = jnp.concatenate([o_t[:, 2 * c * WINDOW:(2 * c + 1) * WINDOW],
                                    o_t[:, (2 * c + 1) * WINDOW:(2 * c + 2) * WINDOW]], axis=0)
            gate = sz_ref[0, r * WINDOW:(r + 1) * WINDOW, a:a + LANES].astype(F32)
            o_ref[0, r * WINDOW:(r + 1) * WINDOW, a:a + LANES] = (pair.T * gate).astype(BF16)

    work = [(r, g) for r in range(nblk) for g in range(SWA_KV_HEADS)]
    pending = [logits(*w) for w in work[:_SWA_AHEAD]]
    for n, (r, g) in enumerate(work):
        if n + _SWA_AHEAD < len(work):
            pending.append(logits(*work[n + _SWA_AHEAD]))
        attend(r, g, pending.pop(0))


def _swa(sinks, qa, ka, vat, sza, *, tq):
    b, s, _ = qa.shape
    kvw = 2 * SWA_KV_WIDTH
    per = tq // WINDOW
    cur = lambda bi, i: (bi, i, 0)
    prev = lambda bi, i: (bi, jnp.maximum(i * per - 1, 0), 0)
    cur_t = lambda bi, i: (bi, 0, i)
    prev_t = lambda bi, i: (bi, 0, jnp.maximum(i * per - 1, 0))
    key = np.arange(2 * WINDOW)[:, None]
    qt = np.tile(np.arange(WINDOW), SWA_Q_HEADS // SWA_KV_HEADS)[None, :]
    band = np.where(key < WINDOW, key > qt, (key - WINDOW) <= qt)
    planes = np.stack([band, band & (key >= WINDOW)])
    bias = jnp.asarray(np.where(planes, 0.0, -np.inf).astype(np.float32))
    return pl.pallas_call(
        functools.partial(_swa_kernel, tq=tq),
        grid=(b, s // tq),
        in_specs=[pl.BlockSpec(memory_space=pltpu.SMEM),
                  pl.BlockSpec((1, tq, SWA_WIDTH), cur),
                  pl.BlockSpec((1, tq, kvw), cur),
                  pl.BlockSpec((1, WINDOW, kvw), prev),
                  pl.BlockSpec((1, SWA_KV_WIDTH, tq), cur_t),
                  pl.BlockSpec((1, SWA_KV_WIDTH, WINDOW), prev_t),
                  pl.BlockSpec(bias.shape, lambda bi, i: (0, 0, 0), pipeline_mode=pl.Buffered(1)),
                  pl.BlockSpec((1, tq, SWA_WIDTH), cur)],
        out_specs=pl.BlockSpec((1, tq, SWA_WIDTH), cur),
        out_shape=jax.ShapeDtypeStruct((b, s, SWA_WIDTH), BF16),
        scratch_shapes=[pltpu.VMEM((tq + WINDOW, kvw), BF16),
                        pltpu.VMEM((SWA_KV_WIDTH, tq + WINDOW), BF16)],
        compiler_params=pltpu.CompilerParams(dimension_semantics=("arbitrary", "arbitrary")),
        name="swa_attn",
    )(sinks, qa, ka, ka, vat, vat, bias, sza)


_ACC_ROWS = HEAD_DIM + _ONES_ROWS
_PIPE = 2
_FOX_UNROLL = 2


def _fox_schedule(n_rows):
    phases = [[(i, i, i) for i in range(n_rows)],
              [(i, j, i) for i in range(n_rows) for j in range(i)]]
    dummy = (0, 0, n_rows)
    tbl, spans = [], []
    for seq in phases:
        n_steps = len(seq) + _PIPE
        n_steps += -n_steps % _FOX_UNROLL
        spans.append((len(tbl), n_steps))
        tbl += [dummy] * _PIPE + seq + [dummy] * (n_steps - len(seq))
    return np.asarray(tbl, np.int32).T.copy(), spans


def _fox_kernel(tbl_ref, q_ref, k_ref, vt_ref, bias_ref, sz_ref, o_ref,
                s0, s1, mx0, mx1, p0, p1, a0, a1, m_ref, acc_ref, *, t, n_rows, spans):
    nt = (((1,), (1,)), ((), ()))
    g = t // SUBLANES
    @pl.when(jnp.logical_and(pl.program_id(0) == 0, pl.program_id(1) == 0))
    def _():
        for ref in (s0, s1, mx0, mx1, p0, p1, a0, a1):
            ref[...] = jnp.zeros_like(ref)
        m_ref[n_rows] = jnp.zeros_like(m_ref[n_rows])
        acc_ref[n_rows] = jnp.zeros_like(acc_ref[n_rows])

    def step(k, diagonal, s_w, mx_w, s_r, mx_r, p_w, a_w, p_r, a_r):
        qrow = pl.multiple_of(tbl_ref[0, k + 2] * t, t)
        krow = pl.multiple_of(tbl_ref[1, k + 2] * t, t)
        for hh in range(2):
            s = lax.dot_general(k_ref[0, hh, pl.ds(krow, t), :], q_ref[0, hh, pl.ds(qrow, t), :], nt,
                                preferred_element_type=F32)
            if diagonal:
                s = s + bias_ref[...]
            s_w[hh] = s
            mx_w[hh] = jnp.max(s.reshape(g, SUBLANES, t), axis=0)

        srow = tbl_ref[2, k + 1]
        for hh in range(2):
            m_tile = jnp.max(mx_r[hh], axis=0, keepdims=True)
            if diagonal:
                m_new = jnp.broadcast_to(m_tile, (SUBLANES, t))
            else:
                m_old = m_ref[srow, hh]
                m_new = jnp.maximum(m_old, m_tile)
                a_w[hh] = jnp.exp2(m_old - m_new)
            m_ref[srow, hh] = m_new
            p = jnp.exp2(s_r[hh].reshape(g, SUBLANES, t) - m_new[None])
            p_w[hh] = p.reshape(t, t).astype(BF16)

        vcol = pl.multiple_of(tbl_ref[1, k] * t, t)
        arow = tbl_ref[2, k]
        for hh in range(2):
            vt = vt_ref[0, hh * HEAD_DIM:(hh + 1) * HEAD_DIM, pl.ds(vcol, t)]
            lhs = jnp.concatenate([vt, jnp.ones((_ONES_ROWS, t), BF16)], axis=0)
            part = jnp.dot(lhs, p_r[hh], preferred_element_type=F32)
            if diagonal:
                acc_ref[arow, hh] = part
            else:
                old = acc_ref[arow, hh].reshape(_ACC_ROWS // SUBLANES, SUBLANES, t)
                acc_ref[arow, hh] = (a_r[hh][None] * old).reshape(_ACC_ROWS, t) + part

    assert _FOX_UNROLL % 2 == 0
    for diagonal, (first, n_steps) in zip((True, False), spans):
        def body(kk, carry, diagonal=diagonal, first=first):
            k = first + _FOX_UNROLL * kk
            for u in range(0, _FOX_UNROLL, 2):
                step(k + u, diagonal, s0, mx0, s1, mx1, p1, a1, p0, a0)
                step(k + u + 1, diagonal, s1, mx1, s0, mx0, p0, a0, p1, a1)
            return carry

        lax.fori_loop(0, n_steps // _FOX_UNROLL, body, 0)

    def finish_row(i, carry):
        halves = []
        for hh in range(2):
            a = acc_ref[i, hh]
            halves.append(a[0:HEAD_DIM] / a[HEAD_DIM:HEAD_DIM + 1])
        o_t = jnp.concatenate(halves, axis=0)
        rows = pl.ds(pl.multiple_of(i * t, t), t)
        o_ref[0, rows, :] = (o_t.T * sz_ref[0, rows, :].astype(F32)).astype(BF16)
        return carry

    lax.fori_loop(0, n_rows, finish_row, 0)


def _fox(q_aug, k_aug, vbt, szb, *, t):
    b, _, s, _ = q_aug.shape
    npair = FOX_HEADS // 2
    n_rows = s // t
    tbl, spans = _fox_schedule(n_rows)
    tri = np.triu(np.ones((t, t), bool))
    bias = jnp.asarray(np.where(tri, 0.0, -np.inf).astype(np.float32))
    pair = lambda bi, p: (bi, p, 0, 0)
    stat = pltpu.VMEM((2, SUBLANES, t), F32)
    return pl.pallas_call(
        functools.partial(_fox_kernel, t=t, n_rows=n_rows, spans=tuple(spans)),
        grid=(b, npair),
        in_specs=[pl.BlockSpec(memory_space=pltpu.SMEM),
                  pl.BlockSpec((1, 2, s, LANES), pair),
                  pl.BlockSpec((1, 2, s, LANES), pair),
                  pl.BlockSpec((1, 2 * HEAD_DIM, s), lambda bi, p: (bi, p, 0)),
                  pl.BlockSpec((t, t), lambda bi, p: (0, 0), pipeline_mode=pl.Buffered(1)),
                  pl.BlockSpec((1, s, LANES), lambda bi, p: (bi, 0, p))],
        out_specs=pl.BlockSpec((1, s, LANES), lambda bi, p: (bi, 0, p)),
        out_shape=jax.ShapeDtypeStruct((b, s, FOX_WIDTH), BF16),
        scratch_shapes=[pltpu.VMEM((2, t, t), F32), pltpu.VMEM((2, t, t), F32),
                        stat, stat,
                        pltpu.VMEM((2, t, t), BF16), pltpu.VMEM((2, t, t), BF16),
                        stat, stat,
                        pltpu.VMEM((n_rows + 1, 2, SUBLANES, t), F32),
                        pltpu.VMEM((n_rows + 1, 2, _ACC_ROWS, t), F32)],
        compiler_params=pltpu.CompilerParams(
            dimension_semantics=("arbitrary", "arbitrary"),
            vmem_limit_bytes=_FOX_VMEM),
        name="fox_attn",
    )(jnp.asarray(tbl), q_aug, k_aug, vbt, bias, szb)


def _out_kernel(x_ref, ga_ref, gb_ref, sga_ref, sgb_ref, gate_ref,
                woa_ref, wob_ref, wout_ref, gf_ref, o_ref):
    ya = jnp.dot(ga_ref[0], woa_ref[...], preferred_element_type=F32)
    yb = jnp.dot(gb_ref[0], wob_ref[...], preferred_element_type=F32)
    merged = sga_ref[0].astype(F32) * ya + sgb_ref[0].astype(F32) * yb
    d = jnp.dot(merged.astype(BF16), wout_ref[...], preferred_element_type=F32)
    xo = x_ref[0] + gate_ref[0] * d
    ms = jnp.mean(xo * xo, axis=-1, keepdims=True)
    o_ref[0] = xo * lax.rsqrt(ms + NORM_EPS) * gf_ref[...]


def _out(x, gated_a, gated_b, sga, sgb, gate, woa, wob, wout, gf, *, tm):
    b, s, _ = x.shape
    row = lambda bi, i: (bi, i, 0)
    per_b = lambda bi, i: (bi, 0, 0)
    const2 = lambda bi, i: (0, 0)
    once = pl.Buffered(1)
    return pl.pallas_call(
        _out_kernel,
        grid=(b, s // tm),
        in_specs=[pl.BlockSpec((1, tm, D_MODEL), row),
                  pl.BlockSpec((1, tm, SWA_WIDTH), row), pl.BlockSpec((1, tm, FOX_WIDTH), row),
                  pl.BlockSpec((1, tm, D_MODEL), row), pl.BlockSpec((1, tm, D_MODEL), row),
                  pl.BlockSpec((1, 1, D_MODEL), per_b),
                  pl.BlockSpec((SWA_WIDTH, D_MODEL), const2, pipeline_mode=once),
                  pl.BlockSpec((FOX_WIDTH, D_MODEL), const2, pipeline_mode=once),
                  pl.BlockSpec((D_MODEL, D_MODEL), const2, pipeline_mode=once),
                  pl.BlockSpec((1, D_MODEL), const2)],
        out_specs=pl.BlockSpec((1, tm, D_MODEL), row),
        out_shape=jax.ShapeDtypeStruct((b, s, D_MODEL), F32),
        compiler_params=pltpu.CompilerParams(
            dimension_semantics=("arbitrary", "arbitrary"),
            vmem_limit_bytes=_OUT_VMEM),
        name="out_proj",
    )(x, gated_a, gated_b, sga, sgb, gate, woa, wob, wout, gf)


def _rot_cols(w, nheads):
    w3 = w.reshape(w.shape[0], nheads, HEAD_DIM)
    return jnp.concatenate([-w3[..., HALF:], w3[..., :HALF]], axis=-1).reshape(w.shape)


def _dup_heads(w, nheads):
    w3 = w.reshape(w.shape[0], nheads, 1, HEAD_DIM)
    return jnp.broadcast_to(w3, (w.shape[0], nheads, 2, HEAD_DIM)).reshape(w.shape[0], 2 * nheads * HEAD_DIM)


def _layout_w_in(w):
    o = np.cumsum([0, SWA_WIDTH, SWA_KV_WIDTH, SWA_KV_WIDTH, SWA_WIDTH, FOX_WIDTH, FOX_WIDTH, FOX_WIDTH,
                   FOX_HEADS, FOX_WIDTH, D_MODEL, D_MODEL]).tolist()
    qa, ka, va, za, qb, kb, vb, fb, zb, ga, gb = [w[:, o[k]:o[k + 1]] for k in range(11)]
    cols = [qa, _rot_cols(qa, SWA_Q_HEADS),
            _dup_heads(ka, SWA_KV_HEADS), _dup_heads(_rot_cols(ka, SWA_KV_HEADS), SWA_KV_HEADS),
            za, qb, kb, jnp.tile(fb, (1, LANES // FOX_HEADS)), zb, ga, gb]
    w_all = jnp.concatenate(cols, axis=1).astype(BF16)
    assert w_all.shape[1] == _PROJ_COLS
    return w_all, jnp.concatenate([va, vb], axis=1).T.astype(BF16)


def kernel(x, c, positions, w_ada, b_ada, g_norm, w_in, b_f, sinks, w_o_swa, w_o_fox, w_out, g_final):
    b, s, _ = x.shape
    depth = w_in.shape[0]
    assert depth == 1, "the output stage fuses the final RMSNorm into the single layer"
    inv_freq = ROPE_THETA ** (-jnp.arange(0, HEAD_DIM, 2, dtype=F32) / HEAD_DIM)
    invf = jnp.broadcast_to(inv_freq[:, None], (HALF, _PROJ_TM))
    pos_f = positions.astype(F32)[:, None, :]
    e_mat = jnp.asarray(_aug_placement(), BF16)
    c_pad = jnp.zeros((SUBLANES, D_MODEL), F32).at[:b].set(c)
    for l in range(depth):
        ada = _ada(c_pad, w_ada[l], b_ada[l][None, :])[:b]
        shift, scale, gate = [ada[:, None, k * D_MODEL:(k + 1) * D_MODEL] for k in range(3)]
        w_all, w_vt = _layout_w_in(w_in[l])
        bf_rep = jnp.tile(b_f[l].astype(F32), LANES // FOX_HEADS)[None, :]
        qa, ka, vat, sza, q_aug, k_aug, vbt, szb, sga, sgb = _proj(
            x, pos_f, scale, shift, g_norm[l][None, :], invf, bf_rep, w_all, w_vt, e_mat, tm=_PROJ_TM)
        gated_a = _swa(sinks[l].astype(F32), qa, ka, vat, sza, tq=_SWA_TQ)
        gated_b = _fox(q_aug, k_aug, vbt, szb, t=_FOX_T)
        x = _out(x, gated_a, gated_b, sga, sgb, gate,
                 w_o_swa[l].astype(BF16), w_o_fox[l].astype(BF16), w_out[l].astype(BF16),
                 g_final[None, :], tm=_OUT_TM)
    return x
```

```python
import functools
import math

import numpy as np
import jax
import jax.numpy as jnp
from jax import lax
from jax.experimental import pallas as pl
from jax.experimental.pallas import tpu as pltpu

D_MODEL = 1024
HEAD_DIM = 64
HALF = HEAD_DIM // 2
SWA_Q_HEADS = 8
SWA_KV_HEADS = 2
SWA_WIDTH = SWA_Q_HEADS * HEAD_DIM
SWA_KV_WIDTH = SWA_KV_HEADS * HEAD_DIM
FOX_HEADS = 8
FOX_WIDTH = FOX_HEADS * HEAD_DIM
WINDOW = 128
ROPE_THETA = 10000.0
NORM_EPS = 1e-6
QK_SCALE = HEAD_DIM ** -0.5
LOG2E = math.log2(math.e)

LANES = 128
SUBLANES = 8
MIB = 1024 * 1024
V7X_VMEM_BYTES = 64 * MIB

_PROJ_TM = 512
_SWA_TQ = 1024
_FOX_T = 512
_OUT_TM = 1024

F32 = jnp.float32
BF16 = jnp.bfloat16

_QA, _KA, _ZA = 0, 512, 768
_QB, _KB, _FB, _ZB, _GA, _GB = 1280, 1792, 2304, 2432, 2944, 3968
_PROJ_COLS = 4992

_ONE_LANE = 24

_SWA_AHEAD = 2
_ONES_ROWS = 16


_PROJ_VMEM = 56 * MIB
_FOX_VMEM = 56 * MIB
_OUT_VMEM = 48 * MIB
assert max(_PROJ_VMEM, _FOX_VMEM, _OUT_VMEM) <= V7X_VMEM_BYTES - 8 * MIB


def _split3(v):
    hi = v.astype(BF16)
    r1 = v - hi.astype(F32)
    mid = r1.astype(BF16)
    lo = (r1 - mid.astype(F32)).astype(BF16)
    return hi, mid, lo


def _ada_kernel(c_ref, w_ref, b_ref, o_ref):
    o_ref[...] = jnp.dot(c_ref[...], w_ref[...], preferred_element_type=F32,
                         precision=lax.Precision.HIGHEST) + b_ref[...]


def _ada(c_pad, w_ada, b_ada):
    rows = c_pad.shape[0]
    n = w_ada.shape[1]
    nblk = n // D_MODEL
    return pl.pallas_call(
        _ada_kernel,
        grid=(nblk,),
        in_specs=[pl.BlockSpec((rows, D_MODEL), lambda j: (0, 0)),
                  pl.BlockSpec((D_MODEL, D_MODEL), lambda j: (0, j)),
                  pl.BlockSpec((1, D_MODEL), lambda j: (0, j))],
        out_specs=pl.BlockSpec((rows, D_MODEL), lambda j: (0, j)),
        out_shape=jax.ShapeDtypeStruct((rows, n), F32),
        name="ada_mod",
    )(c_pad, w_ada, b_ada)


def _sigmoid(z):
    return 0.5 * jnp.tanh(0.5 * z) + 0.5


def _proj_kernel(x_ref, pos_ref, scale_ref, shift_ref, gn_ref, invf_ref, bf_ref, w_ref, wvt_ref, e_ref,
                 qa_ref, ka_ref, vat_ref, sza_ref, qaug_ref, kaug_ref, vbt_ref, szb_ref,
                 sga_ref, sgb_ref, carry_ref, *, tm):
    i = pl.program_id(1)

    @pl.when(i == 0)
    def _():
        carry_ref[...] = jnp.zeros_like(carry_ref)

    x = x_ref[0]
    ms = jnp.mean(x * x, axis=-1, keepdims=True)
    y = x * lax.rsqrt(ms + NORM_EPS) * gn_ref[...]
    h = y * (1.0 + scale_ref[0]) + shift_ref[0]
    hb = h.astype(BF16)

    def proj(off, n):
        return jnp.dot(hb, w_ref[:, off:off + n], preferred_element_type=F32)

    lane = lax.broadcasted_iota(jnp.int32, (tm, LANES), 1)
    low = lane < HEAD_DIM


    zf = proj(_FB, LANES) + bf_ref[...]
    logf = jnp.minimum(zf, 0.0) - jnp.log1p(jnp.exp(-jnp.abs(zf)))
    grp = lane // FOX_HEADS
    hi, mid, lo = _split3(logf)
    zero = jnp.zeros((tm, LANES), F32)
    parts = jnp.where(grp == 0, hi.astype(F32),
                      jnp.where(grp == 1, mid.astype(F32),
                                jnp.where(grp == 2, lo.astype(F32), zero))).astype(BF16)
    tri = (lax.broadcasted_iota(jnp.int32, (tm, tm), 0)
           >= lax.broadcasted_iota(jnp.int32, (tm, tm), 1)).astype(BF16)
    sga_ref[0] = _sigmoid(proj(_GA, D_MODEL)).astype(BF16)
    rsum = jnp.dot(tri, parts, preferred_element_type=F32)
    c0 = rsum + pltpu.roll(rsum, LANES - FOX_HEADS, 1) + pltpu.roll(rsum, LANES - 2 * FOX_HEADS, 1)
    c0 = jnp.where(grp == 0, c0, zero)
    cl = c0 + pltpu.roll(c0, FOX_HEADS, 1) + pltpu.roll(c0, 2 * FOX_HEADS, 1)
    cum = cl + carry_ref[...]
    carry_ref[...] = cum[tm - 1:tm, :]

    hi, mid, lo = _split3(cum * LOG2E)
    one = jnp.ones((tm, LANES), F32)
    carrier = jnp.where(grp == 0, hi.astype(F32),
                        jnp.where(grp == 1, mid.astype(F32),
                                  jnp.where(grp == 2, lo.astype(F32),
                                            jnp.where(lane == _ONE_LANE, one, zero)))).astype(BF16)
    sgb_ref[0] = _sigmoid(proj(_GB, D_MODEL)).astype(BF16)
    aug = jnp.dot(carrier, e_ref[...], preferred_element_type=F32)

    z = proj(_ZA, SWA_WIDTH)
    sza_ref[0] = (z * _sigmoid(z)).astype(BF16)
    z = proj(_ZB, FOX_WIDTH)
    szb_ref[0] = (z * _sigmoid(z)).astype(BF16)

    ang = invf_ref[...] * pos_ref[0]
    cos = jnp.tile(jnp.cos(ang), (LANES // HALF, 1)).T
    sn = jnp.sin(ang)
    sin = jnp.concatenate([-sn, -sn, sn, sn], axis=0).T
    r = proj(_QA, SWA_WIDTH)
    for c in range(SWA_WIDTH // LANES):
        rc = r[:, c * LANES:(c + 1) * LANES]
        t = rc * cos + pltpu.roll(rc, HEAD_DIM, 1) * sin
        qa_ref[0, :, c * LANES:(c + 1) * LANES] = (t * (QK_SCALE * LOG2E)).astype(BF16)
    r = proj(_KA, 2 * SWA_KV_WIDTH)
    for c in range(2 * SWA_KV_WIDTH // LANES):
        rc = r[:, c * LANES:(c + 1) * LANES]
        t = rc * cos + pltpu.roll(rc, HEAD_DIM, 1) * sin
        ka_ref[0, :, c * LANES:(c + 1) * LANES] = t.astype(BF16)

    rq = proj(_QB, FOX_WIDTH)
    rk = proj(_KB, FOX_WIDTH)
    for hd in range(FOX_HEADS):
        a = (hd // 2) * LANES
        gq = aug[:, a:a + LANES]
        gk = aug[:, FOX_WIDTH + a:FOX_WIDTH + a + LANES]
        qv = rq[:, a:a + LANES] * (QK_SCALE * LOG2E)
        kv = rk[:, a:a + LANES]
        if hd % 2 == 0:
            qaug_ref[0, hd] = jnp.where(low, qv, gq).astype(BF16)
            kaug_ref[0, hd] = jnp.where(low, kv, gk).astype(BF16)
        else:
            qaug_ref[0, hd] = jnp.where(low, gq, qv).astype(BF16)
            kaug_ref[0, hd] = jnp.where(low, gk, kv).astype(BF16)

    vt = lax.dot_general(wvt_ref[...], hb, (((1,), (1,)), ((), ())), preferred_element_type=F32)
    vat_ref[0] = vt[0:SWA_KV_WIDTH].astype(BF16)
    vbt_ref[0] = vt[SWA_KV_WIDTH:].astype(BF16)


def _aug_placement():
    e = np.zeros((LANES, 2 * FOX_HEADS * HEAD_DIM), np.float32)
    for hd in range(FOX_HEADS):
        base = (hd // 2) * LANES + (HEAD_DIM if hd % 2 == 0 else 0)
        kbase = FOX_HEADS * HEAD_DIM + base
        for part in range(3):
            e[part * FOX_HEADS + hd, base + part] = 1.0
            e[_ONE_LANE, base + 3 + part] = 1.0
            e[_ONE_LANE, kbase + part] = 1.0
            e[part * FOX_HEADS + hd, kbase + 3 + part] = -1.0
    return e


def _proj(x, pos_f, scale, shift, gn, invf, bf_rep, w_all, w_vt, e_mat, *, tm):
    b, s, _ = x.shape
    grid = (b, s // tm)
    row = lambda bi, i: (bi, i, 0)
    per_b = lambda bi, i: (bi, 0, 0)
    const2 = lambda bi, i: (0, 0)
    once = pl.Buffered(1)
    in_specs = [
        pl.BlockSpec((1, tm, D_MODEL), row),
        pl.BlockSpec((1, 1, tm), lambda bi, i: (bi, 0, i)),
        pl.BlockSpec((1, 1, D_MODEL), per_b),
        pl.BlockSpec((1, 1, D_MODEL), per_b),
        pl.BlockSpec((1, D_MODEL), const2),
        pl.BlockSpec((HALF, tm), const2),
        pl.BlockSpec((1, LANES), const2),
        pl.BlockSpec((D_MODEL, _PROJ_COLS), const2, pipeline_mode=once),
        pl.BlockSpec((SWA_KV_WIDTH + FOX_WIDTH, D_MODEL), const2, pipeline_mode=once),
        pl.BlockSpec((LANES, 2 * FOX_WIDTH), const2, pipeline_mode=once),
    ]
    head4 = lambda bi, i: (bi, 0, i, 0)
    out_specs = [
        pl.BlockSpec((1, tm, SWA_WIDTH), row),
        pl.BlockSpec((1, tm, 2 * SWA_KV_WIDTH), row),
        pl.BlockSpec((1, SWA_KV_WIDTH, tm), lambda bi, i: (bi, 0, i)),
        pl.BlockSpec((1, tm, SWA_WIDTH), row),
        pl.BlockSpec((1, FOX_HEADS, tm, LANES), head4),
        pl.BlockSpec((1, FOX_HEADS, tm, LANES), head4),
        pl.BlockSpec((1, FOX_WIDTH, tm), lambda bi, i: (bi, 0, i)),
        pl.BlockSpec((1, tm, FOX_WIDTH), row),
        pl.BlockSpec((1, tm, D_MODEL), row),
        pl.BlockSpec((1, tm, D_MODEL), row),
    ]
    sds = jax.ShapeDtypeStruct
    out_shape = [
        sds((b, s, SWA_WIDTH), BF16), sds((b, s, 2 * SWA_KV_WIDTH), BF16),
        sds((b, SWA_KV_WIDTH, s), BF16), sds((b, s, SWA_WIDTH), BF16),
        sds((b, FOX_HEADS, s, LANES), BF16), sds((b, FOX_HEADS, s, LANES), BF16),
        sds((b, FOX_WIDTH, s), BF16), sds((b, s, FOX_WIDTH), BF16),
        sds((b, s, D_MODEL), BF16), sds((b, s, D_MODEL), BF16),
    ]
    return pl.pallas_call(
        functools.partial(_proj_kernel, tm=tm),
        grid=grid, in_specs=in_specs, out_specs=out_specs, out_shape=out_shape,
        scratch_shapes=[pltpu.VMEM((1, LANES), F32)],
        compiler_params=pltpu.CompilerParams(
            dimension_semantics=("arbitrary", "arbitrary"),
            vmem_limit_bytes=_PROJ_VMEM),
        name="in_proj",
    )(x, pos_f, scale, shift, gn, invf, bf_rep, w_all, w_vt, e_mat)


def _swa_kernel(sinks_ref, q_ref, kc_ref, kp_ref, vtc_ref, vtp_ref, bias_ref, sz_ref, o_ref,
                kband, vtband, *, tq):
    i = pl.program_id(1)
    nblk = tq // WINDOW
    nhq = SWA_Q_HEADS // SWA_KV_HEADS
    kband[0:WINDOW] = kp_ref[0]
    kband[WINDOW:] = kc_ref[0]
    vtband[:, 0:WINDOW] = vtp_ref[0]
    vtband[:, WINDOW:] = vtc_ref[0]

    cols = nhq * WINDOW
    first_plane = jnp.where(i > 0, 0, 1)
    headid = lax.broadcasted_iota(jnp.int32, (1, cols), 1) // WINDOW
    low = lax.broadcasted_iota(jnp.int32, (WINDOW, LANES), 1) % HEAD_DIM < HALF
    ones = jnp.ones((_ONES_ROWS, 2 * WINDOW), BF16)
    nt = (((1,), (1,)), ((), ()))

    def logits(r, g):
        kb = kband[r * WINDOW:(r + 2) * WINDOW, g * LANES:(g + 1) * LANES]
        qs = []
        for c in range(2):
            a = (2 * g + c) * LANES
            qc = q_ref[0, r * WINDOW:(r + 1) * WINDOW, a:a + LANES]
            qs.append(jnp.where(low, qc, jnp.zeros_like(qc)))
            qs.append(jnp.where(low, jnp.zeros_like(qc), qc))
        qst = jnp.concatenate(qs, axis=0)
        s = lax.dot_general(kb, qst, nt, preferred_element_type=F32)
        return s + bias_ref[first_plane if r == 0 else 0]

    def attend(r, g, s):
        sink = jnp.zeros((1, cols), F32)
        for k in range(nhq):
            sink = jnp.where(headid == k, sinks_ref[g * nhq + k] * LOG2E, sink)
        m = jnp.maximum(jnp.max(s, axis=0, keepdims=True), sink)
        p = jnp.exp2(s - m).astype(BF16)
        vt = vtband[g * HEAD_DIM:(g + 1) * HEAD_DIM, r * WINDOW:(r + 2) * WINDOW]
        pv = jnp.dot(jnp.concatenate([vt, ones], axis=0), p, preferred_element_type=F32)
        den = pv[HEAD_DIM:HEAD_DIM + 1] + jnp.exp2(sink - m)
        o_t = pv[0:HEAD_DIM] / den
        for c in range(2):
            a = (2 * g + c) * LANES
            pair = jnp.concatenate([o_t[:, 2 * c * WINDOW:(2 * c + 1) * WINDOW],
                                    o_t[:, (2 * c + 1) * WINDOW:(2 * c + 2) * WINDOW]], axis=0)
            gate = sz_ref[0, r * WINDOW:(r + 1) * WINDOW, a:a + LANES].astype(F32)
            o_ref[0, r * WINDOW:(r + 1) * WINDOW, a:a + LANES] = (pair.T * gate).astype(BF16)

    work = [(r, g) for r in range(nblk) for g in range(SWA_KV_HEADS)]
    pending = [logits(*w) for w in work[:_SWA_AHEAD]]
    for n, (r, g) in enumerate(work):
        if n + _SWA_AHEAD < len(work):
            pending.append(logits(*work[n + _SWA_AHEAD]))
        attend(r, g, pending.pop(0))


def _swa(sinks, qa, ka, vat, sza, *, tq):
    b, s, _ = qa.shape
    kvw = 2 * SWA_KV_WIDTH
    per = tq // WINDOW
    cur = lambda bi, i: (bi, i, 0)
    prev = lambda bi, i: (bi, jnp.maximum(i * per - 1, 0), 0)
    cur_t = lambda bi, i: (bi, 0, i)
    prev_t = lambda bi, i: (bi, 0, jnp.maximum(i * per - 1, 0))
    key = np.arange(2 * WINDOW)[:, None]
    qt = np.tile(np.arange(WINDOW), SWA_Q_HEADS // SWA_KV_HEADS)[None, :]
    band = np.where(key < WINDOW, key > qt, (key - WINDOW) <= qt)
    planes = np.stack([band, band & (key >= WINDOW)])
    bias = jnp.asarray(np.where(planes, 0.0, -np.inf).astype(np.float32))
    return pl.pallas_call(
        functools.partial(_swa_kernel, tq=tq),
        grid=(b, s // tq),
        in_specs=[pl.BlockSpec(memory_space=pltpu.SMEM),
                  pl.BlockSpec((1, tq, SWA_WIDTH), cur),
                  pl.BlockSpec((1, tq, kvw), cur),
                  pl.BlockSpec((1, WINDOW, kvw), prev),
                  pl.BlockSpec((1, SWA_KV_WIDTH, tq), cur_t),
                  pl.BlockSpec((1, SWA_KV_WIDTH, WINDOW), prev_t),
                  pl.BlockSpec(bias.shape, lambda bi, i: (0, 0, 0), pipeline_mode=pl.Buffered(1)),
                  pl.BlockSpec((1, tq, SWA_WIDTH), cur)],
        out_specs=pl.BlockSpec((1, tq, SWA_WIDTH), cur),
        out_shape=jax.ShapeDtypeStruct((b, s, SWA_WIDTH), BF16),
        scratch_shapes=[pltpu.VMEM((tq + WINDOW, kvw), BF16),
                        pltpu.VMEM((SWA_KV_WIDTH, tq + WINDOW), BF16)],
        compiler_params=pltpu.CompilerParams(dimension_semantics=("arbitrary", "arbitrary")),
        name="swa_attn",
    )(sinks, qa, ka, ka, vat, vat, bias, sza)


_ACC_ROWS = HEAD_DIM + _ONES_ROWS
_PIPE = 2
_FOX_UNROLL = 2


def _fox_schedule(n_rows):
    phases = [[(i, i, i) for i in range(n_rows)],
              [(i, j, i) for i in range(n_rows) for j in range(i)]]
    dummy = (0, 0, n_rows)
    tbl, spans = [], []
    for seq in phases:
        n_steps = len(seq) + _PIPE
        n_steps += -n_steps % _FOX_UNROLL
        spans.append((len(tbl), n_steps))
        tbl += [dummy] * _PIPE + seq + [dummy] * (n_steps - len(seq))
    return np.asarray(tbl, np.int32).T.copy(), spans


def _fox_kernel(tbl_ref, q_ref, k_ref, vt_ref, bias_ref, sz_ref, o_ref,
                s0, s1, mx0, mx1, p0, p1, a0, a1, m_ref, acc_ref, *, t, n_rows, spans):
    nt = (((1,), (1,)), ((), ()))
    g = t // SUBLANES
    @pl.when(jnp.logical_and(pl.program_id(0) == 0, pl.program_id(1) == 0))
    def _():
        for ref in (s0, s1, mx0, mx1, p0, p1, a0, a1):
            ref[...] = jnp.zeros_like(ref)
        m_ref[n_rows] = jnp.zeros_like(m_ref[n_rows])
        acc_ref[n_rows] = jnp.zeros_like(acc_ref[n_rows])

    def step(k, diagonal, s_w, mx_w, s_r, mx_r, p_w, a_w, p_r, a_r):
        qrow = pl.multiple_of(tbl_ref[0, k + 2] * t, t)
        krow = pl.multiple_of(tbl_ref[1, k + 2] * t, t)
        for hh in range(2):
            s = lax.dot_general(k_ref[0, hh, pl.ds(krow, t), :], q_ref[0, hh, pl.ds(qrow, t), :], nt,
                                preferred_element_type=F32)
            if diagonal:
                s = s + bias_ref[...]
            s_w[hh] = s
            mx_w[hh] = jnp.max(s.reshape(g, SUBLANES, t), axis=0)

        srow = tbl_ref[2, k + 1]
        for hh in range(2):
            m_tile = jnp.max(mx_r[hh], axis=0, keepdims=True)
            if diagonal:
                m_new = jnp.broadcast_to(m_tile, (SUBLANES, t))
            else:
                m_old = m_ref[srow, hh]
                m_new = jnp.maximum(m_old, m_tile)
                a_w[hh] = jnp.exp2(m_old - m_new)
            m_ref[srow, hh] = m_new
            p = jnp.exp2(s_r[hh].reshape(g, SUBLANES, t) - m_new[None])
            p_w[hh] = p.reshape(t, t).astype(BF16)

        vcol = pl.multiple_of(tbl_ref[1, k] * t, t)
        arow = tbl_ref[2, k]
        for hh in range(2):
            vt = vt_ref[0, hh * HEAD_DIM:(hh + 1) * HEAD_DIM, pl.ds(vcol, t)]
            lhs = jnp.concatenate([vt, jnp.ones((_ONES_ROWS, t), BF16)], axis=0)
            part = jnp.dot(lhs, p_r[hh], preferred_element_type=F32)
            if diagonal:
                acc_ref[arow, hh] = part
            else:
                old = acc_ref[arow, hh].reshape(_ACC_ROWS // SUBLANES, SUBLANES, t)
                acc_ref[arow, hh] = (a_r[hh][None] * old).reshape(_ACC_ROWS, t) + part

    assert _FOX_UNROLL % 2 == 0
    for diagonal, (first, n_steps) in zip((True, False), spans):
        def body(kk, carry, diagonal=diagonal, first=first):
            k = first + _FOX_UNROLL * kk
            for u in range(0, _FOX_UNROLL, 2):
                step(k + u, diagonal, s0, mx0, s1, mx1, p1, a1, p0, a0)
                step(k + u + 1, diagonal, s1, mx1, s0, mx0, p0, a0, p1, a1)
            return carry

        lax.fori_loop(0, n_steps // _FOX_UNROLL, body, 0)

    def finish_row(i, carry):
        halves = []
        for hh in range(2):
            a = acc_ref[i, hh]
            halves.append(a[0:HEAD_DIM] / a[HEAD_DIM:HEAD_DIM + 1])
        o_t = jnp.concatenate(halves, axis=0)
        rows = pl.ds(pl.multiple_of(i * t, t), t)
        o_ref[0, rows, :] = (o_t.T * sz_ref[0, rows, :].astype(F32)).astype(BF16)
        return carry

    lax.fori_loop(0, n_rows, finish_row, 0)


def _fox(q_aug, k_aug, vbt, szb, *, t):
    b, _, s, _ = q_aug.shape
    npair = FOX_HEADS // 2
    n_rows = s // t
    tbl, spans = _fox_schedule(n_rows)
    tri = np.triu(np.ones((t, t), bool))
    bias = jnp.asarray(np.where(tri, 0.0, -np.inf).astype(np.float32))
    pair = lambda bi, p: (bi, p, 0, 0)
    stat = pltpu.VMEM((2, SUBLANES, t), F32)
    return pl.pallas_call(
        functools.partial(_fox_kernel, t=t, n_rows=n_rows, spans=tuple(spans)),
        grid=(b, npair),
        in_specs=[pl.BlockSpec(memory_space=pltpu.SMEM),
                  pl.BlockSpec((1, 2, s, LANES), pair),
                  pl.BlockSpec((1, 2, s, LANES), pair),
                  pl.BlockSpec((1, 2 * HEAD_DIM, s), lambda bi, p: (bi, p, 0)),
                  pl.BlockSpec((t, t), lambda bi, p: (0, 0), pipeline_mode=pl.Buffered(1)),
                  pl.BlockSpec((1, s, LANES), lambda bi, p: (bi, 0, p))],
        out_specs=pl.BlockSpec((1, s, LANES), lambda bi, p: (bi, 0, p)),
        out_shape=jax.ShapeDtypeStruct((b, s, FOX_WIDTH), BF16),
        scratch_shapes=[pltpu.VMEM((2, t, t), F32), pltpu.VMEM((2, t, t), F32),
                        stat, stat,
                        pltpu.VMEM((2, t, t), BF16), pltpu.VMEM((2, t, t), BF16),
                        stat, stat,
                        pltpu.VMEM((n_rows + 1, 2, SUBLANES, t), F32),
                        pltpu.VMEM((n_rows + 1, 2, _ACC_ROWS, t), F32)],
        compiler_params=pltpu.CompilerParams(
            dimension_semantics=("arbitrary", "arbitrary"),
            vmem_limit_bytes=_FOX_VMEM),
        name="fox_attn",
    )(jnp.asarray(tbl), q_aug, k_aug, vbt, bias, szb)


def _out_kernel(x_ref, ga_ref, gb_ref, sga_ref, sgb_ref, gate_ref,
                woa_ref, wob_ref, wout_ref, gf_ref, o_ref):
    ya = jnp.dot(ga_ref[0], woa_ref[...], preferred_element_type=F32)
    yb = jnp.dot(gb_ref[0], wob_ref[...], preferred_element_type=F32)
    merged = sga_ref[0].astype(F32) * ya + sgb_ref[0].astype(F32) * yb
    d = jnp.dot(merged.astype(BF16), wout_ref[...], preferred_element_type=F32)
    xo = x_ref[0] + gate_ref[0] * d
    ms = jnp.mean(xo * xo, axis=-1, keepdims=True)
    o_ref[0] = xo * lax.rsqrt(ms + NORM_EPS) * gf_ref[...]


def _out(x, gated_a, gated_b, sga, sgb, gate, woa, wob, wout, gf, *, tm):
    b, s, _ = x.shape
    row = lambda bi, i: (bi, i, 0)
    per_b = lambda bi, i: (bi, 0, 0)
    const2 = lambda bi, i: (0, 0)
    once = pl.Buffered(1)
    return pl.pallas_call(
        _out_kernel,
        grid=(b, s // tm),
        in_specs=[pl.BlockSpec((1, tm, D_MODEL), row),
                  pl.BlockSpec((1, tm, SWA_WIDTH), row), pl.BlockSpec((1, tm, FOX_WIDTH), row),
                  pl.BlockSpec((1, tm, D_MODEL), row), pl.BlockSpec((1, tm, D_MODEL), row),
                  pl.BlockSpec((1, 1, D_MODEL), per_b),
                  pl.BlockSpec((SWA_WIDTH, D_MODEL), const2, pipeline_mode=once),
                  pl.BlockSpec((FOX_WIDTH, D_MODEL), const2, pipeline_mode=once),
                  pl.BlockSpec((D_MODEL, D_MODEL), const2, pipeline_mode=once),
                  pl.BlockSpec((1, D_MODEL), const2)],
        out_specs=pl.BlockSpec((1, tm, D_MODEL), row),
        out_shape=jax.ShapeDtypeStruct((b, s, D_MODEL), F32),
        compiler_params=pltpu.CompilerParams(
            dimension_semantics=("arbitrary", "arbitrary"),
            vmem_limit_bytes=_OUT_VMEM),
        name="out_proj",
    )(x, gated_a, gated_b, sga, sgb, gate, woa, wob, wout, gf)


def _pair_halves(w, nheads, dup):
    rows = w.shape[0]
    if dup:
        w5 = jnp.broadcast_to(w.reshape(rows, nheads, 1, 2, HALF), (rows, nheads, 2, 2, HALF))
    else:
        w5 = w.reshape(rows, nheads // 2, 2, 2, HALF)
    return w5.transpose(0, 1, 3, 2, 4).reshape(rows, -1)


def _layout_w_in(w):
    o = np.cumsum([0, SWA_WIDTH, SWA_KV_WIDTH, SWA_KV_WIDTH, SWA_WIDTH, FOX_WIDTH, FOX_WIDTH, FOX_WIDTH,
                   FOX_HEADS, FOX_WIDTH, D_MODEL, D_MODEL]).tolist()
    qa, ka, va, za, qb, kb, vb, fb, zb, ga, gb = [w[:, o[k]:o[k + 1]] for k in range(11)]
    cols = [_pair_halves(qa, SWA_Q_HEADS, False), _pair_halves(ka, SWA_KV_HEADS, True),
            za, qb, kb, jnp.tile(fb, (1, LANES // FOX_HEADS)), zb, ga, gb]
    w_all = jnp.concatenate(cols, axis=1).astype(BF16)
    assert w_all.shape[1] == _PROJ_COLS
    return w_all, jnp.concatenate([va, vb], axis=1).T.astype(BF16)


def kernel(x, c, positions, w_ada, b_ada, g_norm, w_in, b_f, sinks, w_o_swa, w_o_fox, w_out, g_final):
    b, s, _ = x.shape
    depth = w_in.shape[0]
    assert depth == 1, "the output stage fuses the final RMSNorm into the single layer"
    inv_freq = ROPE_THETA ** (-jnp.arange(0, HEAD_DIM, 2, dtype=F32) / HEAD_DIM)
    invf = jnp.broadcast_to(inv_freq[:, None], (HALF, _PROJ_TM))
    pos_f = positions.astype(F32)[:, None, :]
    e_mat = jnp.asarray(_aug_placement(), BF16)
    c_pad = jnp.zeros((SUBLANES, D_MODEL), F32).at[:b].set(c)
    for l in range(depth):
        ada = _ada(c_pad, w_ada[l], b_ada[l][None, :])[:b]
        shift, scale, gate = [ada[:, None, k * D_MODEL:(k + 1) * D_MODEL] for k in range(3)]
        w_all, w_vt = _layout_w_in(w_in[l])
        bf_rep = jnp.tile(b_f[l].astype(F32), LANES // FOX_HEADS)[None, :]
        qa, ka, vat, sza, q_aug, k_aug, vbt, szb, sga, sgb = _proj(
            x, pos_f, scale, shift, g_norm[l][None, :], invf, bf_rep, w_all, w_vt, e_mat, tm=_PROJ_TM)
        gated_a = _swa(sinks[l].astype(F32), qa, ka, vat, sza, tq=_SWA_TQ)
        gated_b = _fox(q_aug, k_aug, vbt, szb, t=_FOX_T)
        x = _out(x, gated_a, gated_b, sga, sgb, gate,
                 w_o_swa[l].astype(BF16), w_o_fox[l].astype(BF16), w_out[l].astype(BF16),
                 g_final[None, :], tm=_OUT_TM)
    return x
```

```python
import functools
import math

import numpy as np
import jax
import jax.numpy as jnp
from jax import lax
from jax.experimental import pallas as pl
from jax.experimental.pallas import tpu as pltpu

D_MODEL = 1024
HEAD_DIM = 64
HALF = HEAD_DIM // 2
SWA_Q_HEADS = 8
SWA_KV_HEADS = 2
SWA_WIDTH = SWA_Q_HEADS * HEAD_DIM
SWA_KV_WIDTH = SWA_KV_HEADS * HEAD_DIM
FOX_HEADS = 8
FOX_WIDTH = FOX_HEADS * HEAD_DIM
WINDOW = 128
ROPE_THETA = 10000.0
NORM_EPS = 1e-6
QK_SCALE = HEAD_DIM ** -0.5
LOG2E = math.log2(math.e)

LANES = 128
SUBLANES = 8
MIB = 1024 * 1024
V7X_VMEM_BYTES = 64 * MIB

_PROJ_TM = 512
_SWA_TQ = 1024
_FOX_T = 512
_OUT_TM = 1024

F32 = jnp.float32
BF16 = jnp.bfloat16

_QA, _KA, _ZA = 0, 512, 768
_QB, _KB, _FB, _ZB, _GA, _GB = 1280, 1792, 2304, 2432, 2944, 3968
_PROJ_COLS = 4992

_ONE_LANE = 24

_SWA_AHEAD = 2
_ONES_ROWS = 16


_PROJ_VMEM = 56 * MIB
_FOX_VMEM = 56 * MIB
_OUT_VMEM = 48 * MIB
assert max(_PROJ_VMEM, _FOX_VMEM, _OUT_VMEM) <= V7X_VMEM_BYTES - 8 * MIB


def _split3(v):
    hi = v.astype(BF16)
    r1 = v - hi.astype(F32)
    mid = r1.astype(BF16)
    lo = (r1 - mid.astype(F32)).astype(BF16)
    return hi, mid, lo


def _ada_kernel(c_ref, w_ref, b_ref, o_ref):
    o_ref[...] = jnp.dot(c_ref[...], w_ref[...], preferred_element_type=F32,
                         precision=lax.Precision.HIGHEST) + b_ref[...]


def _ada(c_pad, w_ada, b_ada):
    rows = c_pad.shape[0]
    n = w_ada.shape[1]
    nblk = n // D_MODEL
    return pl.pallas_call(
        _ada_kernel,
        grid=(nblk,),
        in_specs=[pl.BlockSpec((rows, D_MODEL), lambda j: (0, 0)),
                  pl.BlockSpec((D_MODEL, D_MODEL), lambda j: (0, j)),
                  pl.BlockSpec((1, D_MODEL), lambda j: (0, j))],
        out_specs=pl.BlockSpec((rows, D_MODEL), lambda j: (0, j)),
        out_shape=jax.ShapeDtypeStruct((rows, n), F32),
        name="ada_mod",
    )(c_pad, w_ada, b_ada)


def _sigmoid(z):
    return 0.5 * jnp.tanh(0.5 * z) + 0.5


def _proj_kernel(x_ref, pos_ref, scale_ref, shift_ref, gn_ref, invf_ref, bf_ref, w_ref, wvt_ref, e_ref,
                 qa_ref, ka_ref, vat_ref, sza_ref, qaug_ref, kaug_ref, vbt_ref, szb_ref,
                 sga_ref, sgb_ref, carry_ref, *, tm):
    i = pl.program_id(1)

    @pl.when(i == 0)
    def _():
        carry_ref[...] = jnp.zeros_like(carry_ref)

    x = x_ref[0]
    ms = jnp.mean(x * x, axis=-1, keepdims=True)
    y = x * lax.rsqrt(ms + NORM_EPS) * gn_ref[...]
    h = y * (1.0 + scale_ref[0]) + shift_ref[0]
    hb = h.astype(BF16)

    def proj(off, n):
        return jnp.dot(hb, w_ref[:, off:off + n], preferred_element_type=F32)

    lane = lax.broadcasted_iota(jnp.int32, (tm, LANES), 1)
    low = lane < HEAD_DIM


    zf = proj(_FB, LANES) + bf_ref[...]
    logf = jnp.minimum(zf, 0.0) - jnp.log1p(jnp.exp(-jnp.abs(zf)))
    grp = lane // FOX_HEADS
    hi, mid, lo = _split3(logf)
    zero = jnp.zeros((tm, LANES), F32)
    parts = jnp.where(grp == 0, hi.astype(F32),
                      jnp.where(grp == 1, mid.astype(F32),
                                jnp.where(grp == 2, lo.astype(F32), zero))).astype(BF16)
    tri = (lax.broadcasted_iota(jnp.int32, (tm, tm), 0)
           >= lax.broadcasted_iota(jnp.int32, (tm, tm), 1)).astype(BF16)
    sga_ref[0] = _sigmoid(proj(_GA, D_MODEL)).astype(BF16)
    rsum = jnp.dot(tri, parts, preferred_element_type=F32)
    c0 = rsum + pltpu.roll(rsum, LANES - FOX_HEADS, 1) + pltpu.roll(rsum, LANES - 2 * FOX_HEADS, 1)
    c0 = jnp.where(grp == 0, c0, zero)
    cl = c0 + pltpu.roll(c0, FOX_HEADS, 1) + pltpu.roll(c0, 2 * FOX_HEADS, 1)
    cum = cl + carry_ref[...]
    carry_ref[...] = cum[tm - 1:tm, :]

    hi, mid, lo = _split3(cum * LOG2E)
    one = jnp.ones((tm, LANES), F32)
    carrier = jnp.where(grp == 0, hi.astype(F32),
                        jnp.where(grp == 1, mid.astype(F32),
                                  jnp.where(grp == 2, lo.astype(F32),
                                            jnp.where(lane == _ONE_LANE, one, zero)))).astype(BF16)
    sgb_ref[0] = _sigmoid(proj(_GB, D_MODEL)).astype(BF16)
    aug = jnp.dot(carrier, e_ref[...], preferred_element_type=F32)

    z = proj(_ZA, SWA_WIDTH)
    sza_ref[0] = (z * _sigmoid(z)).astype(BF16)
    z = proj(_ZB, FOX_WIDTH)
    szb_ref[0] = (z * _sigmoid(z)).astype(BF16)

    ang = invf_ref[...] * pos_ref[0]
    cos = jnp.tile(jnp.cos(ang), (LANES // HALF, 1)).T
    sn = jnp.sin(ang)
    sin = jnp.concatenate([-sn, -sn, sn, sn], axis=0).T
    r = proj(_QA, SWA_WIDTH)
    for c in range(SWA_WIDTH // LANES):
        rc = r[:, c * LANES:(c + 1) * LANES]
        t = rc * cos + pltpu.roll(rc, HEAD_DIM, 1) * sin
        qa_ref[0, :, c * LANES:(c + 1) * LANES] = (t * (QK_SCALE * LOG2E)).astype(BF16)
    r = proj(_KA, 2 * SWA_KV_WIDTH)
    for c in range(2 * SWA_KV_WIDTH // LANES):
        rc = r[:, c * LANES:(c + 1) * LANES]
        t = rc * cos + pltpu.roll(rc, HEAD_DIM, 1) * sin
        ka_ref[0, :, c * LANES:(c + 1) * LANES] = t.astype(BF16)

    rq = proj(_QB, FOX_WIDTH)
    rk = proj(_KB, FOX_WIDTH)
    for hd in range(FOX_HEADS):
        a = (hd // 2) * LANES
        gq = aug[:, a:a + LANES]
        gk = aug[:, FOX_WIDTH + a:FOX_WIDTH + a + LANES]
        qv = rq[:, a:a + LANES] * (QK_SCALE * LOG2E)
        kv = rk[:, a:a + LANES]
        if hd % 2 == 0:
            qaug_ref[0, hd] = jnp.where(low, qv, gq).astype(BF16)
            kaug_ref[0, hd] = jnp.where(low, kv, gk).astype(BF16)
        else:
            qaug_ref[0, hd] = jnp.where(low, gq, qv).astype(BF16)
            kaug_ref[0, hd] = jnp.where(low, gk, kv).astype(BF16)

    vt = lax.dot_general(wvt_ref[...], hb, (((1,), (1,)), ((), ())), preferred_element_type=F32)
    vat_ref[0] = vt[0:SWA_KV_WIDTH].astype(BF16)
    vbt_ref[0] = vt[SWA_KV_WIDTH:].astype(BF16)


def _aug_placement():
    e = np.zeros((LANES, 2 * FOX_HEADS * HEAD_DIM), np.float32)
    for hd in range(FOX_HEADS):
        base = (hd // 2) * LANES + (HEAD_DIM if hd % 2 == 0 else 0)
        kbase = FOX_HEADS * HEAD_DIM + base
        for part in range(3):
            e[part * FOX_HEADS + hd, base + part] = 1.0
            e[_ONE_LANE, base + 3 + part] = 1.0
            e[_ONE_LANE, kbase + part] = 1.0
            e[part * FOX_HEADS + hd, kbase + 3 + part] = -1.0
    return e


def _proj(x, pos_f, scale, shift, gn, invf, bf_rep, w_all, w_vt, e_mat, *, tm):
    b, s, _ = x.shape
    grid = (b, s // tm)
    row = lambda bi, i: (bi, i, 0)
    per_b = lambda bi, i: (bi, 0, 0)
    const2 = lambda bi, i: (0, 0)
    once = pl.Buffered(1)
    in_specs = [
        pl.BlockSpec((1, tm, D_MODEL), row),
        pl.BlockSpec((1, 1, tm), lambda bi, i: (bi, 0, i)),
        pl.BlockSpec((1, 1, D_MODEL), per_b),
        pl.BlockSpec((1, 1, D_MODEL), per_b),
        pl.BlockSpec((1, D_MODEL), const2),
        pl.BlockSpec((HALF, tm), const2),
        pl.BlockSpec((1, LANES), const2),
        pl.BlockSpec((D_MODEL, _PROJ_COLS), const2, pipeline_mode=once),
        pl.BlockSpec((SWA_KV_WIDTH + FOX_WIDTH, D_MODEL), const2, pipeline_mode=once),
        pl.BlockSpec((LANES, 2 * FOX_WIDTH), const2, pipeline_mode=once),
    ]
    head4 = lambda bi, i: (bi, 0, i, 0)
    out_specs = [
        pl.BlockSpec((1, tm, SWA_WIDTH), row),
        pl.BlockSpec((1, tm, 2 * SWA_KV_WIDTH), row),
        pl.BlockSpec((1, SWA_KV_WIDTH, tm), lambda bi, i: (bi, 0, i)),
        pl.BlockSpec((1, tm, SWA_WIDTH), row),
        pl.BlockSpec((1, FOX_HEADS, tm, LANES), head4),
        pl.BlockSpec((1, FOX_HEADS, tm, LANES), head4),
        pl.BlockSpec((1, FOX_WIDTH, tm), lambda bi, i: (bi, 0, i)),
        pl.BlockSpec((1, tm, FOX_WIDTH), row),
        pl.BlockSpec((1, tm, D_MODEL), row),
        pl.BlockSpec((1, tm, D_MODEL), row),
    ]
    sds = jax.ShapeDtypeStruct
    out_shape = [
        sds((b, s, SWA_WIDTH), BF16), sds((b, s, 2 * SWA_KV_WIDTH), BF16),
        sds((b, SWA_KV_WIDTH, s), BF16), sds((b, s, SWA_WIDTH), BF16),
        sds((b, FOX_HEADS, s, LANES), BF16), sds((b, FOX_HEADS, s, LANES), BF16),
        sds((b, FOX_WIDTH, s), BF16), sds((b, s, FOX_WIDTH), BF16),
        sds((b, s, D_MODEL), BF16), sds((b, s, D_MODEL), BF16),
    ]
    return pl.pallas_call(
        functools.partial(_proj_kernel, tm=tm),
        grid=grid, in_specs=in_specs, out_specs=out_specs, out_shape=out_shape,
        scratch_shapes=[pltpu.VMEM((1, LANES), F32)],
        compiler_params=pltpu.CompilerParams(
            dimension_semantics=("arbitrary", "arbitrary"),
            vmem_limit_bytes=_PROJ_VMEM),
        name="in_proj",
    )(x, pos_f, scale, shift, gn, invf, bf_rep, w_all, w_vt, e_mat)


def _swa_kernel(sinks_ref, q_ref, kc_ref, kp_ref, vtc_ref, vtp_ref, bias_ref, sz_ref, o_ref,
                kband, vtband, *, tq):
    i = pl.program_id(1)
    nblk = tq // WINDOW
    nhq = SWA_Q_HEADS // SWA_KV_HEADS
    kband[0:WINDOW] = kp_ref[0]
    kband[WINDOW:] = kc_ref[0]
    vtband[:, 0:WINDOW] = vtp_ref[0]
    vtband[:, WINDOW:] = vtc_ref[0]

    cols = nhq * WINDOW
    first_plane = jnp.where(i > 0, 0, 1)
    headid = lax.broadcasted_iota(jnp.int32, (1, cols), 1) // WINDOW
    low = lax.broadcasted_iota(jnp.int32, (WINDOW, LANES), 1) % HEAD_DIM < HALF
    ones = jnp.ones((_ONES_ROWS, 2 * WINDOW), BF16)
    nt = (((1,), (1,)), ((), ()))

    def logits(r, g):
        kb = kband[r * WINDOW:(r + 2) * WINDOW, g * LANES:(g + 1) * LANES]
        qs = []
        for c in range(2):
            a = (2 * g + c) * LANES
            qc = q_ref[0, r * WINDOW:(r + 1) * WINDOW, a:a + LANES]
            qs.append(jnp.where(low, qc, jnp.zeros_like(qc)))
            qs.append(jnp.where(low, jnp.zeros_like(qc), qc))
        qst = jnp.concatenate(qs, axis=0)
        s = lax.dot_general(kb, qst, nt, preferred_element_type=F32)
        return s + bias_ref[first_plane if r == 0 else 0]

    def attend(r, g, s):
        sink = jnp.zeros((1, cols), F32)
        for k in range(nhq):
            sink = jnp.where(headid == k, sinks_ref[g * nhq + k] * LOG2E, sink)
        m = jnp.maximum(jnp.max(s, axis=0, keepdims=True), sink)
        p = jnp.exp2(s - m).astype(BF16)
        vt = vtband[g * HEAD_DIM:(g + 1) * HEAD_DIM, r * WINDOW:(r + 2) * WINDOW]
        pv = jnp.dot(jnp.concatenate([vt, ones], axis=0), p, preferred_element_type=F32)
        den = pv[HEAD_DIM:HEAD_DIM + 1] + jnp.exp2(sink - m)
        o_t = pv[0:HEAD_DIM] / den
        for c in range(2):
            a = (2 * g + c) * LANES
            pair = jnp.concatenate([o_t[:, 2 * c * WINDOW:(2 * c + 1) * WINDOW],
                                    o_t[:, (2 * c + 1) * WINDOW:(2 * c + 2) * WINDOW]], axis=0)
            gate = sz_ref[0, r * WINDOW:(r + 1) * WINDOW, a:a + LANES].astype(F32)
            o_ref[0, r * WINDOW:(r + 1) * WINDOW, a:a + LANES] = (pair.T * gate).astype(BF16)

    work = [(r, g) for r in range(nblk) for g in range(SWA_KV_HEADS)]
    pending = [logits(*w) for w in work[:_SWA_AHEAD]]
    for n, (r, g) in enumerate(work):
        if n + _SWA_AHEAD < len(work):
            pending.append(logits(*work[n + _SWA_AHEAD]))
        attend(r, g, pending.pop(0))


def _swa(sinks, qa, ka, vat, sza, *, tq):
    b, s, _ = qa.shape
    kvw = 2 * SWA_KV_WIDTH
    per = tq // WINDOW
    cur = lambda bi, i: (bi, i, 0)
    prev = lambda bi, i: (bi, jnp.maximum(i * per - 1, 0), 0)
    cur_t = lambda bi, i: (bi, 0, i)
    prev_t = lambda bi, i: (bi, 0, jnp.maximum(i * per - 1, 0))
    key = np.arange(2 * WINDOW)[:, None]
    qt = np.tile(np.arange(WINDOW), SWA_Q_HEADS // SWA_KV_HEADS)[None, :]
    band = np.where(key < WINDOW, key > qt, (key - WINDOW) <= qt)
    planes = np.stack([band, band & (key >= WINDOW)])
    bias = jnp.asarray(np.where(planes, 0.0, -np.inf).astype(np.float32))
    return pl.pallas_call(
        functools.partial(_swa_kernel, tq=tq),
        grid=(b, s // tq),
        in_specs=[pl.BlockSpec(memory_space=pltpu.SMEM),
                  pl.BlockSpec((1, tq, SWA_WIDTH), cur),
                  pl.BlockSpec((1, tq, kvw), cur),
                  pl.BlockSpec((1, WINDOW, kvw), prev),
                  pl.BlockSpec((1, SWA_KV_WIDTH, tq), cur_t),
                  pl.BlockSpec((1, SWA_KV_WIDTH, WINDOW), prev_t),
                  pl.BlockSpec(bias.shape, lambda bi, i: (0, 0, 0), pipeline_mode=pl.Buffered(1)),
                  pl.BlockSpec((1, tq, SWA_WIDTH), cur)],
        out_specs=pl.BlockSpec((1, tq, SWA_WIDTH), cur),
        out_shape=jax.ShapeDtypeStruct((b, s, SWA_WIDTH), BF16),
        scratch_shapes=[pltpu.VMEM((tq + WINDOW, kvw), BF16),
                        pltpu.VMEM((SWA_KV_WIDTH, tq + WINDOW), BF16)],
        compiler_params=pltpu.CompilerParams(dimension_semantics=("arbitrary", "arbitrary")),
        name="swa_attn",
    )(sinks, qa, ka, ka, vat, vat, bias, sza)


_ACC_ROWS = HEAD_DIM + _ONES_ROWS


def _fox_schedule(n_rows):
    tiles = [(i, i) for i in range(n_rows)] + [(i, j) for i in range(n_rows) for j in range(i)]
    return np.asarray(tiles, np.int32).T.copy()


def _fox_kernel(tbl_ref, q_ref, k_ref, vt_ref, bias_ref, sz_ref, o_ref,
                s0, s1, mx0, mx1, p0, p1, a0, a1, m_ref, acc_ref, *, t, n_rows):
    nt = (((1,), (1,)), ((), ()))
    g = t // SUBLANES
    n_diag = n_rows
    n_tiles = n_rows * (n_rows + 1) // 2
    assert n_diag % 2 == 0 and n_tiles % 2 == 0 and n_diag >= 2 and n_tiles - n_diag >= 2

    def step(j, diag, s_w, mx_w, s_r, mx_r, p_w, a_w, p_r, a_r):
        if diag[0] is not None:
            qrow = pl.multiple_of(tbl_ref[0, j] * t, t)
            krow = pl.multiple_of(tbl_ref[1, j] * t, t)
            for hh in range(2):
                s = lax.dot_general(k_ref[0, hh, pl.ds(krow, t), :], q_ref[0, hh, pl.ds(qrow, t), :], nt,
                                    preferred_element_type=F32)
                if diag[0]:
                    s = s + bias_ref[...]
                s_w[hh] = s
                mx_w[hh] = jnp.max(s.reshape(g, SUBLANES, t), axis=0)

        if diag[1] is not None:
            srow = tbl_ref[0, j - 1]
            for hh in range(2):
                m_tile = jnp.max(mx_r[hh], axis=0, keepdims=True)
                if diag[1]:
                    m_new = jnp.broadcast_to(m_tile, (SUBLANES, t))
                else:
                    m_old = m_ref[srow, hh]
                    m_new = jnp.maximum(m_old, m_tile)
                    a_w[hh] = jnp.exp2(m_old - m_new)
                m_ref[srow, hh] = m_new
                p = jnp.exp2(s_r[hh].reshape(g, SUBLANES, t) - m_new[None])
                p_w[hh] = p.reshape(t, t).astype(BF16)

        if diag[2] is not None:
            vcol = pl.multiple_of(tbl_ref[1, j - 2] * t, t)
            arow = tbl_ref[0, j - 2]
            for hh in range(2):
                vt = vt_ref[0, hh * HEAD_DIM:(hh + 1) * HEAD_DIM, pl.ds(vcol, t)]
                lhs = jnp.concatenate([vt, jnp.ones((_ONES_ROWS, t), BF16)], axis=0)
                part = jnp.dot(lhs, p_r[hh], preferred_element_type=F32)
                if diag[2]:
                    acc_ref[arow, hh] = part
                else:
                    old = acc_ref[arow, hh].reshape(_ACC_ROWS // SUBLANES, SUBLANES, t)
                    acc_ref[arow, hh] = (a_r[hh][None] * old).reshape(_ACC_ROWS, t) + part

    def step_pair(j, diag_even, diag_odd):
        step(j, diag_even, s0, mx0, s1, mx1, p1, a1, p0, a0)
        step(j + 1, diag_odd, s1, mx1, s0, mx0, p0, a0, p1, a1)

    def steady(first, last, diag):
        def body(kk, carry):
            step_pair(first + 2 * kk, diag, diag)
            return carry

        lax.fori_loop(0, (last - first) // 2, body, 0)

    step_pair(0, (True, None, None), (True, True, None))
    steady(2, n_diag, (True, True, True))
    step_pair(n_diag, (False, True, True), (False, False, True))
    steady(n_diag + 2, n_tiles, (False, False, False))
    step_pair(n_tiles, (None, False, False), (None, None, False))

    def finish_row(i, carry):
        halves = []
        for hh in range(2):
            a = acc_ref[i, hh]
            halves.append(a[0:HEAD_DIM] / a[HEAD_DIM:HEAD_DIM + 1])
        o_t = jnp.concatenate(halves, axis=0)
        rows = pl.ds(pl.multiple_of(i * t, t), t)
        o_ref[0, rows, :] = (o_t.T * sz_ref[0, rows, :].astype(F32)).astype(BF16)
        return carry

    lax.fori_loop(0, n_rows, finish_row, 0)


def _fox(q_aug, k_aug, vbt, szb, *, t):
    b, _, s, _ = q_aug.shape
    npair = FOX_HEADS // 2
    n_rows = s // t
    tbl = _fox_schedule(n_rows)
    tri = np.triu(np.ones((t, t), bool))
    bias = jnp.asarray(np.where(tri, 0.0, -np.inf).astype(np.float32))
    pair = lambda bi, p: (bi, p, 0, 0)
    stat = pltpu.VMEM((2, SUBLANES, t), F32)
    return pl.pallas_call(
        functools.partial(_fox_kernel, t=t, n_rows=n_rows),
        grid=(b, npair),
        in_specs=[pl.BlockSpec(memory_space=pltpu.SMEM),
                  pl.BlockSpec((1, 2, s, LANES), pair),
                  pl.BlockSpec((1, 2, s, LANES), pair),
                  pl.BlockSpec((1, 2 * HEAD_DIM, s), lambda bi, p: (bi, p, 0)),
                  pl.BlockSpec((t, t), lambda bi, p: (0, 0), pipeline_mode=pl.Buffered(1)),
                  pl.BlockSpec((1, s, LANES), lambda bi, p: (bi, 0, p))],
        out_specs=pl.BlockSpec((1, s, LANES), lambda bi, p: (bi, 0, p)),
        out_shape=jax.ShapeDtypeStruct((b, s, FOX_WIDTH), BF16),
        scratch_shapes=[pltpu.VMEM((2, t, t), F32), pltpu.VMEM((2, t, t), F32),
                        stat, stat,
                        pltpu.VMEM((2, t, t), BF16), pltpu.VMEM((2, t, t), BF16),
                        stat, stat,
                        pltpu.VMEM((n_rows, 2, SUBLANES, t), F32),
                        pltpu.VMEM((n_rows, 2, _ACC_ROWS, t), F32)],
        compiler_params=pltpu.CompilerParams(
            dimension_semantics=("arbitrary", "arbitrary"),
            vmem_limit_bytes=_FOX_VMEM),
        name="fox_attn",
    )(jnp.asarray(tbl), q_aug, k_aug, vbt, bias, szb)


def _out_kernel(x_ref, ga_ref, gb_ref, sga_ref, sgb_ref, gate_ref,
                woa_ref, wob_ref, wout_ref, gf_ref, o_ref):
    ya = jnp.dot(ga_ref[0], woa_ref[...], preferred_element_type=F32)
    yb = jnp.dot(gb_ref[0], wob_ref[...], preferred_element_type=F32)
    merged = sga_ref[0].astype(F32) * ya + sgb_ref[0].astype(F32) * yb
    d = jnp.dot(merged.astype(BF16), wout_ref[...], preferred_element_type=F32)
    xo = x_ref[0] + gate_ref[0] * d
    ms = jnp.mean(xo * xo, axis=-1, keepdims=True)
    o_ref[0] = xo * lax.rsqrt(ms + NORM_EPS) * gf_ref[...]


def _out(x, gated_a, gated_b, sga, sgb, gate, woa, wob, wout, gf, *, tm):
    b, s, _ = x.shape
    row = lambda bi, i: (bi, i, 0)
    per_b = lambda bi, i: (bi, 0, 0)
    const2 = lambda bi, i: (0, 0)
    once = pl.Buffered(1)
    return pl.pallas_call(
        _out_kernel,
        grid=(b, s // tm),
        in_specs=[pl.BlockSpec((1, tm, D_MODEL), row),
                  pl.BlockSpec((1, tm, SWA_WIDTH), row), pl.BlockSpec((1, tm, FOX_WIDTH), row),
                  pl.BlockSpec((1, tm, D_MODEL), row), pl.BlockSpec((1, tm, D_MODEL), row),
                  pl.BlockSpec((1, 1, D_MODEL), per_b),
                  pl.BlockSpec((SWA_WIDTH, D_MODEL), const2, pipeline_mode=once),
                  pl.BlockSpec((FOX_WIDTH, D_MODEL), const2, pipeline_mode=once),
                  pl.BlockSpec((D_MODEL, D_MODEL), const2, pipeline_mode=once),
                  pl.BlockSpec((1, D_MODEL), const2)],
        out_specs=pl.BlockSpec((1, tm, D_MODEL), row),
        out_shape=jax.ShapeDtypeStruct((b, s, D_MODEL), F32),
        compiler_params=pltpu.CompilerParams(
            dimension_semantics=("arbitrary", "arbitrary"),
            vmem_limit_bytes=_OUT_VMEM),
        name="out_proj",
    )(x, gated_a, gated_b, sga, sgb, gate, woa, wob, wout, gf)


def _pair_halves(w, nheads, dup):
    rows = w.shape[0]
    if dup:
        w5 = jnp.broadcast_to(w.reshape(rows, nheads, 1, 2, HALF), (rows, nheads, 2, 2, HALF))
    else:
        w5 = w.reshape(rows, nheads // 2, 2, 2, HALF)
    return w5.transpose(0, 1, 3, 2, 4).reshape(rows, -1)


def _layout_w_in(w):
    o = np.cumsum([0, SWA_WIDTH, SWA_KV_WIDTH, SWA_KV_WIDTH, SWA_WIDTH, FOX_WIDTH, FOX_WIDTH, FOX_WIDTH,
                   FOX_HEADS, FOX_WIDTH, D_MODEL, D_MODEL]).tolist()
    qa, ka, va, za, qb, kb, vb, fb, zb, ga, gb = [w[:, o[k]:o[k + 1]] for k in range(11)]
    cols = [_pair_halves(qa, SWA_Q_HEADS, False), _pair_halves(ka, SWA_KV_HEADS, True),
            za, qb, kb, jnp.tile(fb, (1, LANES // FOX_HEADS)), zb, ga, gb]
    w_all = jnp.concatenate(cols, axis=1).astype(BF16)
    assert w_all.shape[1] == _PROJ_COLS
    return w_all, jnp.concatenate([va, vb], axis=1).T.astype(BF16)


def kernel(x, c, positions, w_ada, b_ada, g_norm, w_in, b_f, sinks, w_o_swa, w_o_fox, w_out, g_final):
    b, s, _ = x.shape
    depth = w_in.shape[0]
    assert depth == 1, "the output stage fuses the final RMSNorm into the single layer"
    inv_freq = ROPE_THETA ** (-jnp.arange(0, HEAD_DIM, 2, dtype=F32) / HEAD_DIM)
    invf = jnp.broadcast_to(inv_freq[:, None], (HALF, _PROJ_TM))
    pos_f = positions.astype(F32)[:, None, :]
    e_mat = jnp.asarray(_aug_placement(), BF16)
    c_pad = jnp.zeros((SUBLANES, D_MODEL), F32).at[:b].set(c)
    for l in range(depth):
        ada = _ada(c_pad, w_ada[l], b_ada[l][None, :])[:b]
        shift, scale, gate = [ada[:, None, k * D_MODEL:(k + 1) * D_MODEL] for k in range(3)]
        w_all, w_vt = _layout_w_in(w_in[l])
        bf_rep = jnp.tile(b_f[l].astype(F32), LANES // FOX_HEADS)[None, :]
        qa, ka, vat, sza, q_aug, k_aug, vbt, szb, sga, sgb = _proj(
            x, pos_f, scale, shift, g_norm[l][None, :], invf, bf_rep, w_all, w_vt, e_mat, tm=_PROJ_TM)
        gated_a = _swa(sinks[l].astype(F32), qa, ka, vat, sza, tq=_SWA_TQ)
        gated_b = _fox(q_aug, k_aug, vbt, szb, t=_FOX_T)
        x = _out(x, gated_a, gated_b, sga, sgb, gate,
                 w_o_swa[l].astype(BF16), w_o_fox[l].astype(BF16), w_out[l].astype(BF16),
                 g_final[None, :], tm=_OUT_TM)
    return x
```

```python
import functools
import math

import numpy as np
import jax
import jax.numpy as jnp
from jax import lax
from jax.experimental import pallas as pl
from jax.experimental.pallas import tpu as pltpu

D_MODEL = 1024
HEAD_DIM = 64
HALF = HEAD_DIM // 2
SWA_Q_HEADS = 8
SWA_KV_HEADS = 2
SWA_WIDTH = SWA_Q_HEADS * HEAD_DIM
SWA_KV_WIDTH = SWA_KV_HEADS * HEAD_DIM
FOX_HEADS = 8
FOX_WIDTH = FOX_HEADS * HEAD_DIM
WINDOW = 128
ROPE_THETA = 10000.0
NORM_EPS = 1e-6
QK_SCALE = HEAD_DIM ** -0.5
LOG2E = math.log2(math.e)

LANES = 128
SUBLANES = 8
MIB = 1024 * 1024
V7X_VMEM_BYTES = 64 * MIB

_PROJ_TM = 512
_SWA_TQ = 1024
_FOX_T = 512
_OUT_TM = 1024

F32 = jnp.float32
BF16 = jnp.bfloat16

_QA, _KA, _ZA = 0, 512, 768
_QB, _KB, _FB, _ZB, _GA, _GB = 1280, 1792, 2304, 2432, 2944, 3968
_PROJ_COLS = 4992

_ONE_LANE = 24

_SWA_AHEAD = 2
_ONES_ROWS = 16


_PROJ_VMEM = 56 * MIB
_FOX_VMEM = 56 * MIB
_OUT_VMEM = 48 * MIB
assert max(_PROJ_VMEM, _FOX_VMEM, _OUT_VMEM) <= V7X_VMEM_BYTES - 8 * MIB


def _split3(v):
    hi = v.astype(BF16)
    r1 = v - hi.astype(F32)
    mid = r1.astype(BF16)
    lo = (r1 - mid.astype(F32)).astype(BF16)
    return hi, mid, lo


def _ada_kernel(c_ref, w_ref, b_ref, o_ref):
    o_ref[...] = jnp.dot(c_ref[...], w_ref[...], preferred_element_type=F32,
                         precision=lax.Precision.HIGHEST) + b_ref[...]


def _ada(c_pad, w_ada, b_ada):
    rows = c_pad.shape[0]
    n = w_ada.shape[1]
    nblk = n // D_MODEL
    return pl.pallas_call(
        _ada_kernel,
        grid=(nblk,),
        in_specs=[pl.BlockSpec((rows, D_MODEL), lambda j: (0, 0)),
                  pl.BlockSpec((D_MODEL, D_MODEL), lambda j: (0, j)),
                  pl.BlockSpec((1, D_MODEL), lambda j: (0, j))],
        out_specs=pl.BlockSpec((rows, D_MODEL), lambda j: (0, j)),
        out_shape=jax.ShapeDtypeStruct((rows, n), F32),
        name="ada_mod",
    )(c_pad, w_ada, b_ada)


def _sigmoid(z):
    return 0.5 * jnp.tanh(0.5 * z) + 0.5


def _proj_kernel(x_ref, pos_ref, scale_ref, shift_ref, gn_ref, invf_ref, bf_ref, w_ref, wvt_ref, e_ref,
                 qa_ref, ka_ref, vat_ref, sza_ref, qaug_ref, kaug_ref, vbt_ref, szb_ref,
                 sga_ref, sgb_ref, carry_ref, *, tm):
    i = pl.program_id(1)

    @pl.when(i == 0)
    def _():
        carry_ref[...] = jnp.zeros_like(carry_ref)

    x = x_ref[0]
    ms = jnp.mean(x * x, axis=-1, keepdims=True)
    y = x * lax.rsqrt(ms + NORM_EPS) * gn_ref[...]
    h = y * (1.0 + scale_ref[0]) + shift_ref[0]
    hb = h.astype(BF16)

    def proj(off, n):
        return jnp.dot(hb, w_ref[:, off:off + n], preferred_element_type=F32)

    lane = lax.broadcasted_iota(jnp.int32, (tm, LANES), 1)
    low = lane < HEAD_DIM


    zf = proj(_FB, LANES) + bf_ref[...]
    logf = jnp.minimum(zf, 0.0) - jnp.log1p(jnp.exp(-jnp.abs(zf)))
    grp = lane // FOX_HEADS
    hi, mid, lo = _split3(logf)
    zero = jnp.zeros((tm, LANES), F32)
    parts = jnp.where(grp == 0, hi.astype(F32),
                      jnp.where(grp == 1, mid.astype(F32),
                                jnp.where(grp == 2, lo.astype(F32), zero))).astype(BF16)
    tri = (lax.broadcasted_iota(jnp.int32, (tm, tm), 0)
           >= lax.broadcasted_iota(jnp.int32, (tm, tm), 1)).astype(BF16)
    sga_ref[0] = _sigmoid(proj(_GA, D_MODEL)).astype(BF16)
    rsum = jnp.dot(tri, parts, preferred_element_type=F32)
    c0 = rsum + pltpu.roll(rsum, LANES - FOX_HEADS, 1) + pltpu.roll(rsum, LANES - 2 * FOX_HEADS, 1)
    c0 = jnp.where(grp == 0, c0, zero)
    cl = c0 + pltpu.roll(c0, FOX_HEADS, 1) + pltpu.roll(c0, 2 * FOX_HEADS, 1)
    cum = cl + carry_ref[...]
    carry_ref[...] = cum[tm - 1:tm, :]

    hi, mid, lo = _split3(cum * LOG2E)
    one = jnp.ones((tm, LANES), F32)
    carrier = jnp.where(grp == 0, hi.astype(F32),
                        jnp.where(grp == 1, mid.astype(F32),
                                  jnp.where(grp == 2, lo.astype(F32),
                                            jnp.where(lane == _ONE_LANE, one, zero)))).astype(BF16)
    sgb_ref[0] = _sigmoid(proj(_GB, D_MODEL)).astype(BF16)
    aug = jnp.dot(carrier, e_ref[...], preferred_element_type=F32)

    z = proj(_ZA, SWA_WIDTH)
    sza_ref[0] = (z * _sigmoid(z)).astype(BF16)
    z = proj(_ZB, FOX_WIDTH)
    szb_ref[0] = (z * _sigmoid(z)).astype(BF16)

    ang = invf_ref[...] * pos_ref[0]
    cos = jnp.tile(jnp.cos(ang), (LANES // HALF, 1)).T
    sn = jnp.sin(ang)
    sin = jnp.concatenate([-sn, -sn, sn, sn], axis=0).T
    r = proj(_QA, SWA_WIDTH)
    for c in range(SWA_WIDTH // LANES):
        rc = r[:, c * LANES:(c + 1) * LANES]
        t = rc * cos + pltpu.roll(rc, HEAD_DIM, 1) * sin
        qa_ref[0, :, c * LANES:(c + 1) * LANES] = (t * (QK_SCALE * LOG2E)).astype(BF16)
    r = proj(_KA, 2 * SWA_KV_WIDTH)
    for c in range(2 * SWA_KV_WIDTH // LANES):
        rc = r[:, c * LANES:(c + 1) * LANES]
        t = rc * cos + pltpu.roll(rc, HEAD_DIM, 1) * sin
        ka_ref[0, :, c * LANES:(c + 1) * LANES] = t.astype(BF16)

    rq = proj(_QB, FOX_WIDTH)
    rk = proj(_KB, FOX_WIDTH)
    for hd in range(FOX_HEADS):
        a = (hd // 2) * LANES
        gq = aug[:, a:a + LANES]
        gk = aug[:, FOX_WIDTH + a:FOX_WIDTH + a + LANES]
        qv = rq[:, a:a + LANES] * (QK_SCALE * LOG2E)
        kv = rk[:, a:a + LANES]
        if hd % 2 == 0:
            qaug_ref[0, hd] = jnp.where(low, qv, gq).astype(BF16)
            kaug_ref[0, hd] = jnp.where(low, kv, gk).astype(BF16)
        else:
            qaug_ref[0, hd] = jnp.where(low, gq, qv).astype(BF16)
            kaug_ref[0, hd] = jnp.where(low, gk, kv).astype(BF16)

    vt = lax.dot_general(wvt_ref[...], hb, (((1,), (1,)), ((), ())), preferred_element_type=F32)
    vat_ref[0] = vt[0:SWA_KV_WIDTH].astype(BF16)
    vbt_ref[0] = vt[SWA_KV_WIDTH:].astype(BF16)


def _aug_placement():
    e = np.zeros((LANES, 2 * FOX_HEADS * HEAD_DIM), np.float32)
    for hd in range(FOX_HEADS):
        base = (hd // 2) * LANES + (HEAD_DIM if hd % 2 == 0 else 0)
        kbase = FOX_HEADS * HEAD_DIM + base
        for part in range(3):
            e[part * FOX_HEADS + hd, base + part] = 1.0
            e[_ONE_LANE, base + 3 + part] = 1.0
            e[_ONE_LANE, kbase + part] = 1.0
            e[part * FOX_HEADS + hd, kbase + 3 + part] = -1.0
    return e


def _proj(x, pos_f, scale, shift, gn, invf, bf_rep, w_all, w_vt, e_mat, *, tm):
    b, s, _ = x.shape
    grid = (b, s // tm)
    row = lambda bi, i: (bi, i, 0)
    per_b = lambda bi, i: (bi, 0, 0)
    const2 = lambda bi, i: (0, 0)
    once = pl.Buffered(1)
    in_specs = [
        pl.BlockSpec((1, tm, D_MODEL), row),
        pl.BlockSpec((1, 1, tm), lambda bi, i: (bi, 0, i)),
        pl.BlockSpec((1, 1, D_MODEL), per_b),
        pl.BlockSpec((1, 1, D_MODEL), per_b),
        pl.BlockSpec((1, D_MODEL), const2),
        pl.BlockSpec((HALF, tm), const2),
        pl.BlockSpec((1, LANES), const2),
        pl.BlockSpec((D_MODEL, _PROJ_COLS), const2, pipeline_mode=once),
        pl.BlockSpec((SWA_KV_WIDTH + FOX_WIDTH, D_MODEL), const2, pipeline_mode=once),
        pl.BlockSpec((LANES, 2 * FOX_WIDTH), const2, pipeline_mode=once),
    ]
    head4 = lambda bi, i: (bi, 0, i, 0)
    out_specs = [
        pl.BlockSpec((1, tm, SWA_WIDTH), row),
        pl.BlockSpec((1, tm, 2 * SWA_KV_WIDTH), row),
        pl.BlockSpec((1, SWA_KV_WIDTH, tm), lambda bi, i: (bi, 0, i)),
        pl.BlockSpec((1, tm, SWA_WIDTH), row),
        pl.BlockSpec((1, FOX_HEADS, tm, LANES), head4),
        pl.BlockSpec((1, FOX_HEADS, tm, LANES), head4),
        pl.BlockSpec((1, FOX_WIDTH, tm), lambda bi, i: (bi, 0, i)),
        pl.BlockSpec((1, tm, FOX_WIDTH), row),
        pl.BlockSpec((1, tm, D_MODEL), row),
        pl.BlockSpec((1, tm, D_MODEL), row),
    ]
    sds = jax.ShapeDtypeStruct
    out_shape = [
        sds((b, s, SWA_WIDTH), BF16), sds((b, s, 2 * SWA_KV_WIDTH), BF16),
        sds((b, SWA_KV_WIDTH, s), BF16), sds((b, s, SWA_WIDTH), BF16),
        sds((b, FOX_HEADS, s, LANES), BF16), sds((b, FOX_HEADS, s, LANES), BF16),
        sds((b, FOX_WIDTH, s), BF16), sds((b, s, FOX_WIDTH), BF16),
        sds((b, s, D_MODEL), BF16), sds((b, s, D_MODEL), BF16),
    ]
    return pl.pallas_call(
        functools.partial(_proj_kernel, tm=tm),
        grid=grid, in_specs=in_specs, out_specs=out_specs, out_shape=out_shape,
        scratch_shapes=[pltpu.VMEM((1, LANES), F32)],
        compiler_params=pltpu.CompilerParams(
            dimension_semantics=("arbitrary", "arbitrary"),
            vmem_limit_bytes=_PROJ_VMEM),
        name="in_proj",
    )(x, pos_f, scale, shift, gn, invf, bf_rep, w_all, w_vt, e_mat)


def _swa_kernel(sinks_ref, q_ref, kc_ref, kp_ref, vtc_ref, vtp_ref, bias_ref, sz_ref, o_ref,
                kband, vtband, *, tq):
    i = pl.program_id(1)
    nblk = tq // WINDOW
    nhq = SWA_Q_HEADS // SWA_KV_HEADS
    kband[0:WINDOW] = kp_ref[0]
    kband[WINDOW:] = kc_ref[0]
    vtband[:, 0:WINDOW] = vtp_ref[0]
    vtband[:, WINDOW:] = vtc_ref[0]

    cols = nhq * WINDOW
    first_plane = jnp.where(i > 0, 0, 1)
    headid = lax.broadcasted_iota(jnp.int32, (1, cols), 1) // WINDOW
    low = lax.broadcasted_iota(jnp.int32, (WINDOW, LANES), 1) % HEAD_DIM < HALF
    ones = jnp.ones((_ONES_ROWS, 2 * WINDOW), BF16)
    nt = (((1,), (1,)), ((), ()))

    def logits(r, g):
        kb = kband[r * WINDOW:(r + 2) * WINDOW, g * LANES:(g + 1) * LANES]
        qs = []
        for c in range(2):
            a = (2 * g + c) * LANES
            qc = q_ref[0, r * WINDOW:(r + 1) * WINDOW, a:a + LANES]
            qs.append(jnp.where(low, qc, jnp.zeros_like(qc)))
            qs.append(jnp.where(low, jnp.zeros_like(qc), qc))
        qst = jnp.concatenate(qs, axis=0)
        s = lax.dot_general(kb, qst, nt, preferred_element_type=F32)
        return s + bias_ref[first_plane if r == 0 else 0]

    def attend(r, g, s):
        sink = jnp.zeros((1, cols), F32)
        for k in range(nhq):
            sink = jnp.where(headid == k, sinks_ref[g * nhq + k] * LOG2E, sink)
        m = jnp.maximum(jnp.max(s, axis=0, keepdims=True), sink)
        p = jnp.exp2(s - m).astype(BF16)
        vt = vtband[g * HEAD_DIM:(g + 1) * HEAD_DIM, r * WINDOW:(r + 2) * WINDOW]
        pv = jnp.dot(jnp.concatenate([vt, ones], axis=0), p, preferred_element_type=F32)
        den = pv[HEAD_DIM:HEAD_DIM + 1] + jnp.exp2(sink - m)
        o_t = pv[0:HEAD_DIM] / den
        for c in range(2):
            a = (2 * g + c) * LANES
            pair = jnp.concatenate([o_t[:, 2 * c * WINDOW:(2 * c + 1) * WINDOW],
                                    o_t[:, (2 * c + 1) * WINDOW:(2 * c + 2) * WINDOW]], axis=0)
            gate = sz_ref[0, r * WINDOW:(r + 1) * WINDOW, a:a + LANES].astype(F32)
            o_ref[0, r * WINDOW:(r + 1) * WINDOW, a:a + LANES] = (pair.T * gate).astype(BF16)

    work = [(r, g) for r in range(nblk) for g in range(SWA_KV_HEADS)]
    pending = [logits(*w) for w in work[:_SWA_AHEAD]]
    for n, (r, g) in enumerate(work):
        if n + _SWA_AHEAD < len(work):
            pending.append(logits(*work[n + _SWA_AHEAD]))
        attend(r, g, pending.pop(0))


def _swa(sinks, qa, ka, vat, sza, *, tq):
    b, s, _ = qa.shape
    kvw = 2 * SWA_KV_WIDTH
    per = tq // WINDOW
    cur = lambda bi, i: (bi, i, 0)
    prev = lambda bi, i: (bi, jnp.maximum(i * per - 1, 0), 0)
    cur_t = lambda bi, i: (bi, 0, i)
    prev_t = lambda bi, i: (bi, 0, jnp.maximum(i * per - 1, 0))
    key = np.arange(2 * WINDOW)[:, None]
    qt = np.tile(np.arange(WINDOW), SWA_Q_HEADS // SWA_KV_HEADS)[None, :]
    band = np.where(key < WINDOW, key > qt, (key - WINDOW) <= qt)
    planes = np.stack([band, band & (key >= WINDOW)])
    bias = jnp.asarray(np.where(planes, 0.0, -np.inf).astype(np.float32))
    return pl.pallas_call(
        functools.partial(_swa_kernel, tq=tq),
        grid=(b, s // tq),
        in_specs=[pl.BlockSpec(memory_space=pltpu.SMEM),
                  pl.BlockSpec((1, tq, SWA_WIDTH), cur),
                  pl.BlockSpec((1, tq, kvw), cur),
                  pl.BlockSpec((1, WINDOW, kvw), prev),
                  pl.BlockSpec((1, SWA_KV_WIDTH, tq), cur_t),
                  pl.BlockSpec((1, SWA_KV_WIDTH, WINDOW), prev_t),
                  pl.BlockSpec(bias.shape, lambda bi, i: (0, 0, 0), pipeline_mode=pl.Buffered(1)),
                  pl.BlockSpec((1, tq, SWA_WIDTH), cur)],
        out_specs=pl.BlockSpec((1, tq, SWA_WIDTH), cur),
        out_shape=jax.ShapeDtypeStruct((b, s, SWA_WIDTH), BF16),
        scratch_shapes=[pltpu.VMEM((tq + WINDOW, kvw), BF16),
                        pltpu.VMEM((SWA_KV_WIDTH, tq + WINDOW), BF16)],
        compiler_params=pltpu.CompilerParams(dimension_semantics=("arbitrary", "arbitrary")),
        name="swa_attn",
    )(sinks, qa, ka, ka, vat, vat, bias, sza)


_ACC_ROWS = HEAD_DIM + _ONES_ROWS


def _fox_schedule(n_rows):
    tiles = [(i, i) for i in range(n_rows)] + [(i, j) for i in range(n_rows) for j in range(i)]
    return np.asarray(tiles, np.int32).T.copy()


def _fox_kernel(tbl_ref, q_ref, k_ref, vt_ref, bias_ref, sz_ref, o_ref,
                s0, s1, mx0, mx1, p0, p1, a0, a1, m_ref, acc_ref, *, t, n_rows):
    nt = (((1,), (1,)), ((), ()))
    g = t // SUBLANES
    h = t // 2
    n_diag = n_rows
    n_tiles = n_rows * (n_rows + 1) // 2
    assert n_diag % 2 == 0 and n_tiles % 2 == 0 and n_diag >= 2 and n_tiles - n_diag >= 2

    def step(j, diag, s_w, mx_w, s_r, mx_r, p_w, a_w, p_r, a_r):
        if diag[0] is not None:
            qrow = pl.multiple_of(tbl_ref[0, j] * t, t)
            krow = pl.multiple_of(tbl_ref[1, j] * t, t)
            for hh in range(2):
                if diag[0]:
                    top = lax.dot_general(k_ref[0, hh, pl.ds(krow, h), :], q_ref[0, hh, pl.ds(qrow, t), :],
                                          nt, preferred_element_type=F32) + bias_ref[0:h, :]
                    bot = lax.dot_general(k_ref[0, hh, pl.ds(pl.multiple_of(krow + h, h), h), :],
                                          q_ref[0, hh, pl.ds(pl.multiple_of(qrow + h, h), h), :],
                                          nt, preferred_element_type=F32) + bias_ref[h:, h:]
                    s_w[hh, 0:h] = top
                    s_w[hh, h:, h:] = bot
                    top_mx = jnp.max(top.reshape(h // SUBLANES, SUBLANES, t), axis=0)
                    bot_mx = jnp.max(bot.reshape(h // SUBLANES, SUBLANES, h), axis=0)
                    mx_w[hh, :, 0:h] = top_mx[:, 0:h]
                    mx_w[hh, :, h:] = jnp.maximum(top_mx[:, h:], bot_mx)
                else:
                    s = lax.dot_general(k_ref[0, hh, pl.ds(krow, t), :], q_ref[0, hh, pl.ds(qrow, t), :],
                                        nt, preferred_element_type=F32)
                    s_w[hh] = s
                    mx_w[hh] = jnp.max(s.reshape(g, SUBLANES, t), axis=0)

        if diag[1] is not None:
            srow = tbl_ref[0, j - 1]
            for hh in range(2):
                m_tile = jnp.max(mx_r[hh], axis=0, keepdims=True)
                if diag[1]:
                    m_new = jnp.broadcast_to(m_tile, (SUBLANES, t))
                else:
                    m_old = m_ref[srow, hh]
                    m_new = jnp.maximum(m_old, m_tile)
                    a_w[hh] = jnp.exp2(m_old - m_new)
                m_ref[srow, hh] = m_new
                if diag[1]:
                    top = jnp.exp2(s_r[hh, 0:h].reshape(h // SUBLANES, SUBLANES, t) - m_new[None])
                    bot = jnp.exp2(s_r[hh, h:, h:].reshape(h // SUBLANES, SUBLANES, h) - m_new[None, :, h:])
                    p_w[hh, 0:h] = top.reshape(h, t).astype(BF16)
                    p_w[hh, h:, h:] = bot.reshape(h, h).astype(BF16)
                else:
                    p = jnp.exp2(s_r[hh].reshape(g, SUBLANES, t) - m_new[None])
                    p_w[hh] = p.reshape(t, t).astype(BF16)

        if diag[2] is not None:
            vcol = pl.multiple_of(tbl_ref[1, j - 2] * t, t)
            arow = tbl_ref[0, j - 2]
            for hh in range(2):
                vt = vt_ref[0, hh * HEAD_DIM:(hh + 1) * HEAD_DIM, pl.ds(vcol, t)]
                lhs = jnp.concatenate([vt, jnp.ones((_ONES_ROWS, t), BF16)], axis=0)
                if diag[2]:
                    acc_ref[arow, hh, :, 0:h] = jnp.dot(lhs[:, 0:h], p_r[hh, 0:h, 0:h],
                                                        preferred_element_type=F32)
                    acc_ref[arow, hh, :, h:] = jnp.dot(lhs, p_r[hh, :, h:], preferred_element_type=F32)
                else:
                    part = jnp.dot(lhs, p_r[hh], preferred_element_type=F32)
                    old = acc_ref[arow, hh].reshape(_ACC_ROWS // SUBLANES, SUBLANES, t)
                    acc_ref[arow, hh] = (a_r[hh][None] * old).reshape(_ACC_ROWS, t) + part

    def step_pair(j, diag_even, diag_odd):
        step(j, diag_even, s0, mx0, s1, mx1, p1, a1, p0, a0)
        step(j + 1, diag_odd, s1, mx1, s0, mx0, p0, a0, p1, a1)

    def steady(first, last, diag):
        def body(kk, carry):
            step_pair(first + 2 * kk, diag, diag)
            return carry

        lax.fori_loop(0, (last - first) // 2, body, 0)

    step_pair(0, (True, None, None), (True, True, None))
    steady(2, n_diag, (True, True, True))
    step_pair(n_diag, (False, True, True), (False, False, True))
    steady(n_diag + 2, n_tiles, (False, False, False))
    step_pair(n_tiles, (None, False, False), (None, None, False))

    def finish_row(i, carry):
        halves = []
        for hh in range(2):
            a = acc_ref[i, hh]
            halves.append(a[0:HEAD_DIM] / a[HEAD_DIM:HEAD_DIM + 1])
        o_t = jnp.concatenate(halves, axis=0)
        rows = pl.ds(pl.multiple_of(i * t, t), t)
        o_ref[0, rows, :] = (o_t.T * sz_ref[0, rows, :].astype(F32)).astype(BF16)
        return carry

    lax.fori_loop(0, n_rows, finish_row, 0)


def _fox(q_aug, k_aug, vbt, szb, *, t):
    b, _, s, _ = q_aug.shape
    npair = FOX_HEADS // 2
    n_rows = s // t
    tbl = _fox_schedule(n_rows)
    tri = np.triu(np.ones((t, t), bool))
    bias = jnp.asarray(np.where(tri, 0.0, -np.inf).astype(np.float32))
    pair = lambda bi, p: (bi, p, 0, 0)
    stat = pltpu.VMEM((2, SUBLANES, t), F32)
    return pl.pallas_call(
        functools.partial(_fox_kernel, t=t, n_rows=n_rows),
        grid=(b, npair),
        in_specs=[pl.BlockSpec(memory_space=pltpu.SMEM),
                  pl.BlockSpec((1, 2, s, LANES), pair),
                  pl.BlockSpec((1, 2, s, LANES), pair),
                  pl.BlockSpec((1, 2 * HEAD_DIM, s), lambda bi, p: (bi, p, 0)),
                  pl.BlockSpec((t, t), lambda bi, p: (0, 0), pipeline_mode=pl.Buffered(1)),
                  pl.BlockSpec((1, s, LANES), lambda bi, p: (bi, 0, p))],
        out_specs=pl.BlockSpec((1, s, LANES), lambda bi, p: (bi, 0, p)),
        out_shape=jax.ShapeDtypeStruct((b, s, FOX_WIDTH), BF16),
        scratch_shapes=[pltpu.VMEM((2, t, t), F32), pltpu.VMEM((2, t, t), F32),
                        stat, stat,
                        pltpu.VMEM((2, t, t), BF16), pltpu.VMEM((2, t, t), BF16),
                        stat, stat,
                        pltpu.VMEM((n_rows, 2, SUBLANES, t), F32),
                        pltpu.VMEM((n_rows, 2, _ACC_ROWS, t), F32)],
        compiler_params=pltpu.CompilerParams(
            dimension_semantics=("arbitrary", "arbitrary"),
            vmem_limit_bytes=_FOX_VMEM),
        name="fox_attn",
    )(jnp.asarray(tbl), q_aug, k_aug, vbt, bias, szb)


def _out_kernel(x_ref, ga_ref, gb_ref, sga_ref, sgb_ref, gate_ref,
                woa_ref, wob_ref, wout_ref, gf_ref, o_ref):
    ya = jnp.dot(ga_ref[0], woa_ref[...], preferred_element_type=F32)
    yb = jnp.dot(gb_ref[0], wob_ref[...], preferred_element_type=F32)
    merged = sga_ref[0].astype(F32) * ya + sgb_ref[0].astype(F32) * yb
    d = jnp.dot(merged.astype(BF16), wout_ref[...], preferred_element_type=F32)
    xo = x_ref[0] + gate_ref[0] * d
    ms = jnp.mean(xo * xo, axis=-1, keepdims=True)
    o_ref[0] = xo * lax.rsqrt(ms + NORM_EPS) * gf_ref[...]


def _out(x, gated_a, gated_b, sga, sgb, gate, woa, wob, wout, gf, *, tm):
    b, s, _ = x.shape
    row = lambda bi, i: (bi, i, 0)
    per_b = lambda bi, i: (bi, 0, 0)
    const2 = lambda bi, i: (0, 0)
    once = pl.Buffered(1)
    return pl.pallas_call(
        _out_kernel,
        grid=(b, s // tm),
        in_specs=[pl.BlockSpec((1, tm, D_MODEL), row),
                  pl.BlockSpec((1, tm, SWA_WIDTH), row), pl.BlockSpec((1, tm, FOX_WIDTH), row),
                  pl.BlockSpec((1, tm, D_MODEL), row), pl.BlockSpec((1, tm, D_MODEL), row),
                  pl.BlockSpec((1, 1, D_MODEL), per_b),
                  pl.BlockSpec((SWA_WIDTH, D_MODEL), const2, pipeline_mode=once),
                  pl.BlockSpec((FOX_WIDTH, D_MODEL), const2, pipeline_mode=once),
                  pl.BlockSpec((D_MODEL, D_MODEL), const2, pipeline_mode=once),
                  pl.BlockSpec((1, D_MODEL), const2)],
        out_specs=pl.BlockSpec((1, tm, D_MODEL), row),
        out_shape=jax.ShapeDtypeStruct((b, s, D_MODEL), F32),
        compiler_params=pltpu.CompilerParams(
            dimension_semantics=("arbitrary", "arbitrary"),
            vmem_limit_bytes=_OUT_VMEM),
        name="out_proj",
    )(x, gated_a, gated_b, sga, sgb, gate, woa, wob, wout, gf)


def _pair_halves(w, nheads, dup):
    rows = w.shape[0]
    if dup:
        w5 = jnp.broadcast_to(w.reshape(rows, nheads, 1, 2, HALF), (rows, nheads, 2, 2, HALF))
    else:
        w5 = w.reshape(rows, nheads // 2, 2, 2, HALF)
    return w5.transpose(0, 1, 3, 2, 4).reshape(rows, -1)


def _layout_w_in(w):
    o = np.cumsum([0, SWA_WIDTH, SWA_KV_WIDTH, SWA_KV_WIDTH, SWA_WIDTH, FOX_WIDTH, FOX_WIDTH, FOX_WIDTH,
                   FOX_HEADS, FOX_WIDTH, D_MODEL, D_MODEL]).tolist()
    qa, ka, va, za, qb, kb, vb, fb, zb, ga, gb = [w[:, o[k]:o[k + 1]] for k in range(11)]
    cols = [_pair_halves(qa, SWA_Q_HEADS, False), _pair_halves(ka, SWA_KV_HEADS, True),
            za, qb, kb, jnp.tile(fb, (1, LANES // FOX_HEADS)), zb, ga, gb]
    w_all = jnp.concatenate(cols, axis=1).astype(BF16)
    assert w_all.shape[1] == _PROJ_COLS
    return w_all, jnp.concatenate([va, vb], axis=1).T.astype(BF16)


def kernel(x, c, positions, w_ada, b_ada, g_norm, w_in, b_f, sinks, w_o_swa, w_o_fox, w_out, g_final):
    b, s, _ = x.shape
    depth = w_in.shape[0]
    assert depth == 1, "the output stage fuses the final RMSNorm into the single layer"
    inv_freq = ROPE_THETA ** (-jnp.arange(0, HEAD_DIM, 2, dtype=F32) / HEAD_DIM)
    invf = jnp.broadcast_to(inv_freq[:, None], (HALF, _PROJ_TM))
    pos_f = positions.astype(F32)[:, None, :]
    e_mat = jnp.asarray(_aug_placement(), BF16)
    c_pad = jnp.zeros((SUBLANES, D_MODEL), F32).at[:b].set(c)
    for l in range(depth):
        ada = _ada(c_pad, w_ada[l], b_ada[l][None, :])[:b]
        shift, scale, gate = [ada[:, None, k * D_MODEL:(k + 1) * D_MODEL] for k in range(3)]
        w_all, w_vt = _layout_w_in(w_in[l])
        bf_rep = jnp.tile(b_f[l].astype(F32), LANES // FOX_HEADS)[None, :]
        qa, ka, vat, sza, q_aug, k_aug, vbt, szb, sga, sgb = _proj(
            x, pos_f, scale, shift, g_norm[l][None, :], invf, bf_rep, w_all, w_vt, e_mat, tm=_PROJ_TM)
        gated_a = _swa(sinks[l].astype(F32), qa, ka, vat, sza, tq=_SWA_TQ)
        gated_b = _fox(q_aug, k_aug, vbt, szb, t=_FOX_T)
        x = _out(x, gated_a, gated_b, sga, sgb, gate,
                 w_o_swa[l].astype(BF16), w_o_fox[l].astype(BF16), w_out[l].astype(BF16),
                 g_final[None, :], tm=_OUT_TM)
    return x
```

```python
import functools
import math

import numpy as np
import jax
import jax.numpy as jnp
from jax import lax
from jax.experimental import pallas as pl
from jax.experimental.pallas import tpu as pltpu

D_MODEL = 1024
HEAD_DIM = 64
HALF = HEAD_DIM // 2
SWA_Q_HEADS = 8
SWA_KV_HEADS = 2
SWA_WIDTH = SWA_Q_HEADS * HEAD_DIM
SWA_KV_WIDTH = SWA_KV_HEADS * HEAD_DIM
FOX_HEADS = 8
FOX_WIDTH = FOX_HEADS * HEAD_DIM
WINDOW = 128
ROPE_THETA = 10000.0
NORM_EPS = 1e-6
QK_SCALE = HEAD_DIM ** -0.5
LOG2E = math.log2(math.e)

LANES = 128
SUBLANES = 8
MIB = 1024 * 1024
V7X_VMEM_BYTES = 64 * MIB

_PROJ_TM = 512
_SWA_TQ = 1024
_FOX_T = 512
_OUT_TM = 1024
_OUT_CHUNK = 256
_OUT_SUB = 512

F32 = jnp.float32
BF16 = jnp.bfloat16

_QA, _KA, _ZA = 0, 512, 768
_QB, _KB, _FB, _ZB, _GA, _GB = 1280, 1792, 2304, 2432, 2944, 3968
_PROJ_COLS = 4992

_ONE_LANE = 24

_SWA_AHEAD = 2
_ONES_ROWS = 16


_PROJ_VMEM = 56 * MIB
_FOX_VMEM = 56 * MIB
_OUT_VMEM = 48 * MIB
assert max(_PROJ_VMEM, _FOX_VMEM, _OUT_VMEM) <= V7X_VMEM_BYTES - 8 * MIB


def _split3(v):
    hi = v.astype(BF16)
    r1 = v - hi.astype(F32)
    mid = r1.astype(BF16)
    lo = (r1 - mid.astype(F32)).astype(BF16)
    return hi, mid, lo


def _ada_kernel(c_ref, w_ref, b_ref, o_ref):
    o_ref[...] = jnp.dot(c_ref[...], w_ref[...], preferred_element_type=F32,
                         precision=lax.Precision.HIGHEST) + b_ref[...]


def _ada(c_pad, w_ada, b_ada):
    rows = c_pad.shape[0]
    n = w_ada.shape[1]
    nblk = n // D_MODEL
    return pl.pallas_call(
        _ada_kernel,
        grid=(nblk,),
        in_specs=[pl.BlockSpec((rows, D_MODEL), lambda j: (0, 0)),
                  pl.BlockSpec((D_MODEL, D_MODEL), lambda j: (0, j)),
                  pl.BlockSpec((1, D_MODEL), lambda j: (0, j))],
        out_specs=pl.BlockSpec((rows, D_MODEL), lambda j: (0, j)),
        out_shape=jax.ShapeDtypeStruct((rows, n), F32),
        name="ada_mod",
    )(c_pad, w_ada, b_ada)


def _sigmoid(z):
    return 0.5 * jnp.tanh(0.5 * z) + 0.5


def _proj_kernel(x_ref, pos_ref, scale_ref, shift_ref, gn_ref, invf_ref, bf_ref, w_ref, wvt_ref, e_ref,
                 qa_ref, ka_ref, vat_ref, sza_ref, qaug_ref, kaug_ref, vbt_ref, szb_ref,
                 sga_ref, sgb_ref, carry_ref, *, tm):
    i = pl.program_id(1)

    @pl.when(i == 0)
    def _():
        carry_ref[...] = jnp.zeros_like(carry_ref)

    x = x_ref[0]
    ms = jnp.mean(x * x, axis=-1, keepdims=True)
    h = x * lax.rsqrt(ms + NORM_EPS) * (gn_ref[...] * (1.0 + scale_ref[0])) + shift_ref[0]
    hb = h.astype(BF16)

    def proj(off, n):
        return jnp.dot(hb, w_ref[:, off:off + n], preferred_element_type=F32)

    lane = lax.broadcasted_iota(jnp.int32, (tm, LANES), 1)
    low = lane < HEAD_DIM


    zf = proj(_FB, LANES) + bf_ref[...]
    logf = jnp.minimum(zf, 0.0) - jnp.log1p(jnp.exp(-jnp.abs(zf)))
    grp = lane // FOX_HEADS
    hi, mid, lo = _split3(logf)
    zero = jnp.zeros((tm, LANES), F32)
    parts = jnp.where(grp == 0, hi.astype(F32),
                      jnp.where(grp == 1, mid.astype(F32),
                                jnp.where(grp == 2, lo.astype(F32), zero))).astype(BF16)
    tri = (lax.broadcasted_iota(jnp.int32, (tm, tm), 0)
           >= lax.broadcasted_iota(jnp.int32, (tm, tm), 1)).astype(BF16)
    sga_ref[0] = _sigmoid(proj(_GA, D_MODEL)).astype(BF16)
    rsum = jnp.dot(tri, parts, preferred_element_type=F32)
    c0 = rsum + pltpu.roll(rsum, LANES - FOX_HEADS, 1) + pltpu.roll(rsum, LANES - 2 * FOX_HEADS, 1)
    c0 = jnp.where(grp == 0, c0, zero)
    cl = c0 + pltpu.roll(c0, FOX_HEADS, 1) + pltpu.roll(c0, 2 * FOX_HEADS, 1)
    cum = cl + carry_ref[...]
    carry_ref[...] = cum[tm - 1:tm, :]

    hi, mid, lo = _split3(cum * LOG2E)
    one = jnp.ones((tm, LANES), F32)
    carrier = jnp.where(grp == 0, hi.astype(F32),
                        jnp.where(grp == 1, mid.astype(F32),
                                  jnp.where(grp == 2, lo.astype(F32),
                                            jnp.where(lane == _ONE_LANE, one, zero)))).astype(BF16)
    sgb_ref[0] = _sigmoid(proj(_GB, D_MODEL)).astype(BF16)
    aug = jnp.dot(carrier, e_ref[...], preferred_element_type=F32)

    z = proj(_ZA, SWA_WIDTH)
    sza_ref[0] = (z * _sigmoid(z)).astype(BF16)
    z = proj(_ZB, FOX_WIDTH)
    szb_ref[0] = (z * _sigmoid(z)).astype(BF16)

    ang = invf_ref[...] * pos_ref[0]
    cos = jnp.tile(jnp.cos(ang), (LANES // HALF, 1)).T
    sn = jnp.sin(ang)
    sin = jnp.concatenate([-sn, -sn, sn, sn], axis=0).T
    r = proj(_QA, SWA_WIDTH)
    for c in range(SWA_WIDTH // LANES):
        rc = r[:, c * LANES:(c + 1) * LANES]
        t = rc * cos + pltpu.roll(rc, HEAD_DIM, 1) * sin
        qa_ref[0, :, c * LANES:(c + 1) * LANES] = (t * (QK_SCALE * LOG2E)).astype(BF16)
    r = proj(_KA, 2 * SWA_KV_WIDTH)
    for c in range(2 * SWA_KV_WIDTH // LANES):
        rc = r[:, c * LANES:(c + 1) * LANES]
        t = rc * cos + pltpu.roll(rc, HEAD_DIM, 1) * sin
        ka_ref[0, :, c * LANES:(c + 1) * LANES] = t.astype(BF16)

    rq = proj(_QB, FOX_WIDTH)
    rk = proj(_KB, FOX_WIDTH)
    for hd in range(FOX_HEADS):
        a = (hd // 2) * LANES
        gq = aug[:, a:a + LANES]
        gk = aug[:, FOX_WIDTH + a:FOX_WIDTH + a + LANES]
        qv = rq[:, a:a + LANES] * (QK_SCALE * LOG2E)
        kv = rk[:, a:a + LANES]
        if hd % 2 == 0:
            qaug_ref[0, hd] = jnp.where(low, qv, gq).astype(BF16)
            kaug_ref[0, hd] = jnp.where(low, kv, gk).astype(BF16)
        else:
            qaug_ref[0, hd] = jnp.where(low, gq, qv).astype(BF16)
            kaug_ref[0, hd] = jnp.where(low, gk, kv).astype(BF16)

    vt = lax.dot_general(wvt_ref[...], hb, (((1,), (1,)), ((), ())), preferred_element_type=F32)
    vat_ref[0] = vt[0:SWA_KV_WIDTH].astype(BF16)
    vbt_ref[0] = vt[SWA_KV_WIDTH:].astype(BF16)


def _aug_placement():
    e = np.zeros((LANES, 2 * FOX_HEADS * HEAD_DIM), np.float32)
    for hd in range(FOX_HEADS):
        base = (hd // 2) * LANES + (HEAD_DIM if hd % 2 == 0 else 0)
        kbase = FOX_HEADS * HEAD_DIM + base
        for part in range(3):
            e[part * FOX_HEADS + hd, base + part] = 1.0
            e[_ONE_LANE, base + 3 + part] = 1.0
            e[_ONE_LANE, kbase + part] = 1.0
            e[part * FOX_HEADS + hd, kbase + 3 + part] = -1.0
    return e


def _proj(x, pos_f, scale, shift, gn, invf, bf_rep, w_all, w_vt, e_mat, *, tm):
    b, s, _ = x.shape
    grid = (b, s // tm)
    row = lambda bi, i: (bi, i, 0)
    per_b = lambda bi, i: (bi, 0, 0)
    const2 = lambda bi, i: (0, 0)
    once = pl.Buffered(1)
    in_specs = [
        pl.BlockSpec((1, tm, D_MODEL), row),
        pl.BlockSpec((1, 1, tm), lambda bi, i: (bi, 0, i)),
        pl.BlockSpec((1, 1, D_MODEL), per_b),
        pl.BlockSpec((1, 1, D_MODEL), per_b),
        pl.BlockSpec((1, D_MODEL), const2),
        pl.BlockSpec((HALF, tm), const2),
        pl.BlockSpec((1, LANES), const2),
        pl.BlockSpec((D_MODEL, _PROJ_COLS), const2, pipeline_mode=once),
        pl.BlockSpec((SWA_KV_WIDTH + FOX_WIDTH, D_MODEL), const2, pipeline_mode=once),
        pl.BlockSpec((LANES, 2 * FOX_WIDTH), const2, pipeline_mode=once),
    ]
    head4 = lambda bi, i: (bi, 0, i, 0)
    out_specs = [
        pl.BlockSpec((1, tm, SWA_WIDTH), row),
        pl.BlockSpec((1, tm, 2 * SWA_KV_WIDTH), row),
        pl.BlockSpec((1, SWA_KV_WIDTH, tm), lambda bi, i: (bi, 0, i)),
        pl.BlockSpec((1, tm, SWA_WIDTH), row),
        pl.BlockSpec((1, FOX_HEADS, tm, LANES), head4),
        pl.BlockSpec((1, FOX_HEADS, tm, LANES), head4),
        pl.BlockSpec((1, FOX_WIDTH, tm), lambda bi, i: (bi, 0, i)),
        pl.BlockSpec((1, tm, FOX_WIDTH), row),
        pl.BlockSpec((1, tm, D_MODEL), row),
        pl.BlockSpec((1, tm, D_MODEL), row),
    ]
    sds = jax.ShapeDtypeStruct
    out_shape = [
        sds((b, s, SWA_WIDTH), BF16), sds((b, s, 2 * SWA_KV_WIDTH), BF16),
        sds((b, SWA_KV_WIDTH, s), BF16), sds((b, s, SWA_WIDTH), BF16),
        sds((b, FOX_HEADS, s, LANES), BF16), sds((b, FOX_HEADS, s, LANES), BF16),
        sds((b, FOX_WIDTH, s), BF16), sds((b, s, FOX_WIDTH), BF16),
        sds((b, s, D_MODEL), BF16), sds((b, s, D_MODEL), BF16),
    ]
    return pl.pallas_call(
        functools.partial(_proj_kernel, tm=tm),
        grid=grid, in_specs=in_specs, out_specs=out_specs, out_shape=out_shape,
        scratch_shapes=[pltpu.VMEM((1, LANES), F32)],
        compiler_params=pltpu.CompilerParams(
            dimension_semantics=("arbitrary", "arbitrary"),
            vmem_limit_bytes=_PROJ_VMEM),
        name="in_proj",
    )(x, pos_f, scale, shift, gn, invf, bf_rep, w_all, w_vt, e_mat)


def _swa_kernel(sinks_ref, q_ref, kc_ref, kp_ref, vtc_ref, vtp_ref, bias_ref, sz_ref, o_ref,
                kband, vtband, *, tq):
    i = pl.program_id(1)
    nblk = tq // WINDOW
    nhq = SWA_Q_HEADS // SWA_KV_HEADS
    kband[0:WINDOW] = kp_ref[0]
    kband[WINDOW:] = kc_ref[0]
    vtband[:, 0:WINDOW] = vtp_ref[0]
    vtband[:, WINDOW:] = vtc_ref[0]

    cols = nhq * WINDOW
    first_plane = jnp.where(i > 0, 0, 1)
    headid = lax.broadcasted_iota(jnp.int32, (1, cols), 1) // WINDOW
    low = lax.broadcasted_iota(jnp.int32, (WINDOW, LANES), 1) % HEAD_DIM < HALF
    ones = jnp.ones((_ONES_ROWS, 2 * WINDOW), BF16)
    nt = (((1,), (1,)), ((), ()))

    def logits(r, g):
        kb = kband[r * WINDOW:(r + 2) * WINDOW, g * LANES:(g + 1) * LANES]
        qs = []
        for c in range(2):
            a = (2 * g + c) * LANES
            qc = q_ref[0, r * WINDOW:(r + 1) * WINDOW, a:a + LANES]
            qs.append(jnp.where(low, qc, jnp.zeros_like(qc)))
            qs.append(jnp.where(low, jnp.zeros_like(qc), qc))
        qst = jnp.concatenate(qs, axis=0)
        s = lax.dot_general(kb, qst, nt, preferred_element_type=F32)
        return s + bias_ref[first_plane if r == 0 else 0]

    def attend(r, g, s):
        sink = jnp.zeros((1, cols), F32)
        for k in range(nhq):
            sink = jnp.where(headid == k, sinks_ref[g * nhq + k] * LOG2E, sink)
        m = jnp.maximum(jnp.max(s, axis=0, keepdims=True), sink)
        p = jnp.exp2(s - m).astype(BF16)
        vt = vtband[g * HEAD_DIM:(g + 1) * HEAD_DIM, r * WINDOW:(r + 2) * WINDOW]
        pv = jnp.dot(jnp.concatenate([vt, ones], axis=0), p, preferred_element_type=F32)
        den = pv[HEAD_DIM:HEAD_DIM + 1] + jnp.exp2(sink - m)
        o_t = pv[0:HEAD_DIM] / den
        for c in range(2):
            a = (2 * g + c) * LANES
            pair = jnp.concatenate([o_t[:, 2 * c * WINDOW:(2 * c + 1) * WINDOW],
                                    o_t[:, (2 * c + 1) * WINDOW:(2 * c + 2) * WINDOW]], axis=0)
            gate = sz_ref[0, r * WINDOW:(r + 1) * WINDOW, a:a + LANES].astype(F32)
            o_ref[0, r * WINDOW:(r + 1) * WINDOW, a:a + LANES] = (pair.T * gate).astype(BF16)

    work = [(r, g) for r in range(nblk) for g in range(SWA_KV_HEADS)]
    pending = [logits(*w) for w in work[:_SWA_AHEAD]]
    for n, (r, g) in enumerate(work):
        if n + _SWA_AHEAD < len(work):
            pending.append(logits(*work[n + _SWA_AHEAD]))
        attend(r, g, pending.pop(0))


def _swa(sinks, qa, ka, vat, sza, *, tq):
    b, s, _ = qa.shape
    kvw = 2 * SWA_KV_WIDTH
    per = tq // WINDOW
    cur = lambda bi, i: (bi, i, 0)
    prev = lambda bi, i: (bi, jnp.maximum(i * per - 1, 0), 0)
    cur_t = lambda bi, i: (bi, 0, i)
    prev_t = lambda bi, i: (bi, 0, jnp.maximum(i * per - 1, 0))
    key = np.arange(2 * WINDOW)[:, None]
    qt = np.tile(np.arange(WINDOW), SWA_Q_HEADS // SWA_KV_HEADS)[None, :]
    band = np.where(key < WINDOW, key > qt, (key - WINDOW) <= qt)
    planes = np.stack([band, band & (key >= WINDOW)])
    bias = jnp.asarray(np.where(planes, 0.0, -np.inf).astype(np.float32))
    return pl.pallas_call(
        functools.partial(_swa_kernel, tq=tq),
        grid=(b, s // tq),
        in_specs=[pl.BlockSpec(memory_space=pltpu.SMEM),
                  pl.BlockSpec((1, tq, SWA_WIDTH), cur),
                  pl.BlockSpec((1, tq, kvw), cur),
                  pl.BlockSpec((1, WINDOW, kvw), prev),
                  pl.BlockSpec((1, SWA_KV_WIDTH, tq), cur_t),
                  pl.BlockSpec((1, SWA_KV_WIDTH, WINDOW), prev_t),
                  pl.BlockSpec(bias.shape, lambda bi, i: (0, 0, 0), pipeline_mode=pl.Buffered(1)),
                  pl.BlockSpec((1, tq, SWA_WIDTH), cur)],
        out_specs=pl.BlockSpec((1, tq, SWA_WIDTH), cur),
        out_shape=jax.ShapeDtypeStruct((b, s, SWA_WIDTH), BF16),
        scratch_shapes=[pltpu.VMEM((tq + WINDOW, kvw), BF16),
                        pltpu.VMEM((SWA_KV_WIDTH, tq + WINDOW), BF16)],
        compiler_params=pltpu.CompilerParams(dimension_semantics=("arbitrary", "arbitrary")),
        name="swa_attn",
    )(sinks, qa, ka, ka, vat, vat, bias, sza)


_ACC_ROWS = HEAD_DIM + _ONES_ROWS


def _fox_schedule(n_rows):
    tiles = [(i, i) for i in range(n_rows)] + [(i, j) for i in range(n_rows) for j in range(i)]
    return np.asarray(tiles, np.int32).T.copy()


def _fox_kernel(tbl_ref, q_ref, k_ref, vt_ref, bias_ref, sz_ref, o_ref,
                s0, s1, mx0, mx1, p0, p1, a0, a1, m_ref, acc_ref, *, t, n_rows):
    nt = (((1,), (1,)), ((), ()))
    g = t // SUBLANES
    h = t // 2
    n_diag = n_rows
    n_tiles = n_rows * (n_rows + 1) // 2
    assert n_diag % 2 == 0 and n_tiles % 2 == 0 and n_diag >= 2 and n_tiles - n_diag >= 2

    def step(j, diag, s_w, mx_w, s_r, mx_r, p_w, a_w, p_r, a_r):
        if diag[0] is not None:
            qrow = pl.multiple_of(tbl_ref[0, j] * t, t)
            krow = pl.multiple_of(tbl_ref[1, j] * t, t)
            for hh in range(2):
                if diag[0]:
                    top = lax.dot_general(k_ref[0, hh, pl.ds(krow, h), :], q_ref[0, hh, pl.ds(qrow, t), :],
                                          nt, preferred_element_type=F32) + bias_ref[0:h, :]
                    bot = lax.dot_general(k_ref[0, hh, pl.ds(pl.multiple_of(krow + h, h), h), :],
                                          q_ref[0, hh, pl.ds(pl.multiple_of(qrow + h, h), h), :],
                                          nt, preferred_element_type=F32) + bias_ref[h:, h:]
                    s_w[hh, 0:h] = top
                    s_w[hh, h:, h:] = bot
                    top_mx = jnp.max(top.reshape(h // SUBLANES, SUBLANES, t), axis=0)
                    bot_mx = jnp.max(bot.reshape(h // SUBLANES, SUBLANES, h), axis=0)
                    mx_w[hh, :, 0:h] = top_mx[:, 0:h]
                    mx_w[hh, :, h:] = jnp.maximum(top_mx[:, h:], bot_mx)
                else:
                    s = lax.dot_general(k_ref[0, hh, pl.ds(krow, t), :], q_ref[0, hh, pl.ds(qrow, t), :],
                                        nt, preferred_element_type=F32)
                    s_w[hh] = s
                    mx_w[hh] = jnp.max(s.reshape(g, SUBLANES, t), axis=0)

        if diag[1] is not None:
            srow = tbl_ref[0, j - 1]
            for hh in range(2):
                m_tile = jnp.max(mx_r[hh], axis=0, keepdims=True)
                if diag[1]:
                    m_new = jnp.broadcast_to(m_tile, (SUBLANES, t))
                else:
                    m_old = m_ref[srow, hh]
                    m_new = jnp.maximum(m_old, m_tile)
                    a_w[hh] = jnp.exp2(m_old - m_new)
                m_ref[srow, hh] = m_new
                if diag[1]:
                    top = jnp.exp2(s_r[hh, 0:h].reshape(h // SUBLANES, SUBLANES, t) - m_new[None])
                    bot = jnp.exp2(s_r[hh, h:, h:].reshape(h // SUBLANES, SUBLANES, h) - m_new[None, :, h:])
                    p_w[hh, 0:h] = top.reshape(h, t).astype(BF16)
                    p_w[hh, h:, h:] = bot.reshape(h, h).astype(BF16)
                else:
                    p = jnp.exp2(s_r[hh].reshape(g, SUBLANES, t) - m_new[None])
                    p_w[hh] = p.reshape(t, t).astype(BF16)

        if diag[2] is not None:
            vcol = pl.multiple_of(tbl_ref[1, j - 2] * t, t)
            arow = tbl_ref[0, j - 2]
            for hh in range(2):
                vt = vt_ref[0, hh * HEAD_DIM:(hh + 1) * HEAD_DIM, pl.ds(vcol, t)]
                lhs = jnp.concatenate([vt, jnp.ones((_ONES_ROWS, t), BF16)], axis=0)
                if diag[2]:
                    acc_ref[arow, hh, :, 0:h] = jnp.dot(lhs[:, 0:h], p_r[hh, 0:h, 0:h],
                                                        preferred_element_type=F32)
                    acc_ref[arow, hh, :, h:] = jnp.dot(lhs, p_r[hh, :, h:], preferred_element_type=F32)
                else:
                    part = jnp.dot(lhs, p_r[hh], preferred_element_type=F32)
                    old = acc_ref[arow, hh].reshape(_ACC_ROWS // SUBLANES, SUBLANES, t)
                    acc_ref[arow, hh] = (a_r[hh][None] * old).reshape(_ACC_ROWS, t) + part

    def step_pair(j, diag_even, diag_odd):
        step(j, diag_even, s0, mx0, s1, mx1, p1, a1, p0, a0)
        step(j + 1, diag_odd, s1, mx1, s0, mx0, p0, a0, p1, a1)

    def steady(first, last, diag):
        def body(kk, carry):
            step_pair(first + 2 * kk, diag, diag)
            return carry

        lax.fori_loop(0, (last - first) // 2, body, 0)

    step_pair(0, (True, None, None), (True, True, None))
    steady(2, n_diag, (True, True, True))
    step_pair(n_diag, (False, True, True), (False, False, True))
    steady(n_diag + 2, n_tiles, (False, False, False))
    step_pair(n_tiles, (None, False, False), (None, None, False))

    def finish_row(i, carry):
        halves = []
        for hh in range(2):
            a = acc_ref[i, hh]
            halves.append(a[0:HEAD_DIM] / a[HEAD_DIM:HEAD_DIM + 1])
        o_t = jnp.concatenate(halves, axis=0)
        rows = pl.ds(pl.multiple_of(i * t, t), t)
        o_ref[0, rows, :] = (o_t.T * sz_ref[0, rows, :].astype(F32)).astype(BF16)
        return carry

    lax.fori_loop(0, n_rows, finish_row, 0)


def _fox(q_aug, k_aug, vbt, szb, *, t):
    b, _, s, _ = q_aug.shape
    npair = FOX_HEADS // 2
    n_rows = s // t
    tbl = _fox_schedule(n_rows)
    tri = np.triu(np.ones((t, t), bool))
    bias = jnp.asarray(np.where(tri, 0.0, -np.inf).astype(np.float32))
    pair = lambda bi, p: (bi, p, 0, 0)
    stat = pltpu.VMEM((2, SUBLANES, t), F32)
    return pl.pallas_call(
        functools.partial(_fox_kernel, t=t, n_rows=n_rows),
        grid=(b, npair),
        in_specs=[pl.BlockSpec(memory_space=pltpu.SMEM),
                  pl.BlockSpec((1, 2, s, LANES), pair),
                  pl.BlockSpec((1, 2, s, LANES), pair),
                  pl.BlockSpec((1, 2 * HEAD_DIM, s), lambda bi, p: (bi, p, 0)),
                  pl.BlockSpec((t, t), lambda bi, p: (0, 0), pipeline_mode=pl.Buffered(1)),
                  pl.BlockSpec((1, s, LANES), lambda bi, p: (bi, 0, p))],
        out_specs=pl.BlockSpec((1, s, LANES), lambda bi, p: (bi, 0, p)),
        out_shape=jax.ShapeDtypeStruct((b, s, FOX_WIDTH), BF16),
        scratch_shapes=[pltpu.VMEM((2, t, t), F32), pltpu.VMEM((2, t, t), F32),
                        stat, stat,
                        pltpu.VMEM((2, t, t), BF16), pltpu.VMEM((2, t, t), BF16),
                        stat, stat,
                        pltpu.VMEM((n_rows, 2, SUBLANES, t), F32),
                        pltpu.VMEM((n_rows, 2, _ACC_ROWS, t), F32)],
        compiler_params=pltpu.CompilerParams(
            dimension_semantics=("arbitrary", "arbitrary"),
            vmem_limit_bytes=_FOX_VMEM),
        name="fox_attn",
    )(jnp.asarray(tbl), q_aug, k_aug, vbt, bias, szb)


def _out_kernel(x_ref, ga_ref, gb_ref, sga_ref, sgb_ref, gate_ref,
                woa_ref, wob_ref, wout_ref, gf_ref, o_ref, merged_ref):
    chunks = [slice(c * _OUT_CHUNK, (c + 1) * _OUT_CHUNK) for c in range(D_MODEL // _OUT_CHUNK)]
    for r in range(x_ref.shape[1] // _OUT_SUB):
        rs = slice(r * _OUT_SUB, (r + 1) * _OUT_SUB)
        for cs in chunks:
            ya = jnp.dot(ga_ref[0, rs], woa_ref[:, cs], preferred_element_type=F32)
            yb = jnp.dot(gb_ref[0, rs], wob_ref[:, cs], preferred_element_type=F32)
            merged = sga_ref[0, rs, cs].astype(F32) * ya + sgb_ref[0, rs, cs].astype(F32) * yb
            merged_ref[rs, cs] = merged.astype(BF16)
        ss = jnp.zeros((_OUT_SUB, 1), F32)
        for cs in chunks:
            d = jnp.dot(merged_ref[rs, :], wout_ref[:, cs], preferred_element_type=F32)
            xo = x_ref[0, rs, cs] + gate_ref[0, :, cs] * d
            ss = ss + jnp.sum(xo * xo, axis=-1, keepdims=True)
            o_ref[0, rs, cs] = xo
        o_ref[0, rs] = o_ref[0, rs] * lax.rsqrt(ss * (1.0 / D_MODEL) + NORM_EPS) * gf_ref[...]


def _out(x, gated_a, gated_b, sga, sgb, gate, woa, wob, wout, gf, *, tm):
    b, s, _ = x.shape
    row = lambda bi, i: (bi, i, 0)
    per_b = lambda bi, i: (bi, 0, 0)
    const2 = lambda bi, i: (0, 0)
    once = pl.Buffered(1)
    return pl.pallas_call(
        _out_kernel,
        grid=(b, s // tm),
        in_specs=[pl.BlockSpec((1, tm, D_MODEL), row),
                  pl.BlockSpec((1, tm, SWA_WIDTH), row), pl.BlockSpec((1, tm, FOX_WIDTH), row),
                  pl.BlockSpec((1, tm, D_MODEL), row), pl.BlockSpec((1, tm, D_MODEL), row),
                  pl.BlockSpec((1, 1, D_MODEL), per_b),
                  pl.BlockSpec((SWA_WIDTH, D_MODEL), const2, pipeline_mode=once),
                  pl.BlockSpec((FOX_WIDTH, D_MODEL), const2, pipeline_mode=once),
                  pl.BlockSpec((D_MODEL, D_MODEL), const2, pipeline_mode=once),
                  pl.BlockSpec((1, D_MODEL), const2)],
        out_specs=pl.BlockSpec((1, tm, D_MODEL), row),
        out_shape=jax.ShapeDtypeStruct((b, s, D_MODEL), F32),
        scratch_shapes=[pltpu.VMEM((tm, D_MODEL), BF16)],
        compiler_params=pltpu.CompilerParams(
            dimension_semantics=("arbitrary", "arbitrary"),
            vmem_limit_bytes=_OUT_VMEM),
        name="out_proj",
    )(x, gated_a, gated_b, sga, sgb, gate, woa, wob, wout, gf)


def _pair_halves(w, nheads, dup):
    rows = w.shape[0]
    if dup:
        w5 = jnp.broadcast_to(w.reshape(rows, nheads, 1, 2, HALF), (rows, nheads, 2, 2, HALF))
    else:
        w5 = w.reshape(rows, nheads // 2, 2, 2, HALF)
    return w5.transpose(0, 1, 3, 2, 4).reshape(rows, -1)


def _layout_w_in(w):
    o = np.cumsum([0, SWA_WIDTH, SWA_KV_WIDTH, SWA_KV_WIDTH, SWA_WIDTH, FOX_WIDTH, FOX_WIDTH, FOX_WIDTH,
                   FOX_HEADS, FOX_WIDTH, D_MODEL, D_MODEL]).tolist()
    qa, ka, va, za, qb, kb, vb, fb, zb, ga, gb = [w[:, o[k]:o[k + 1]] for k in range(11)]
    cols = [_pair_halves(qa, SWA_Q_HEADS, False), _pair_halves(ka, SWA_KV_HEADS, True),
            za, qb, kb, jnp.tile(fb, (1, LANES // FOX_HEADS)), zb, ga, gb]
    w_all = jnp.concatenate(cols, axis=1).astype(BF16)
    assert w_all.shape[1] == _PROJ_COLS
    return w_all, jnp.concatenate([va, vb], axis=1).T.astype(BF16)


def kernel(x, c, positions, w_ada, b_ada, g_norm, w_in, b_f, sinks, w_o_swa, w_o_fox, w_out, g_final):
    b, s, _ = x.shape
    depth = w_in.shape[0]
    assert depth == 1, "the output stage fuses the final RMSNorm into the single layer"
    inv_freq = ROPE_THETA ** (-jnp.arange(0, HEAD_DIM, 2, dtype=F32) / HEAD_DIM)
    invf = jnp.broadcast_to(inv_freq[:, None], (HALF, _PROJ_TM))
    pos_f = positions.astype(F32)[:, None, :]
    e_mat = jnp.asarray(_aug_placement(), BF16)
    c_pad = jnp.zeros((SUBLANES, D_MODEL), F32).at[:b].set(c)
    for l in range(depth):
        ada = _ada(c_pad, w_ada[l], b_ada[l][None, :])[:b]
        shift, scale, gate = [ada[:, None, k * D_MODEL:(k + 1) * D_MODEL] for k in range(3)]
        w_all, w_vt = _layout_w_in(w_in[l])
        bf_rep = jnp.tile(b_f[l].astype(F32), LANES // FOX_HEADS)[None, :]
        qa, ka, vat, sza, q_aug, k_aug, vbt, szb, sga, sgb = _proj(
            x, pos_f, scale, shift, g_norm[l][None, :], invf, bf_rep, w_all, w_vt, e_mat, tm=_PROJ_TM)
        gated_a = _swa(sinks[l].astype(F32), qa, ka, vat, sza, tq=_SWA_TQ)
        gated_b = _fox(q_aug, k_aug, vbt, szb, t=_FOX_T)
        x = _out(x, gated_a, gated_b, sga, sgb, gate,
                 w_o_swa[l].astype(BF16), w_o_fox[l].astype(BF16), w_out[l].astype(BF16),
                 g_final[None, :], tm=_OUT_TM)
    return x
```

```python
import functools
import math

import numpy as np
import jax
import jax.numpy as jnp
from jax import lax
from jax.experimental import pallas as pl
from jax.experimental.pallas import tpu as pltpu

D_MODEL = 1024
HEAD_DIM = 64
HALF = HEAD_DIM // 2
SWA_Q_HEADS = 8
SWA_KV_HEADS = 2
SWA_WIDTH = SWA_Q_HEADS * HEAD_DIM
SWA_KV_WIDTH = SWA_KV_HEADS * HEAD_DIM
FOX_HEADS = 8
FOX_WIDTH = FOX_HEADS * HEAD_DIM
WINDOW = 128
ROPE_THETA = 10000.0
NORM_EPS = 1e-6
QK_SCALE = HEAD_DIM ** -0.5
LOG2E = math.log2(math.e)

LANES = 128
SUBLANES = 8
MIB = 1024 * 1024
V7X_VMEM_BYTES = 64 * MIB

_PROJ_TM = 512
_SWA_TQ = 1024
_FOX_T = 512
_OUT_TM = 1024

F32 = jnp.float32
BF16 = jnp.bfloat16

_QA, _KA, _ZA = 0, 512, 768
_QB, _KB, _FB, _ZB, _GA, _GB = 1280, 1792, 2304, 2432, 2944, 3968
_PROJ_COLS = 4992

_ONE_LANE = 24

_SWA_AHEAD = 2
_ONES_ROWS = 16


_PROJ_VMEM = 56 * MIB
_FOX_VMEM = 56 * MIB
_OUT_VMEM = 48 * MIB
assert max(_PROJ_VMEM, _FOX_VMEM, _OUT_VMEM) <= V7X_VMEM_BYTES - 8 * MIB


def _split3(v):
    hi = v.astype(BF16)
    r1 = v - hi.astype(F32)
    mid = r1.astype(BF16)
    lo = (r1 - mid.astype(F32)).astype(BF16)
    return hi, mid, lo


def _ada_kernel(c_ref, w_ref, b_ref, o_ref):
    o_ref[...] = jnp.dot(c_ref[...], w_ref[...], preferred_element_type=F32,
                         precision=lax.Precision.HIGHEST) + b_ref[...]


def _ada(c_pad, w_ada, b_ada):
    rows = c_pad.shape[0]
    n = w_ada.shape[1]
    nblk = n // D_MODEL
    return pl.pallas_call(
        _ada_kernel,
        grid=(nblk,),
        in_specs=[pl.BlockSpec((rows, D_MODEL), lambda j: (0, 0)),
                  pl.BlockSpec((D_MODEL, D_MODEL), lambda j: (0, j)),
                  pl.BlockSpec((1, D_MODEL), lambda j: (0, j))],
        out_specs=pl.BlockSpec((rows, D_MODEL), lambda j: (0, j)),
        out_shape=jax.ShapeDtypeStruct((rows, n), F32),
        name="ada_mod",
    )(c_pad, w_ada, b_ada)


def _sigmoid(z):
    return 0.5 * jnp.tanh(0.5 * z) + 0.5


def _proj_kernel(x_ref, pos_ref, scale_ref, shift_ref, gn_ref, invf_ref, bf_ref, w_ref, wvt_ref, e_ref,
                 qa_ref, ka_ref, vat_ref, sza_ref, qaug_ref, kaug_ref, vbt_ref, szb_ref,
                 sga_ref, sgb_ref, carry_ref, *, tm):
    i = pl.program_id(1)

    @pl.when(i == 0)
    def _():
        carry_ref[...] = jnp.zeros_like(carry_ref)

    x = x_ref[0]
    ms = jnp.mean(x * x, axis=-1, keepdims=True)
    y = x * lax.rsqrt(ms + NORM_EPS) * gn_ref[...]
    h = y * (1.0 + scale_ref[0]) + shift_ref[0]
    hb = h.astype(BF16)

    def proj(off, n):
        return jnp.dot(hb, w_ref[:, off:off + n], preferred_element_type=F32)

    lane = lax.broadcasted_iota(jnp.int32, (tm, LANES), 1)
    low = lane < HEAD_DIM


    zf = proj(_FB, LANES) + bf_ref[...]
    logf = jnp.minimum(zf, 0.0) - jnp.log1p(jnp.exp(-jnp.abs(zf)))
    grp = lane // FOX_HEADS
    hi, mid, lo = _split3(logf)
    zero = jnp.zeros((tm, LANES), F32)
    parts = jnp.where(grp == 0, hi.astype(F32),
                      jnp.where(grp == 1, mid.astype(F32),
                                jnp.where(grp == 2, lo.astype(F32), zero))).astype(BF16)
    tri = (lax.broadcasted_iota(jnp.int32, (tm, tm), 0)
           >= lax.broadcasted_iota(jnp.int32, (tm, tm), 1)).astype(BF16)
    sga_ref[0] = _sigmoid(proj(_GA, D_MODEL)).astype(BF16)
    rsum = jnp.dot(tri, parts, preferred_element_type=F32)
    c0 = rsum + pltpu.roll(rsum, LANES - FOX_HEADS, 1) + pltpu.roll(rsum, LANES - 2 * FOX_HEADS, 1)
    c0 = jnp.where(grp == 0, c0, zero)
    cl = c0 + pltpu.roll(c0, FOX_HEADS, 1) + pltpu.roll(c0, 2 * FOX_HEADS, 1)
    cum = cl + carry_ref[...]
    carry_ref[...] = cum[tm - 1:tm, :]

    hi, mid, lo = _split3(cum * LOG2E)
    one = jnp.ones((tm, LANES), F32)
    carrier = jnp.where(grp == 0, hi.astype(F32),
                        jnp.where(grp == 1, mid.astype(F32),
                                  jnp.where(grp == 2, lo.astype(F32),
                                            jnp.where(lane == _ONE_LANE, one, zero)))).astype(BF16)
    sgb_ref[0] = _sigmoid(proj(_GB, D_MODEL)).astype(BF16)
    aug = jnp.dot(carrier, e_ref[...], preferred_element_type=F32)

    z = proj(_ZA, SWA_WIDTH)
    sza_ref[0] = (z * _sigmoid(z)).astype(BF16)
    z = proj(_ZB, FOX_WIDTH)
    szb_ref[0] = (z * _sigmoid(z)).astype(BF16)

    ang = invf_ref[...] * pos_ref[0]
    cos = jnp.tile(jnp.cos(ang), (LANES // HALF, 1)).T
    sn = jnp.sin(ang)
    sin = jnp.concatenate([-sn, -sn, sn, sn], axis=0).T
    r = proj(_QA, SWA_WIDTH)
    for c in range(SWA_WIDTH // LANES):
        rc = r[:, c * LANES:(c + 1) * LANES]
        t = rc * cos + pltpu.roll(rc, HEAD_DIM, 1) * sin
        qa_ref[0, :, c * LANES:(c + 1) * LANES] = (t * (QK_SCALE * LOG2E)).astype(BF16)
    r = proj(_KA, 2 * SWA_KV_WIDTH)
    for c in range(2 * SWA_KV_WIDTH // LANES):
        rc = r[:, c * LANES:(c + 1) * LANES]
        t = rc * cos + pltpu.roll(rc, HEAD_DIM, 1) * sin
        ka_ref[0, :, c * LANES:(c + 1) * LANES] = t.astype(BF16)

    rq = proj(_QB, FOX_WIDTH)
    rk = proj(_KB, FOX_WIDTH)
    for hd in range(FOX_HEADS):
        a = (hd // 2) * LANES
        gq = aug[:, a:a + LANES]
        gk = aug[:, FOX_WIDTH + a:FOX_WIDTH + a + LANES]
        qv = rq[:, a:a + LANES] * (QK_SCALE * LOG2E)
        kv = rk[:, a:a + LANES]
        if hd % 2 == 0:
            qaug_ref[0, hd] = jnp.where(low, qv, gq).astype(BF16)
            kaug_ref[0, hd] = jnp.where(low, kv, gk).astype(BF16)
        else:
            qaug_ref[0, hd] = jnp.where(low, gq, qv).astype(BF16)
            kaug_ref[0, hd] = jnp.where(low, gk, kv).astype(BF16)

    vt = lax.dot_general(wvt_ref[...], hb, (((1,), (1,)), ((), ())), preferred_element_type=F32)
    vat_ref[0] = vt[0:SWA_KV_WIDTH].astype(BF16)
    vbt_ref[0] = vt[SWA_KV_WIDTH:].astype(BF16)


def _aug_placement():
    e = np.zeros((LANES, 2 * FOX_HEADS * HEAD_DIM), np.float32)
    for hd in range(FOX_HEADS):
        base = (hd // 2) * LANES + (HEAD_DIM if hd % 2 == 0 else 0)
        kbase = FOX_HEADS * HEAD_DIM + base
        for part in range(3):
            e[part * FOX_HEADS + hd, base + part] = 1.0
            e[_ONE_LANE, base + 3 + part] = 1.0
            e[_ONE_LANE, kbase + part] = 1.0
            e[part * FOX_HEADS + hd, kbase + 3 + part] = -1.0
    return e


def _proj(x, pos_f, scale, shift, gn, invf, bf_rep, w_all, w_vt, e_mat, *, tm):
    b, s, _ = x.shape
    grid = (b, s // tm)
    row = lambda bi, i: (bi, i, 0)
    per_b = lambda bi, i: (bi, 0, 0)
    const2 = lambda bi, i: (0, 0)
    once = pl.Buffered(1)
    in_specs = [
        pl.BlockSpec((1, tm, D_MODEL), row),
        pl.BlockSpec((1, 1, tm), lambda bi, i: (bi, 0, i)),
        pl.BlockSpec((1, 1, D_MODEL), per_b),
        pl.BlockSpec((1, 1, D_MODEL), per_b),
        pl.BlockSpec((1, D_MODEL), const2),
        pl.BlockSpec((HALF, tm), const2),
        pl.BlockSpec((1, LANES), const2),
        pl.BlockSpec((D_MODEL, _PROJ_COLS), const2, pipeline_mode=once),
        pl.BlockSpec((SWA_KV_WIDTH + FOX_WIDTH, D_MODEL), const2, pipeline_mode=once),
        pl.BlockSpec((LANES, 2 * FOX_WIDTH), const2, pipeline_mode=once),
    ]
    head4 = lambda bi, i: (bi, 0, i, 0)
    out_specs = [
        pl.BlockSpec((1, tm, SWA_WIDTH), row),
        pl.BlockSpec((1, tm, 2 * SWA_KV_WIDTH), row),
        pl.BlockSpec((1, SWA_KV_WIDTH, tm), lambda bi, i: (bi, 0, i)),
        pl.BlockSpec((1, tm, SWA_WIDTH), row),
        pl.BlockSpec((1, FOX_HEADS, tm, LANES), head4),
        pl.BlockSpec((1, FOX_HEADS, tm, LANES), head4),
        pl.BlockSpec((1, FOX_WIDTH, tm), lambda bi, i: (bi, 0, i)),
        pl.BlockSpec((1, tm, FOX_WIDTH), row),
        pl.BlockSpec((1, tm, D_MODEL), row),
        pl.BlockSpec((1, tm, D_MODEL), row),
    ]
    sds = jax.ShapeDtypeStruct
    out_shape = [
        sds((b, s, SWA_WIDTH), BF16), sds((b, s, 2 * SWA_KV_WIDTH), BF16),
        sds((b, SWA_KV_WIDTH, s), BF16), sds((b, s, SWA_WIDTH), BF16),
        sds((b, FOX_HEADS, s, LANES), BF16), sds((b, FOX_HEADS, s, LANES), BF16),
        sds((b, FOX_WIDTH, s), BF16), sds((b, s, FOX_WIDTH), BF16),
        sds((b, s, D_MODEL), BF16), sds((b, s, D_MODEL), BF16),
    ]
    return pl.pallas_call(
        functools.partial(_proj_kernel, tm=tm),
        grid=grid, in_specs=in_specs, out_specs=out_specs, out_shape=out_shape,
        scratch_shapes=[pltpu.VMEM((1, LANES), F32)],
        compiler_params=pltpu.CompilerParams(
            dimension_semantics=("arbitrary", "arbitrary"),
            vmem_limit_bytes=_PROJ_VMEM),
        name="in_proj",
    )(x, pos_f, scale, shift, gn, invf, bf_rep, w_all, w_vt, e_mat)


def _swa_kernel(sinks_ref, q_ref, kc_ref, kp_ref, vtc_ref, vtp_ref, bias_ref, sz_ref, o_ref,
                kband, vtband, *, tq):
    i = pl.program_id(1)
    nblk = tq // WINDOW
    nhq = SWA_Q_HEADS // SWA_KV_HEADS
    kband[0:WINDOW] = kp_ref[0]
    kband[WINDOW:] = kc_ref[0]
    vtband[:, 0:WINDOW] = vtp_ref[0]
    vtband[:, WINDOW:] = vtc_ref[0]

    cols = nhq * WINDOW
    first_plane = jnp.where(i > 0, 0, 1)
    headid = lax.broadcasted_iota(jnp.int32, (1, cols), 1) // WINDOW
    low = lax.broadcasted_iota(jnp.int32, (WINDOW, LANES), 1) % HEAD_DIM < HALF
    ones = jnp.ones((_ONES_ROWS, 2 * WINDOW), BF16)
    nt = (((1,), (1,)), ((), ()))

    def logits(r, g):
        kb = kband[r * WINDOW:(r + 2) * WINDOW, g * LANES:(g + 1) * LANES]
        qs = []
        for c in range(2):
            a = (2 * g + c) * LANES
            qc = q_ref[0, r * WINDOW:(r + 1) * WINDOW, a:a + LANES]
            qs.append(jnp.where(low, qc, jnp.zeros_like(qc)))
            qs.append(jnp.where(low, jnp.zeros_like(qc), qc))
        qst = jnp.concatenate(qs, axis=0)
        s = lax.dot_general(kb, qst, nt, preferred_element_type=F32)
        return s + bias_ref[first_plane if r == 0 else 0]

    def attend(r, g, s):
        sink = jnp.zeros((1, cols), F32)
        for k in range(nhq):
            sink = jnp.where(headid == k, sinks_ref[g * nhq + k] * LOG2E, sink)
        m = jnp.maximum(jnp.max(s, axis=0, keepdims=True), sink)
        p = jnp.exp2(s - m).astype(BF16)
        vt = vtband[g * HEAD_DIM:(g + 1) * HEAD_DIM, r * WINDOW:(r + 2) * WINDOW]
        pv = jnp.dot(jnp.concatenate([vt, ones], axis=0), p, preferred_element_type=F32)
        den = pv[HEAD_DIM:HEAD_DIM + 1] + jnp.exp2(sink - m)
        o_t = pv[0:HEAD_DIM] / den
        for c in range(2):
            a = (2 * g + c) * LANES
            pair = jnp.concatenate([o_t[:, 2 * c * WINDOW:(2 * c + 1) * WINDOW],
                                    o_t[:, (2 * c + 1) * WINDOW:(2 * c + 2) * WINDOW]], axis=0)
            gate = sz_ref[0, r * WINDOW:(r + 1) * WINDOW, a:a + LANES].astype(F32)
            o_ref[0, r * WINDOW:(r + 1) * WINDOW, a:a + LANES] = (pair.T * gate).astype(BF16)

    work = [(r, g) for r in range(nblk) for g in range(SWA_KV_HEADS)]
    pending = [logits(*w) for w in work[:_SWA_AHEAD]]
    for n, (r, g) in enumerate(work):
        if n + _SWA_AHEAD < len(work):
            pending.append(logits(*work[n + _SWA_AHEAD]))
        attend(r, g, pending.pop(0))


def _swa(sinks, qa, ka, vat, sza, *, tq):
    b, s, _ = qa.shape
    kvw = 2 * SWA_KV_WIDTH
    per = tq // WINDOW
    cur = lambda bi, i: (bi, i, 0)
    prev = lambda bi, i: (bi, jnp.maximum(i * per - 1, 0), 0)
    cur_t = lambda bi, i: (bi, 0, i)
    prev_t = lambda bi, i: (bi, 0, jnp.maximum(i * per - 1, 0))
    key = np.arange(2 * WINDOW)[:, None]
    qt = np.tile(np.arange(WINDOW), SWA_Q_HEADS // SWA_KV_HEADS)[None, :]
    band = np.where(key < WINDOW, key > qt, (key - WINDOW) <= qt)
    planes = np.stack([band, band & (key >= WINDOW)])
    bias = jnp.asarray(np.where(planes, 0.0, -np.inf).astype(np.float32))
    return pl.pallas_call(
        functools.partial(_swa_kernel, tq=tq),
        grid=(b, s // tq),
        in_specs=[pl.BlockSpec(memory_space=pltpu.SMEM),
                  pl.BlockSpec((1, tq, SWA_WIDTH), cur),
                  pl.BlockSpec((1, tq, kvw), cur),
                  pl.BlockSpec((1, WINDOW, kvw), prev),
                  pl.BlockSpec((1, SWA_KV_WIDTH, tq), cur_t),
                  pl.BlockSpec((1, SWA_KV_WIDTH, WINDOW), prev_t),
                  pl.BlockSpec(bias.shape, lambda bi, i: (0, 0, 0), pipeline_mode=pl.Buffered(1)),
                  pl.BlockSpec((1, tq, SWA_WIDTH), cur)],
        out_specs=pl.BlockSpec((1, tq, SWA_WIDTH), cur),
        out_shape=jax.ShapeDtypeStruct((b, s, SWA_WIDTH), BF16),
        scratch_shapes=[pltpu.VMEM((tq + WINDOW, kvw), BF16),
                        pltpu.VMEM((SWA_KV_WIDTH, tq + WINDOW), BF16)],
        compiler_params=pltpu.CompilerParams(dimension_semantics=("arbitrary", "arbitrary")),
        name="swa_attn",
    )(sinks, qa, ka, ka, vat, vat, bias, sza)


_ACC_ROWS = HEAD_DIM + _ONES_ROWS


def _fox_schedule(n_rows):
    tiles = [(i, i) for i in range(n_rows)] + [(i, j) for i in range(n_rows) for j in range(i)]
    return np.asarray(tiles, np.int32).T.copy()


def _fox_kernel(tbl_ref, q_ref, k_ref, vt_ref, bias_ref, sz_ref, o_ref,
                s0, s1, mx0, mx1, p0, p1, a0, a1, m_ref, acc_ref, *, t, n_rows):
    nt = (((1,), (1,)), ((), ()))
    g = t // SUBLANES
    h = t // 2
    n_diag = n_rows
    n_tiles = n_rows * (n_rows + 1) // 2
    assert n_diag % 2 == 0 and n_tiles % 2 == 0 and n_diag >= 2 and n_tiles - n_diag >= 2

    def step(j, diag, s_w, mx_w, s_r, mx_r, p_w, a_w, p_r, a_r):
        if diag[0] is not None:
            qrow = pl.multiple_of(tbl_ref[0, j] * t, t)
            krow = pl.multiple_of(tbl_ref[1, j] * t, t)
            for hh in range(2):
                if diag[0]:
                    top = lax.dot_general(k_ref[0, hh, pl.ds(krow, h), :], q_ref[0, hh, pl.ds(qrow, t), :],
                                          nt, preferred_element_type=F32) + bias_ref[0:h, :]
                    bot = lax.dot_general(k_ref[0, hh, pl.ds(pl.multiple_of(krow + h, h), h), :],
                                          q_ref[0, hh, pl.ds(pl.multiple_of(qrow + h, h), h), :],
                                          nt, preferred_element_type=F32) + bias_ref[h:, h:]
                    s_w[hh, 0:h] = top
                    s_w[hh, h:, h:] = bot
                    top_mx = jnp.max(top.reshape(h // SUBLANES, SUBLANES, t), axis=0)
                    bot_mx = jnp.max(bot.reshape(h // SUBLANES, SUBLANES, h), axis=0)
                    mx_w[hh, :, 0:h] = top_mx[:, 0:h]
                    mx_w[hh, :, h:] = jnp.maximum(top_mx[:, h:], bot_mx)
                else:
                    s = lax.dot_general(k_ref[0, hh, pl.ds(krow, t), :], q_ref[0, hh, pl.ds(qrow, t), :],
                                        nt, preferred_element_type=F32)
                    s_w[hh] = s
                    mx_w[hh] = jnp.max(s.reshape(g, SUBLANES, t), axis=0)

        if diag[1] is not None:
            srow = tbl_ref[0, j - 1]
            for hh in range(2):
                m_tile = jnp.max(mx_r[hh], axis=0, keepdims=True)
                if diag[1]:
                    m_new = jnp.broadcast_to(m_tile, (SUBLANES, t))
                else:
                    m_old = m_ref[srow, hh]
                    m_new = jnp.maximum(m_old, m_tile)
                    a_w[hh] = jnp.exp2(m_old - m_new)
                m_ref[srow, hh] = m_new
                if diag[1]:
                    top = jnp.exp2(s_r[hh, 0:h].reshape(h // SUBLANES, SUBLANES, t) - m_new[None])
                    bot = jnp.exp2(s_r[hh, h:, h:].reshape(h // SUBLANES, SUBLANES, h) - m_new[None, :, h:])
                    p_w[hh, 0:h] = top.reshape(h, t).astype(BF16)
                    p_w[hh, h:, h:] = bot.reshape(h, h).astype(BF16)
                else:
                    p = jnp.exp2(s_r[hh].reshape(g, SUBLANES, t) - m_new[None])
                    p_w[hh] = p.reshape(t, t).astype(BF16)

        if diag[2] is not None:
            vcol = pl.multiple_of(tbl_ref[1, j - 2] * t, t)
            arow = tbl_ref[0, j - 2]
            for hh in range(2):
                vt = vt_ref[0, hh * HEAD_DIM:(hh + 1) * HEAD_DIM, pl.ds(vcol, t)]
                lhs = jnp.concatenate([vt, jnp.ones((_ONES_ROWS, t), BF16)], axis=0)
                if diag[2]:
                    acc_ref[arow, hh, :, 0:h] = jnp.dot(lhs[:, 0:h], p_r[hh, 0:h, 0:h],
                                                        preferred_element_type=F32)
                    acc_ref[arow, hh, :, h:] = jnp.dot(lhs, p_r[hh, :, h:], preferred_element_type=F32)
                else:
                    part = jnp.dot(lhs, p_r[hh], preferred_element_type=F32)
                    old = acc_ref[arow, hh].reshape(_ACC_ROWS // SUBLANES, SUBLANES, t)
                    acc_ref[arow, hh] = (a_r[hh][None] * old).reshape(_ACC_ROWS, t) + part

    def step_pair(j, diag_even, diag_odd):
        step(j, diag_even, s0, mx0, s1, mx1, p1, a1, p0, a0)
        step(j + 1, diag_odd, s1, mx1, s0, mx0, p0, a0, p1, a1)

    def steady(first, last, diag):
        def body(kk, carry):
            step_pair(first + 2 * kk, diag, diag)
            return carry

        lax.fori_loop(0, (last - first) // 2, body, 0)

    step_pair(0, (True, None, None), (True, True, None))
    steady(2, n_diag, (True, True, True))
    step_pair(n_diag, (False, True, True), (False, False, True))
    steady(n_diag + 2, n_tiles, (False, False, False))
    step_pair(n_tiles, (None, False, False), (None, None, False))

    def finish_row(i, carry):
        halves = []
        for hh in range(2):
            a = acc_ref[i, hh]
            halves.append(a[0:HEAD_DIM] / a[HEAD_DIM:HEAD_DIM + 1])
        o_t = jnp.concatenate(halves, axis=0)
        rows = pl.ds(pl.multiple_of(i * t, t), t)
        o_ref[0, rows, :] = (o_t.T * sz_ref[0, rows, :].astype(F32)).astype(BF16)
        return carry

    lax.fori_loop(0, n_rows, finish_row, 0, unroll=8)


def _fox(q_aug, k_aug, vbt, szb, *, t):
    b, _, s, _ = q_aug.shape
    npair = FOX_HEADS // 2
    n_rows = s // t
    tbl = _fox_schedule(n_rows)
    tri = np.triu(np.ones((t, t), bool))
    bias = jnp.asarray(np.where(tri, 0.0, -np.inf).astype(np.float32))
    pair = lambda bi, p: (bi, p, 0, 0)
    stat = pltpu.VMEM((2, SUBLANES, t), F32)
    return pl.pallas_call(
        functools.partial(_fox_kernel, t=t, n_rows=n_rows),
        grid=(b, npair),
        in_specs=[pl.BlockSpec(memory_space=pltpu.SMEM),
                  pl.BlockSpec((1, 2, s, LANES), pair),
                  pl.BlockSpec((1, 2, s, LANES), pair),
                  pl.BlockSpec((1, 2 * HEAD_DIM, s), lambda bi, p: (bi, p, 0)),
                  pl.BlockSpec((t, t), lambda bi, p: (0, 0), pipeline_mode=pl.Buffered(1)),
                  pl.BlockSpec((1, s, LANES), lambda bi, p: (bi, 0, p))],
        out_specs=pl.BlockSpec((1, s, LANES), lambda bi, p: (bi, 0, p)),
        out_shape=jax.ShapeDtypeStruct((b, s, FOX_WIDTH), BF16),
        scratch_shapes=[pltpu.VMEM((2, t, t), F32), pltpu.VMEM((2, t, t), F32),
                        stat, stat,
                        pltpu.VMEM((2, t, t), BF16), pltpu.VMEM((2, t, t), BF16),
                        stat, stat,
                        pltpu.VMEM((n_rows, 2, SUBLANES, t), F32),
                        pltpu.VMEM((n_rows, 2, _ACC_ROWS, t), F32)],
        compiler_params=pltpu.CompilerParams(
            dimension_semantics=("arbitrary", "arbitrary"),
            vmem_limit_bytes=_FOX_VMEM),
        name="fox_attn",
    )(jnp.asarray(tbl), q_aug, k_aug, vbt, bias, szb)


def _out_kernel(x_ref, ga_ref, gb_ref, sga_ref, sgb_ref, gate_ref,
                woa_ref, wob_ref, wout_ref, gf_ref, o_ref):
    ya = jnp.dot(ga_ref[0], woa_ref[...], preferred_element_type=F32)
    yb = jnp.dot(gb_ref[0], wob_ref[...], preferred_element_type=F32)
    merged = sga_ref[0].astype(F32) * ya + sgb_ref[0].astype(F32) * yb
    d = jnp.dot(merged.astype(BF16), wout_ref[...], preferred_element_type=F32)
    xo = x_ref[0] + gate_ref[0] * d
    ms = jnp.mean(xo * xo, axis=-1, keepdims=True)
    o_ref[0] = xo * lax.rsqrt(ms + NORM_EPS) * gf_ref[...]


def _out(x, gated_a, gated_b, sga, sgb, gate, woa, wob, wout, gf, *, tm):
    b, s, _ = x.shape
    row = lambda bi, i: (bi, i, 0)
    per_b = lambda bi, i: (bi, 0, 0)
    const2 = lambda bi, i: (0, 0)
    once = pl.Buffered(1)
    return pl.pallas_call(
        _out_kernel,
        grid=(b, s // tm),
        in_specs=[pl.BlockSpec((1, tm, D_MODEL), row),
                  pl.BlockSpec((1, tm, SWA_WIDTH), row), pl.BlockSpec((1, tm, FOX_WIDTH), row),
                  pl.BlockSpec((1, tm, D_MODEL), row), pl.BlockSpec((1, tm, D_MODEL), row),
                  pl.BlockSpec((1, 1, D_MODEL), per_b),
                  pl.BlockSpec((SWA_WIDTH, D_MODEL), const2, pipeline_mode=once),
                  pl.BlockSpec((FOX_WIDTH, D_MODEL), const2, pipeline_mode=once),
                  pl.BlockSpec((D_MODEL, D_MODEL), const2, pipeline_mode=once),
                  pl.BlockSpec((1, D_MODEL), const2)],
        out_specs=pl.BlockSpec((1, tm, D_MODEL), row),
        out_shape=jax.ShapeDtypeStruct((b, s, D_MODEL), F32),
        compiler_params=pltpu.CompilerParams(
            dimension_semantics=("arbitrary", "arbitrary"),
            vmem_limit_bytes=_OUT_VMEM),
        name="out_proj",
    )(x, gated_a, gated_b, sga, sgb, gate, woa, wob, wout, gf)


def _pair_halves(w, nheads, dup):
    rows = w.shape[0]
    if dup:
        w5 = jnp.broadcast_to(w.reshape(rows, nheads, 1, 2, HALF), (rows, nheads, 2, 2, HALF))
    else:
        w5 = w.reshape(rows, nheads // 2, 2, 2, HALF)
    return w5.transpose(0, 1, 3, 2, 4).reshape(rows, -1)


def _layout_w_in(w):
    o = np.cumsum([0, SWA_WIDTH, SWA_KV_WIDTH, SWA_KV_WIDTH, SWA_WIDTH, FOX_WIDTH, FOX_WIDTH, FOX_WIDTH,
                   FOX_HEADS, FOX_WIDTH, D_MODEL, D_MODEL]).tolist()
    qa, ka, va, za, qb, kb, vb, fb, zb, ga, gb = [w[:, o[k]:o[k + 1]] for k in range(11)]
    cols = [_pair_halves(qa, SWA_Q_HEADS, False), _pair_halves(ka, SWA_KV_HEADS, True),
            za, qb, kb, jnp.tile(fb, (1, LANES // FOX_HEADS)), zb, ga, gb]
    w_all = jnp.concatenate(cols, axis=1).astype(BF16)
    assert w_all.shape[1] == _PROJ_COLS
    return w_all, jnp.concatenate([va, vb], axis=1).T.astype(BF16)


def kernel(x, c, positions, w_ada, b_ada, g_norm, w_in, b_f, sinks, w_o_swa, w_o_fox, w_out, g_final):
    b, s, _ = x.shape
    depth = w_in.shape[0]
    assert depth == 1, "the output stage fuses the final RMSNorm into the single layer"
    inv_freq = ROPE_THETA ** (-jnp.arange(0, HEAD_DIM, 2, dtype=F32) / HEAD_DIM)
    invf = jnp.broadcast_to(inv_freq[:, None], (HALF, _PROJ_TM))
    pos_f = positions.astype(F32)[:, None, :]
    e_mat = jnp.asarray(_aug_placement(), BF16)
    c_pad = jnp.zeros((SUBLANES, D_MODEL), F32).at[:b].set(c)
    for l in range(depth):
        ada = _ada(c_pad, w_ada[l], b_ada[l][None, :])[:b]
        shift, scale, gate = [ada[:, None, k * D_MODEL:(k + 1) * D_MODEL] for k in range(3)]
        w_all, w_vt = _layout_w_in(w_in[l])
        bf_rep = jnp.tile(b_f[l].astype(F32), LANES // FOX_HEADS)[None, :]
        qa, ka, vat, sza, q_aug, k_aug, vbt, szb, sga, sgb = _proj(
            x, pos_f, scale, shift, g_norm[l][None, :], invf, bf_rep, w_all, w_vt, e_mat, tm=_PROJ_TM)
        gated_a = _swa(sinks[l].astype(F32), qa, ka, vat, sza, tq=_SWA_TQ)
        gated_b = _fox(q_aug, k_aug, vbt, szb, t=_FOX_T)
        x = _out(x, gated_a, gated_b, sga, sgb, gate,
                 w_o_swa[l].astype(BF16), w_o_fox[l].astype(BF16), w_out[l].astype(BF16),
                 g_final[None, :], tm=_OUT_TM)
    return x
```

```python
import functools
import math

import numpy as np
import jax
import jax.numpy as jnp
from jax import lax
from jax.experimental import pallas as pl
from jax.experimental.pallas import tpu as pltpu

D_MODEL = 1024
HEAD_DIM = 64
HALF = HEAD_DIM // 2
SWA_Q_HEADS = 8
SWA_KV_HEADS = 2
SWA_WIDTH = SWA_Q_HEADS * HEAD_DIM
SWA_KV_WIDTH = SWA_KV_HEADS * HEAD_DIM
FOX_HEADS = 8
FOX_WIDTH = FOX_HEADS * HEAD_DIM
WINDOW = 128
ROPE_THETA = 10000.0
NORM_EPS = 1e-6
QK_SCALE = HEAD_DIM ** -0.5
LOG2E = math.log2(math.e)

LANES = 128
SUBLANES = 8
MIB = 1024 * 1024
V7X_VMEM_BYTES = 64 * MIB

_PROJ_TM = 512
_SWA_TQ = 2048
_FOX_T = 512
_OUT_TM = 1024

F32 = jnp.float32
BF16 = jnp.bfloat16

_QA, _KA, _ZA = 0, 512, 768
_QB, _KB, _FB, _ZB, _GA, _GB = 1280, 1792, 2304, 2432, 2944, 3968
_PROJ_COLS = 4992

_ONE_LANE = 24

_SWA_AHEAD = 2
_ONES_ROWS = 16


_PROJ_VMEM = 56 * MIB
_FOX_VMEM = 56 * MIB
_OUT_VMEM = 48 * MIB
assert max(_PROJ_VMEM, _FOX_VMEM, _OUT_VMEM) <= V7X_VMEM_BYTES - 8 * MIB


def _split3(v):
    hi = v.astype(BF16)
    r1 = v - hi.astype(F32)
    mid = r1.astype(BF16)
    lo = (r1 - mid.astype(F32)).astype(BF16)
    return hi, mid, lo


def _ada_kernel(c_ref, w_ref, b_ref, o_ref):
    o_ref[...] = jnp.dot(c_ref[...], w_ref[...], preferred_element_type=F32,
                         precision=lax.Precision.HIGHEST) + b_ref[...]


def _ada(c_pad, w_ada, b_ada):
    rows = c_pad.shape[0]
    n = w_ada.shape[1]
    nblk = n // D_MODEL
    return pl.pallas_call(
        _ada_kernel,
        grid=(nblk,),
        in_specs=[pl.BlockSpec((rows, D_MODEL), lambda j: (0, 0)),
                  pl.BlockSpec((D_MODEL, D_MODEL), lambda j: (0, j)),
                  pl.BlockSpec((1, D_MODEL), lambda j: (0, j))],
        out_specs=pl.BlockSpec((rows, D_MODEL), lambda j: (0, j)),
        out_shape=jax.ShapeDtypeStruct((rows, n), F32),
        name="ada_mod",
    )(c_pad, w_ada, b_ada)


def _sigmoid(z):
    return 0.5 * jnp.tanh(0.5 * z) + 0.5


def _proj_kernel(x_ref, pos_ref, scale_ref, shift_ref, gn_ref, invf_ref, bf_ref, w_ref, wvt_ref, e_ref,
                 qa_ref, ka_ref, vat_ref, sza_ref, qaug_ref, kaug_ref, vbt_ref, szb_ref,
                 sga_ref, sgb_ref, carry_ref, *, tm):
    i = pl.program_id(1)

    @pl.when(i == 0)
    def _():
        carry_ref[...] = jnp.zeros_like(carry_ref)

    x = x_ref[0]
    ms = jnp.mean(x * x, axis=-1, keepdims=True)
    h = x * lax.rsqrt(ms + NORM_EPS) * (gn_ref[...] * (1.0 + scale_ref[0])) + shift_ref[0]
    hb = h.astype(BF16)

    def proj(off, n):
        return jnp.dot(hb, w_ref[:, off:off + n], preferred_element_type=F32)

    lane = lax.broadcasted_iota(jnp.int32, (tm, LANES), 1)
    low = lane < HEAD_DIM


    zf = proj(_FB, LANES) + bf_ref[...]
    logf = jnp.minimum(zf, 0.0) - jnp.log1p(jnp.exp(-jnp.abs(zf)))
    grp = lane // FOX_HEADS
    hi, mid, lo = _split3(logf)
    zero = jnp.zeros((tm, LANES), F32)
    parts = jnp.where(grp == 0, hi.astype(F32),
                      jnp.where(grp == 1, mid.astype(F32),
                                jnp.where(grp == 2, lo.astype(F32), zero))).astype(BF16)
    tri = (lax.broadcasted_iota(jnp.int32, (tm, tm), 0)
           >= lax.broadcasted_iota(jnp.int32, (tm, tm), 1)).astype(BF16)
    sga_ref[0] = _sigmoid(proj(_GA, D_MODEL)).astype(BF16)
    rsum = jnp.dot(tri, parts, preferred_element_type=F32)
    c0 = rsum + pltpu.roll(rsum, LANES - FOX_HEADS, 1) + pltpu.roll(rsum, LANES - 2 * FOX_HEADS, 1)
    c0 = jnp.where(grp == 0, c0, zero)
    cl = c0 + pltpu.roll(c0, FOX_HEADS, 1) + pltpu.roll(c0, 2 * FOX_HEADS, 1)
    cum = cl + carry_ref[...]
    carry_ref[...] = cum[tm - 1:tm, :]

    hi, mid, lo = _split3(cum * LOG2E)
    one = jnp.ones((tm, LANES), F32)
    carrier = jnp.where(grp == 0, hi.astype(F32),
                        jnp.where(grp == 1, mid.astype(F32),
                                  jnp.where(grp == 2, lo.astype(F32),
                                            jnp.where(lane == _ONE_LANE, one, zero)))).astype(BF16)
    sgb_ref[0] = _sigmoid(proj(_GB, D_MODEL)).astype(BF16)
    aug = jnp.dot(carrier, e_ref[...], preferred_element_type=F32)

    z = proj(_ZA, SWA_WIDTH)
    sza_ref[0] = (z * _sigmoid(z)).astype(BF16)
    z = proj(_ZB, FOX_WIDTH)
    szb_ref[0] = (z * _sigmoid(z)).astype(BF16)

    ang = invf_ref[...] * pos_ref[0]
    cos = jnp.tile(jnp.cos(ang), (LANES // HALF, 1)).T
    sn = jnp.sin(ang)
    sin = jnp.concatenate([-sn, -sn, sn, sn], axis=0).T
    r = proj(_QA, SWA_WIDTH)
    for c in range(SWA_WIDTH // LANES):
        rc = r[:, c * LANES:(c + 1) * LANES]
        t = rc * cos + pltpu.roll(rc, HEAD_DIM, 1) * sin
        qa_ref[0, :, c * LANES:(c + 1) * LANES] = (t * (QK_SCALE * LOG2E)).astype(BF16)
    r = proj(_KA, 2 * SWA_KV_WIDTH)
    for c in range(2 * SWA_KV_WIDTH // LANES):
        rc = r[:, c * LANES:(c + 1) * LANES]
        t = rc * cos + pltpu.roll(rc, HEAD_DIM, 1) * sin
        ka_ref[0, :, c * LANES:(c + 1) * LANES] = t.astype(BF16)

    rq = proj(_QB, FOX_WIDTH)
    rk = proj(_KB, FOX_WIDTH)
    for hd in range(FOX_HEADS):
        a = (hd // 2) * LANES
        gq = aug[:, a:a + LANES]
        gk = aug[:, FOX_WIDTH + a:FOX_WIDTH + a + LANES]
        qv = rq[:, a:a + LANES] * (QK_SCALE * LOG2E)
        kv = rk[:, a:a + LANES]
        if hd % 2 == 0:
            qaug_ref[0, hd] = jnp.where(low, qv, gq).astype(BF16)
            kaug_ref[0, hd] = jnp.where(low, kv, gk).astype(BF16)
        else:
            qaug_ref[0, hd] = jnp.where(low, gq, qv).astype(BF16)
            kaug_ref[0, hd] = jnp.where(low, gk, kv).astype(BF16)

    vt = lax.dot_general(wvt_ref[...], hb, (((1,), (1,)), ((), ())), preferred_element_type=F32)
    vat_ref[0] = vt[0:SWA_KV_WIDTH].astype(BF16)
    vbt_ref[0] = vt[SWA_KV_WIDTH:].astype(BF16)


def _aug_placement():
    e = np.zeros((LANES, 2 * FOX_HEADS * HEAD_DIM), np.float32)
    for hd in range(FOX_HEADS):
        base = (hd // 2) * LANES + (HEAD_DIM if hd % 2 == 0 else 0)
        kbase = FOX_HEADS * HEAD_DIM + base
        for part in range(3):
            e[part * FOX_HEADS + hd, base + part] = 1.0
            e[_ONE_LANE, base + 3 + part] = 1.0
            e[_ONE_LANE, kbase + part] = 1.0
            e[part * FOX_HEADS + hd, kbase + 3 + part] = -1.0
    return e


def _proj(x, pos_f, scale, shift, gn, invf, bf_rep, w_all, w_vt, e_mat, *, tm):
    b, s, _ = x.shape
    grid = (b, s // tm)
    row = lambda bi, i: (bi, i, 0)
    per_b = lambda bi, i: (bi, 0, 0)
    const2 = lambda bi, i: (0, 0)
    once = pl.Buffered(1)
    in_specs = [
        pl.BlockSpec((1, tm, D_MODEL), row),
        pl.BlockSpec((1, 1, tm), lambda bi, i: (bi, 0, i)),
        pl.BlockSpec((1, 1, D_MODEL), per_b),
        pl.BlockSpec((1, 1, D_MODEL), per_b),
        pl.BlockSpec((1, D_MODEL), const2),
        pl.BlockSpec((HALF, tm), const2),
        pl.BlockSpec((1, LANES), const2),
        pl.BlockSpec((D_MODEL, _PROJ_COLS), const2, pipeline_mode=once),
        pl.BlockSpec((SWA_KV_WIDTH + FOX_WIDTH, D_MODEL), const2, pipeline_mode=once),
        pl.BlockSpec((LANES, 2 * FOX_WIDTH), const2, pipeline_mode=once),
    ]
    head4 = lambda bi, i: (bi, 0, i, 0)
    out_specs = [
        pl.BlockSpec((1, tm, SWA_WIDTH), row),
        pl.BlockSpec((1, tm, 2 * SWA_KV_WIDTH), row),
        pl.BlockSpec((1, SWA_KV_WIDTH, tm), lambda bi, i: (bi, 0, i)),
        pl.BlockSpec((1, tm, SWA_WIDTH), row),
        pl.BlockSpec((1, FOX_HEADS, tm, LANES), head4),
        pl.BlockSpec((1, FOX_HEADS, tm, LANES), head4),
        pl.BlockSpec((1, FOX_WIDTH, tm), lambda bi, i: (bi, 0, i)),
        pl.BlockSpec((1, tm, FOX_WIDTH), row),
        pl.BlockSpec((1, tm, D_MODEL), row),
        pl.BlockSpec((1, tm, D_MODEL), row),
    ]
    sds = jax.ShapeDtypeStruct
    out_shape = [
        sds((b, s, SWA_WIDTH), BF16), sds((b, s, 2 * SWA_KV_WIDTH), BF16),
        sds((b, SWA_KV_WIDTH, s), BF16), sds((b, s, SWA_WIDTH), BF16),
        sds((b, FOX_HEADS, s, LANES), BF16), sds((b, FOX_HEADS, s, LANES), BF16),
        sds((b, FOX_WIDTH, s), BF16), sds((b, s, FOX_WIDTH), BF16),
        sds((b, s, D_MODEL), BF16), sds((b, s, D_MODEL), BF16),
    ]
    return pl.pallas_call(
        functools.partial(_proj_kernel, tm=tm),
        grid=grid, in_specs=in_specs, out_specs=out_specs, out_shape=out_shape,
        scratch_shapes=[pltpu.VMEM((1, LANES), F32)],
        compiler_params=pltpu.CompilerParams(
            dimension_semantics=("arbitrary", "arbitrary"),
            vmem_limit_bytes=_PROJ_VMEM),
        name="in_proj",
    )(x, pos_f, scale, shift, gn, invf, bf_rep, w_all, w_vt, e_mat)


def _swa_kernel(sinks_ref, q_ref, kc_ref, kp_ref, vtc_ref, vtp_ref, bias_ref, sz_ref, o_ref,
                kband, vtband, *, tq):
    i = pl.program_id(1)
    nblk = tq // WINDOW
    nhq = SWA_Q_HEADS // SWA_KV_HEADS
    kband[0:WINDOW] = kp_ref[0]
    kband[WINDOW:] = kc_ref[0]
    vtband[:, 0:WINDOW] = vtp_ref[0]
    vtband[:, WINDOW:] = vtc_ref[0]

    cols = nhq * WINDOW
    first_plane = jnp.where(i > 0, 0, 1)
    headid = lax.broadcasted_iota(jnp.int32, (1, cols), 1) // WINDOW
    low = lax.broadcasted_iota(jnp.int32, (WINDOW, LANES), 1) % HEAD_DIM < HALF
    ones = jnp.ones((_ONES_ROWS, 2 * WINDOW), BF16)
    nt = (((1,), (1,)), ((), ()))

    def logits(r, g):
        kb = kband[r * WINDOW:(r + 2) * WINDOW, g * LANES:(g + 1) * LANES]
        qs = []
        for c in range(2):
            a = (2 * g + c) * LANES
            qc = q_ref[0, r * WINDOW:(r + 1) * WINDOW, a:a + LANES]
            qs.append(jnp.where(low, qc, jnp.zeros_like(qc)))
            qs.append(jnp.where(low, jnp.zeros_like(qc), qc))
        qst = jnp.concatenate(qs, axis=0)
        s = lax.dot_general(kb, qst, nt, preferred_element_type=F32)
        return s + bias_ref[first_plane if r == 0 else 0]

    def attend(r, g, s):
        sink = jnp.zeros((1, cols), F32)
        for k in range(nhq):
            sink = jnp.where(headid == k, sinks_ref[g * nhq + k] * LOG2E, sink)
        m = jnp.maximum(jnp.max(s, axis=0, keepdims=True), sink)
        p = jnp.exp2(s - m).astype(BF16)
        vt = vtband[g * HEAD_DIM:(g + 1) * HEAD_DIM, r * WINDOW:(r + 2) * WINDOW]
        pv = jnp.dot(jnp.concatenate([vt, ones], axis=0), p, preferred_element_type=F32)
        den = pv[HEAD_DIM:HEAD_DIM + 1] + jnp.exp2(sink - m)
        o_t = pv[0:HEAD_DIM] / den
        for c in range(2):
            a = (2 * g + c) * LANES
            pair = jnp.concatenate([o_t[:, 2 * c * WINDOW:(2 * c + 1) * WINDOW],
                                    o_t[:, (2 * c + 1) * WINDOW:(2 * c + 2) * WINDOW]], axis=0)
            gate = sz_ref[0, r * WINDOW:(r + 1) * WINDOW, a:a + LANES].astype(F32)
            o_ref[0, r * WINDOW:(r + 1) * WINDOW, a:a + LANES] = (pair.T * gate).astype(BF16)

    work = [(r, g) for r in range(nblk) for g in range(SWA_KV_HEADS)]
    pending = [logits(*w) for w in work[:_SWA_AHEAD]]
    for n, (r, g) in enumerate(work):
        if n + _SWA_AHEAD < len(work):
            pending.append(logits(*work[n + _SWA_AHEAD]))
        attend(r, g, pending.pop(0))


def _swa(sinks, qa, ka, vat, sza, *, tq):
    b, s, _ = qa.shape
    kvw = 2 * SWA_KV_WIDTH
    per = tq // WINDOW
    cur = lambda bi, i: (bi, i, 0)
    prev = lambda bi, i: (bi, jnp.maximum(i * per - 1, 0), 0)
    cur_t = lambda bi, i: (bi, 0, i)
    prev_t = lambda bi, i: (bi, 0, jnp.maximum(i * per - 1, 0))
    key = np.arange(2 * WINDOW)[:, None]
    qt = np.tile(np.arange(WINDOW), SWA_Q_HEADS // SWA_KV_HEADS)[None, :]
    band = np.where(key < WINDOW, key > qt, (key - WINDOW) <= qt)
    planes = np.stack([band, band & (key >= WINDOW)])
    bias = jnp.asarray(np.where(planes, 0.0, -np.inf).astype(np.float32))
    return pl.pallas_call(
        functools.partial(_swa_kernel, tq=tq),
        grid=(b, s // tq),
        in_specs=[pl.BlockSpec(memory_space=pltpu.SMEM),
                  pl.BlockSpec((1, tq, SWA_WIDTH), cur),
                  pl.BlockSpec((1, tq, kvw), cur),
                  pl.BlockSpec((1, WINDOW, kvw), prev),
                  pl.BlockSpec((1, SWA_KV_WIDTH, tq), cur_t),
                  pl.BlockSpec((1, SWA_KV_WIDTH, WINDOW), prev_t),
                  pl.BlockSpec(bias.shape, lambda bi, i: (0, 0, 0), pipeline_mode=pl.Buffered(1)),
                  pl.BlockSpec((1, tq, SWA_WIDTH), cur)],
        out_specs=pl.BlockSpec((1, tq, SWA_WIDTH), cur),
        out_shape=jax.ShapeDtypeStruct((b, s, SWA_WIDTH), BF16),
        scratch_shapes=[pltpu.VMEM((tq + WINDOW, kvw), BF16),
                        pltpu.VMEM((SWA_KV_WIDTH, tq + WINDOW), BF16)],
        compiler_params=pltpu.CompilerParams(dimension_semantics=("arbitrary", "arbitrary")),
        name="swa_attn",
    )(sinks, qa, ka, ka, vat, vat, bias, sza)


_ACC_ROWS = HEAD_DIM + _ONES_ROWS


def _fox_schedule(n_rows):
    tiles = [(i, i) for i in range(n_rows)] + [(i, j) for i in range(n_rows) for j in range(i)]
    return np.asarray(tiles, np.int32).T.copy()


def _fox_kernel(tbl_ref, q_ref, k_ref, vt_ref, bias_ref, sz_ref, o_ref,
                s0, s1, mx0, mx1, p0, p1, a0, a1, m_ref, acc_ref, *, t, n_rows):
    nt = (((1,), (1,)), ((), ()))
    g = t // SUBLANES
    h = t // 2
    n_diag = n_rows
    n_tiles = n_rows * (n_rows + 1) // 2
    assert n_diag % 2 == 0 and n_tiles % 2 == 0 and n_diag >= 2 and n_tiles - n_diag >= 2

    def step(j, diag, s_w, mx_w, s_r, mx_r, p_w, a_w, p_r, a_r):
        if diag[0] is not None:
            qrow = pl.multiple_of(tbl_ref[0, j] * t, t)
            krow = pl.multiple_of(tbl_ref[1, j] * t, t)
            for hh in range(2):
                if diag[0]:
                    top = lax.dot_general(k_ref[0, hh, pl.ds(krow, h), :], q_ref[0, hh, pl.ds(qrow, t), :],
                                          nt, preferred_element_type=F32) + bias_ref[0:h, :]
                    bot = lax.dot_general(k_ref[0, hh, pl.ds(pl.multiple_of(krow + h, h), h), :],
                                          q_ref[0, hh, pl.ds(pl.multiple_of(qrow + h, h), h), :],
                                          nt, preferred_element_type=F32) + bias_ref[h:, h:]
                    s_w[hh, 0:h] = top
                    s_w[hh, h:, h:] = bot
                    top_mx = jnp.max(top.reshape(h // SUBLANES, SUBLANES, t), axis=0)
                    bot_mx = jnp.max(bot.reshape(h // SUBLANES, SUBLANES, h), axis=0)
                    mx_w[hh, :, 0:h] = top_mx[:, 0:h]
                    mx_w[hh, :, h:] = jnp.maximum(top_mx[:, h:], bot_mx)
                else:
                    s = lax.dot_general(k_ref[0, hh, pl.ds(krow, t), :], q_ref[0, hh, pl.ds(qrow, t), :],
                                        nt, preferred_element_type=F32)
                    s_w[hh] = s
                    mx_w[hh] = jnp.max(s.reshape(g, SUBLANES, t), axis=0)

        if diag[1] is not None:
            srow = tbl_ref[0, j - 1]
            for hh in range(2):
                m_tile = jnp.max(mx_r[hh], axis=0, keepdims=True)
                if diag[1]:
                    m_new = jnp.broadcast_to(m_tile, (SUBLANES, t))
                else:
                    m_old = m_ref[srow, hh]
                    m_new = jnp.maximum(m_old, m_tile)
                    a_w[hh] = jnp.exp2(m_old - m_new)
                m_ref[srow, hh] = m_new
                if diag[1]:
                    top = jnp.exp2(s_r[hh, 0:h].reshape(h // SUBLANES, SUBLANES, t) - m_new[None])
                    bot = jnp.exp2(s_r[hh, h:, h:].reshape(h // SUBLANES, SUBLANES, h) - m_new[None, :, h:])
                    p_w[hh, 0:h] = top.reshape(h, t).astype(BF16)
                    p_w[hh, h:, h:] = bot.reshape(h, h).astype(BF16)
                else:
                    p = jnp.exp2(s_r[hh].reshape(g, SUBLANES, t) - m_new[None])
                    p_w[hh] = p.reshape(t, t).astype(BF16)

        if diag[2] is not None:
            vcol = pl.multiple_of(tbl_ref[1, j - 2] * t, t)
            arow = tbl_ref[0, j - 2]
            for hh in range(2):
                vt = vt_ref[0, hh * HEAD_DIM:(hh + 1) * HEAD_DIM, pl.ds(vcol, t)]
                lhs = jnp.concatenate([vt, jnp.ones((_ONES_ROWS, t), BF16)], axis=0)
                if diag[2]:
                    acc_ref[arow, hh, :, 0:h] = jnp.dot(lhs[:, 0:h], p_r[hh, 0:h, 0:h],
                                                        preferred_element_type=F32)
                    acc_ref[arow, hh, :, h:] = jnp.dot(lhs, p_r[hh, :, h:], preferred_element_type=F32)
                else:
                    part = jnp.dot(lhs, p_r[hh], preferred_element_type=F32)
                    old = acc_ref[arow, hh].reshape(_ACC_ROWS // SUBLANES, SUBLANES, t)
                    acc_ref[arow, hh] = (a_r[hh][None] * old).reshape(_ACC_ROWS, t) + part

    def step_pair(j, diag_even, diag_odd):
        step(j, diag_even, s0, mx0, s1, mx1, p1, a1, p0, a0)
        step(j + 1, diag_odd, s1, mx1, s0, mx0, p0, a0, p1, a1)

    def steady(first, last, diag):
        def body(kk, carry):
            step_pair(first + 2 * kk, diag, diag)
            return carry

        lax.fori_loop(0, (last - first) // 2, body, 0)

    step_pair(0, (True, None, None), (True, True, None))
    steady(2, n_diag, (True, True, True))
    step_pair(n_diag, (False, True, True), (False, False, True))
    steady(n_diag + 2, n_tiles, (False, False, False))
    step_pair(n_tiles, (None, False, False), (None, None, False))

    def finish_row(i, carry):
        halves = []
        for hh in range(2):
            a = acc_ref[i, hh]
            halves.append(a[0:HEAD_DIM] / a[HEAD_DIM:HEAD_DIM + 1])
        o_t = jnp.concatenate(halves, axis=0)
        rows = pl.ds(pl.multiple_of(i * t, t), t)
        o_ref[0, rows, :] = (o_t.T * sz_ref[0, rows, :].astype(F32)).astype(BF16)
        return carry

    lax.fori_loop(0, n_rows, finish_row, 0, unroll=8)


def _fox(q_aug, k_aug, vbt, szb, *, t):
    b, _, s, _ = q_aug.shape
    npair = FOX_HEADS // 2
    n_rows = s // t
    tbl = _fox_schedule(n_rows)
    tri = np.triu(np.ones((t, t), bool))
    bias = jnp.asarray(np.where(tri, 0.0, -np.inf).astype(np.float32))
    pair = lambda bi, p: (bi, p, 0, 0)
    stat = pltpu.VMEM((2, SUBLANES, t), F32)
    return pl.pallas_call(
        functools.partial(_fox_kernel, t=t, n_rows=n_rows),
        grid=(b, npair),
        in_specs=[pl.BlockSpec(memory_space=pltpu.SMEM),
                  pl.BlockSpec((1, 2, s, LANES), pair),
                  pl.BlockSpec((1, 2, s, LANES), pair),
                  pl.BlockSpec((1, 2 * HEAD_DIM, s), lambda bi, p: (bi, p, 0)),
                  pl.BlockSpec((t, t), lambda bi, p: (0, 0), pipeline_mode=pl.Buffered(1)),
                  pl.BlockSpec((1, s, LANES), lambda bi, p: (bi, 0, p))],
        out_specs=pl.BlockSpec((1, s, LANES), lambda bi, p: (bi, 0, p)),
        out_shape=jax.ShapeDtypeStruct((b, s, FOX_WIDTH), BF16),
        scratch_shapes=[pltpu.VMEM((2, t, t), F32), pltpu.VMEM((2, t, t), F32),
                        stat, stat,
                        pltpu.VMEM((2, t, t), BF16), pltpu.VMEM((2, t, t), BF16),
                        stat, stat,
                        pltpu.VMEM((n_rows, 2, SUBLANES, t), F32),
                        pltpu.VMEM((n_rows, 2, _ACC_ROWS, t), F32)],
        compiler_params=pltpu.CompilerParams(
            dimension_semantics=("arbitrary", "arbitrary"),
            vmem_limit_bytes=_FOX_VMEM),
        name="fox_attn",
    )(jnp.asarray(tbl), q_aug, k_aug, vbt, bias, szb)


def _out_kernel(x_ref, ga_ref, gb_ref, sga_ref, sgb_ref, gate_ref,
                woa_ref, wob_ref, wout_ref, gf_ref, o_ref):
    ya = jnp.dot(ga_ref[0], woa_ref[...], preferred_element_type=F32)
    yb = jnp.dot(gb_ref[0], wob_ref[...], preferred_element_type=F32)
    merged = sga_ref[0].astype(F32) * ya + sgb_ref[0].astype(F32) * yb
    d = jnp.dot(merged.astype(BF16), wout_ref[...], preferred_element_type=F32)
    xo = x_ref[0] + gate_ref[0] * d
    ms = jnp.mean(xo * xo, axis=-1, keepdims=True)
    o_ref[0] = xo * lax.rsqrt(ms + NORM_EPS) * gf_ref[...]


def _out(x, gated_a, gated_b, sga, sgb, gate, woa, wob, wout, gf, *, tm):
    b, s, _ = x.shape
    row = lambda bi, i: (bi, i, 0)
    per_b = lambda bi, i: (bi, 0, 0)
    const2 = lambda bi, i: (0, 0)
    once = pl.Buffered(1)
    return pl.pallas_call(
        _out_kernel,
        grid=(b, s // tm),
        in_specs=[pl.BlockSpec((1, tm, D_MODEL), row),
                  pl.BlockSpec((1, tm, SWA_WIDTH), row), pl.BlockSpec((1, tm, FOX_WIDTH), row),
                  pl.BlockSpec((1, tm, D_MODEL), row), pl.BlockSpec((1, tm, D_MODEL), row),
                  pl.BlockSpec((1, 1, D_MODEL), per_b),
                  pl.BlockSpec((SWA_WIDTH, D_MODEL), const2, pipeline_mode=once),
                  pl.BlockSpec((FOX_WIDTH, D_MODEL), const2, pipeline_mode=once),
                  pl.BlockSpec((D_MODEL, D_MODEL), const2, pipeline_mode=once),
                  pl.BlockSpec((1, D_MODEL), const2)],
        out_specs=pl.BlockSpec((1, tm, D_MODEL), row),
        out_shape=jax.ShapeDtypeStruct((b, s, D_MODEL), F32),
        compiler_params=pltpu.CompilerParams(
            dimension_semantics=("arbitrary", "arbitrary"),
            vmem_limit_bytes=_OUT_VMEM),
        name="out_proj",
    )(x, gated_a, gated_b, sga, sgb, gate, woa, wob, wout, gf)


def _pair_halves(w, nheads, dup):
    rows = w.shape[0]
    if dup:
        w5 = jnp.broadcast_to(w.reshape(rows, nheads, 1, 2, HALF), (rows, nheads, 2, 2, HALF))
    else:
        w5 = w.reshape(rows, nheads // 2, 2, 2, HALF)
    return w5.transpose(0, 1, 3, 2, 4).reshape(rows, -1)


def _layout_w_in(w):
    o = np.cumsum([0, SWA_WIDTH, SWA_KV_WIDTH, SWA_KV_WIDTH, SWA_WIDTH, FOX_WIDTH, FOX_WIDTH, FOX_WIDTH,
                   FOX_HEADS, FOX_WIDTH, D_MODEL, D_MODEL]).tolist()
    qa, ka, va, za, qb, kb, vb, fb, zb, ga, gb = [w[:, o[k]:o[k + 1]] for k in range(11)]
    cols = [_pair_halves(qa, SWA_Q_HEADS, False), _pair_halves(ka, SWA_KV_HEADS, True),
            za, qb, kb, jnp.tile(fb, (1, LANES // FOX_HEADS)), zb, ga, gb]
    w_all = jnp.concatenate(cols, axis=1).astype(BF16)
    assert w_all.shape[1] == _PROJ_COLS
    return w_all, jnp.concatenate([va, vb], axis=1).T.astype(BF16)


def kernel(x, c, positions, w_ada, b_ada, g_norm, w_in, b_f, sinks, w_o_swa, w_o_fox, w_out, g_final):
    b, s, _ = x.shape
    depth = w_in.shape[0]
    assert depth == 1, "the output stage fuses the final RMSNorm into the single layer"
    inv_freq = ROPE_THETA ** (-jnp.arange(0, HEAD_DIM, 2, dtype=F32) / HEAD_DIM)
    invf = jnp.broadcast_to(inv_freq[:, None], (HALF, _PROJ_TM))
    pos_f = positions.astype(F32)[:, None, :]
    e_mat = jnp.asarray(_aug_placement(), BF16)
    c_pad = jnp.zeros((SUBLANES, D_MODEL), F32).at[:b].set(c)
    for l in range(depth):
        ada = _ada(c_pad, w_ada[l], b_ada[l][None, :])[:b]
        shift, scale, gate = [ada[:, None, k * D_MODEL:(k + 1) * D_MODEL] for k in range(3)]
        w_all, w_vt = _layout_w_in(w_in[l])
        bf_rep = jnp.tile(b_f[l].astype(F32), LANES // FOX_HEADS)[None, :]
        qa, ka, vat, sza, q_aug, k_aug, vbt, szb, sga, sgb = _proj(
            x, pos_f, scale, shift, g_norm[l][None, :], invf, bf_rep, w_all, w_vt, e_mat, tm=_PROJ_TM)
        gated_a = _swa(sinks[l].astype(F32), qa, ka, vat, sza, tq=_SWA_TQ)
        gated_b = _fox(q_aug, k_aug, vbt, szb, t=_FOX_T)
        x = _out(x, gated_a, gated_b, sga, sgb, gate,
                 w_o_swa[l].astype(BF16), w_o_fox[l].astype(BF16), w_out[l].astype(BF16),
                 g_final[None, :], tm=_OUT_TM)
    return x
```

```python
import functools
import math

import numpy as np
import jax
import jax.numpy as jnp
from jax import lax
from jax.experimental import pallas as pl
from jax.experimental.pallas import tpu as pltpu

D_MODEL = 1024
HEAD_DIM = 64
HALF = HEAD_DIM // 2
SWA_Q_HEADS = 8
SWA_KV_HEADS = 2
SWA_WIDTH = SWA_Q_HEADS * HEAD_DIM
SWA_KV_WIDTH = SWA_KV_HEADS * HEAD_DIM
FOX_HEADS = 8
FOX_WIDTH = FOX_HEADS * HEAD_DIM
WINDOW = 128
ROPE_THETA = 10000.0
NORM_EPS = 1e-6
QK_SCALE = HEAD_DIM ** -0.5
LOG2E = math.log2(math.e)

LANES = 128
SUBLANES = 8
MIB = 1024 * 1024
V7X_VMEM_BYTES = 64 * MIB

_PROJ_TM = 512
_SWA_TQ = 2048
_FOX_T = 512
_OUT_TM = 1024

F32 = jnp.float32
BF16 = jnp.bfloat16

_QA, _KA, _ZA = 0, 512, 768
_QB, _KB, _FB, _ZB, _GA, _GB = 1280, 1792, 2304, 2432, 2944, 3968
_PROJ_COLS = 4992

_ONE_LANE = 24

_SWA_AHEAD = 2
_ONES_ROWS = 16


_PROJ_VMEM = 56 * MIB
_FOX_VMEM = 56 * MIB
_OUT_VMEM = 48 * MIB
assert max(_PROJ_VMEM, _FOX_VMEM, _OUT_VMEM) <= V7X_VMEM_BYTES - 8 * MIB


def _split3(v):
    hi = v.astype(BF16)
    r1 = v - hi.astype(F32)
    mid = r1.astype(BF16)
    lo = (r1 - mid.astype(F32)).astype(BF16)
    return hi, mid, lo


def _ada_kernel(c_ref, w_ref, b_ref, o_ref):
    o_ref[...] = jnp.dot(c_ref[...], w_ref[...], preferred_element_type=F32,
                         precision=lax.Precision.HIGHEST) + b_ref[...]


def _ada(c_pad, w_ada, b_ada):
    rows = c_pad.shape[0]
    n = w_ada.shape[1]
    nblk = n // D_MODEL
    return pl.pallas_call(
        _ada_kernel,
        grid=(nblk,),
        in_specs=[pl.BlockSpec((rows, D_MODEL), lambda j: (0, 0)),
                  pl.BlockSpec((D_MODEL, D_MODEL), lambda j: (0, j)),
                  pl.BlockSpec((1, D_MODEL), lambda j: (0, j))],
        out_specs=pl.BlockSpec((rows, D_MODEL), lambda j: (0, j)),
        out_shape=jax.ShapeDtypeStruct((rows, n), F32),
        name="ada_mod",
    )(c_pad, w_ada, b_ada)


def _sigmoid(z):
    return 0.5 * jnp.tanh(0.5 * z) + 0.5


def _proj_kernel(x_ref, pos_ref, scale_ref, shift_ref, gn_ref, invf_ref, bf_ref, w_ref, wvt_ref, e_ref,
                 qa_ref, ka_ref, vat_ref, sza_ref, qaug_ref, kaug_ref, vbt_ref, szb_ref,
                 sga_ref, sgb_ref, carry_ref, *, tm):
    i = pl.program_id(1)

    @pl.when(i == 0)
    def _():
        carry_ref[...] = jnp.zeros_like(carry_ref)

    x = x_ref[0]
    ms = jnp.mean(x * x, axis=-1, keepdims=True)
    h = x * lax.rsqrt(ms + NORM_EPS) * (gn_ref[...] * (1.0 + scale_ref[0])) + shift_ref[0]
    hb = h.astype(BF16)

    def proj(off, n):
        return jnp.dot(hb, w_ref[:, off:off + n], preferred_element_type=F32)

    lane = lax.broadcasted_iota(jnp.int32, (tm, LANES), 1)
    low = lane < HEAD_DIM


    zf = proj(_FB, LANES) + bf_ref[...]
    logf = jnp.minimum(zf, 0.0) - jnp.log1p(jnp.exp(-jnp.abs(zf)))
    grp = lane // FOX_HEADS
    hi, mid, lo = _split3(logf)
    zero = jnp.zeros((tm, LANES), F32)
    parts = jnp.where(grp == 0, hi.astype(F32),
                      jnp.where(grp == 1, mid.astype(F32),
                                jnp.where(grp == 2, lo.astype(F32), zero))).astype(BF16)
    tri = (lax.broadcasted_iota(jnp.int32, (tm, tm), 0)
           >= lax.broadcasted_iota(jnp.int32, (tm, tm), 1)).astype(BF16)
    sga_ref[0] = _sigmoid(proj(_GA, D_MODEL)).astype(BF16)
    rsum = jnp.dot(tri, parts, preferred_element_type=F32)
    c0 = rsum + pltpu.roll(rsum, LANES - FOX_HEADS, 1) + pltpu.roll(rsum, LANES - 2 * FOX_HEADS, 1)
    c0 = jnp.where(grp == 0, c0, zero)
    cl = c0 + pltpu.roll(c0, FOX_HEADS, 1) + pltpu.roll(c0, 2 * FOX_HEADS, 1)
    cum = cl + carry_ref[...]
    carry_ref[...] = cum[tm - 1:tm, :]

    hi, mid, lo = _split3(cum * LOG2E)
    one = jnp.ones((tm, LANES), F32)
    carrier = jnp.where(grp == 0, hi.astype(F32),
                        jnp.where(grp == 1, mid.astype(F32),
                                  jnp.where(grp == 2, lo.astype(F32),
                                            jnp.where(lane == _ONE_LANE, one, zero)))).astype(BF16)
    sgb_ref[0] = _sigmoid(proj(_GB, D_MODEL)).astype(BF16)
    aug = jnp.dot(carrier, e_ref[...], preferred_element_type=F32)

    z = proj(_ZA, SWA_WIDTH)
    sza_ref[0] = (z * _sigmoid(z)).astype(BF16)
    z = proj(_ZB, FOX_WIDTH)
    szb_ref[0] = (z * _sigmoid(z)).astype(BF16)

    ang = invf_ref[...] * pos_ref[0]
    cos = jnp.tile(jnp.cos(ang), (LANES // HALF, 1)).T
    sn = jnp.sin(ang)
    sin = jnp.concatenate([-sn, -sn, sn, sn], axis=0).T
    r = proj(_QA, SWA_WIDTH)
    for c in range(SWA_WIDTH // LANES):
        rc = r[:, c * LANES:(c + 1) * LANES]
        t = rc * cos + pltpu.roll(rc, HEAD_DIM, 1) * sin
        qa_ref[0, :, c * LANES:(c + 1) * LANES] = (t * (QK_SCALE * LOG2E)).astype(BF16)
    r = proj(_KA, 2 * SWA_KV_WIDTH)
    for c in range(2 * SWA_KV_WIDTH // LANES):
        rc = r[:, c * LANES:(c + 1) * LANES]
        t = rc * cos + pltpu.roll(rc, HEAD_DIM, 1) * sin
        ka_ref[0, :, c * LANES:(c + 1) * LANES] = t.astype(BF16)

    rq = proj(_QB, FOX_WIDTH)
    rk = proj(_KB, FOX_WIDTH)
    for hd in range(FOX_HEADS):
        a = (hd // 2) * LANES
        gq = aug[:, a:a + LANES]
        gk = aug[:, FOX_WIDTH + a:FOX_WIDTH + a + LANES]
        qv = rq[:, a:a + LANES] * (QK_SCALE * LOG2E)
        kv = rk[:, a:a + LANES]
        if hd % 2 == 0:
            qaug_ref[0, hd] = jnp.where(low, qv, gq).astype(BF16)
            kaug_ref[0, hd] = jnp.where(low, kv, gk).astype(BF16)
        else:
            qaug_ref[0, hd] = jnp.where(low, gq, qv).astype(BF16)
            kaug_ref[0, hd] = jnp.where(low, gk, kv).astype(BF16)

    vt = lax.dot_general(wvt_ref[...], hb, (((1,), (1,)), ((), ())), preferred_element_type=F32)
    vat_ref[0] = vt[0:SWA_KV_WIDTH].astype(BF16)
    vbt_ref[0] = vt[SWA_KV_WIDTH:].astype(BF16)


def _aug_placement():
    e = np.zeros((LANES, 2 * FOX_HEADS * HEAD_DIM), np.float32)
    for hd in range(FOX_HEADS):
        base = (hd // 2) * LANES + (HEAD_DIM if hd % 2 == 0 else 0)
        kbase = FOX_HEADS * HEAD_DIM + base
        for part in range(3):
            e[part * FOX_HEADS + hd, base + part] = 1.0
            e[_ONE_LANE, base + 3 + part] = 1.0
            e[_ONE_LANE, kbase + part] = 1.0
            e[part * FOX_HEADS + hd, kbase + 3 + part] = -1.0
    return e


def _proj(x, pos_f, scale, shift, gn, invf, bf_rep, w_all, w_vt, e_mat, *, tm):
    b, s, _ = x.shape
    grid = (b, s // tm)
    row = lambda bi, i: (bi, i, 0)
    per_b = lambda bi, i: (bi, 0, 0)
    const2 = lambda bi, i: (0, 0)
    once = pl.Buffered(1)
    in_specs = [
        pl.BlockSpec((1, tm, D_MODEL), row),
        pl.BlockSpec((1, 1, tm), lambda bi, i: (bi, 0, i)),
        pl.BlockSpec((1, 1, D_MODEL), per_b),
        pl.BlockSpec((1, 1, D_MODEL), per_b),
        pl.BlockSpec((1, D_MODEL), const2),
        pl.BlockSpec((HALF, tm), const2),
        pl.BlockSpec((1, LANES), const2),
        pl.BlockSpec((D_MODEL, _PROJ_COLS), const2, pipeline_mode=once),
        pl.BlockSpec((SWA_KV_WIDTH + FOX_WIDTH, D_MODEL), const2, pipeline_mode=once),
        pl.BlockSpec((LANES, 2 * FOX_WIDTH), const2, pipeline_mode=once),
    ]
    head4 = lambda bi, i: (bi, 0, i, 0)
    out_specs = [
        pl.BlockSpec((1, tm, SWA_WIDTH), row),
        pl.BlockSpec((1, tm, 2 * SWA_KV_WIDTH), row),
        pl.BlockSpec((1, SWA_KV_WIDTH, tm), lambda bi, i: (bi, 0, i)),
        pl.BlockSpec((1, tm, SWA_WIDTH), row),
        pl.BlockSpec((1, FOX_HEADS, tm, LANES), head4),
        pl.BlockSpec((1, FOX_HEADS, tm, LANES), head4),
        pl.BlockSpec((1, FOX_WIDTH, tm), lambda bi, i: (bi, 0, i)),
        pl.BlockSpec((1, tm, FOX_WIDTH), row),
        pl.BlockSpec((1, tm, D_MODEL), row),
        pl.BlockSpec((1, tm, D_MODEL), row),
    ]
    sds = jax.ShapeDtypeStruct
    out_shape = [
        sds((b, s, SWA_WIDTH), BF16), sds((b, s, 2 * SWA_KV_WIDTH), BF16),
        sds((b, SWA_KV_WIDTH, s), BF16), sds((b, s, SWA_WIDTH), BF16),
        sds((b, FOX_HEADS, s, LANES), BF16), sds((b, FOX_HEADS, s, LANES), BF16),
        sds((b, FOX_WIDTH, s), BF16), sds((b, s, FOX_WIDTH), BF16),
        sds((b, s, D_MODEL), BF16), sds((b, s, D_MODEL), BF16),
    ]
    return pl.pallas_call(
        functools.partial(_proj_kernel, tm=tm),
        grid=grid, in_specs=in_specs, out_specs=out_specs, out_shape=out_shape,
        scratch_shapes=[pltpu.VMEM((1, LANES), F32)],
        compiler_params=pltpu.CompilerParams(
            dimension_semantics=("arbitrary", "arbitrary"),
            vmem_limit_bytes=_PROJ_VMEM),
        name="in_proj",
    )(x, pos_f, scale, shift, gn, invf, bf_rep, w_all, w_vt, e_mat)


def _swa_kernel(sinks_ref, q_ref, kc_ref, kp_ref, vtc_ref, vtp_ref, bias_ref, sz_ref, o_ref,
                kband, vtband, *, tq):
    i = pl.program_id(1)
    nblk = tq // WINDOW
    nhq = SWA_Q_HEADS // SWA_KV_HEADS
    kband[0:WINDOW] = kp_ref[0]
    kband[WINDOW:] = kc_ref[0]
    vtband[:, 0:WINDOW] = vtp_ref[0]
    vtband[:, WINDOW:] = vtc_ref[0]

    cols = nhq * WINDOW
    first_plane = jnp.where(i > 0, 0, 1)
    headid = lax.broadcasted_iota(jnp.int32, (1, cols), 1) // WINDOW
    low = lax.broadcasted_iota(jnp.int32, (WINDOW, LANES), 1) % HEAD_DIM < HALF
    ones = jnp.ones((_ONES_ROWS, 2 * WINDOW), BF16)
    nt = (((1,), (1,)), ((), ()))

    def logits(r, g):
        kb = kband[r * WINDOW:(r + 2) * WINDOW, g * LANES:(g + 1) * LANES]
        qs = []
        for c in range(2):
            a = (2 * g + c) * LANES
            qc = q_ref[0, r * WINDOW:(r + 1) * WINDOW, a:a + LANES]
            qs.append(jnp.where(low, qc, jnp.zeros_like(qc)))
            qs.append(jnp.where(low, jnp.zeros_like(qc), qc))
        qst = jnp.concatenate(qs, axis=0)
        s = lax.dot_general(kb, qst, nt, preferred_element_type=F32)
        return s + bias_ref[first_plane if r == 0 else 0]

    def attend(r, g, s):
        sink = jnp.zeros((1, cols), F32)
        for k in range(nhq):
            sink = jnp.where(headid == k, sinks_ref[g * nhq + k] * LOG2E, sink)
        m = jnp.maximum(jnp.max(s, axis=0, keepdims=True), sink)
        p = jnp.exp2(s - m).astype(BF16)
        vt = vtband[g * HEAD_DIM:(g + 1) * HEAD_DIM, r * WINDOW:(r + 2) * WINDOW]
        pv = jnp.dot(jnp.concatenate([vt, ones], axis=0), p, preferred_element_type=F32)
        den = pv[HEAD_DIM:HEAD_DIM + 1] + jnp.exp2(sink - m)
        o_t = pv[0:HEAD_DIM] / den
        for c in range(2):
            a = (2 * g + c) * LANES
            pair = jnp.concatenate([o_t[:, 2 * c * WINDOW:(2 * c + 1) * WINDOW],
                                    o_t[:, (2 * c + 1) * WINDOW:(2 * c + 2) * WINDOW]], axis=0)
            gate = sz_ref[0, r * WINDOW:(r + 1) * WINDOW, a:a + LANES].astype(F32)
            o_ref[0, r * WINDOW:(r + 1) * WINDOW, a:a + LANES] = (pair.T * gate).astype(BF16)

    work = [(r, g) for r in range(nblk) for g in range(SWA_KV_HEADS)]
    pending = [logits(*w) for w in work[:_SWA_AHEAD]]
    for n, (r, g) in enumerate(work):
        if n + _SWA_AHEAD < len(work):
            pending.append(logits(*work[n + _SWA_AHEAD]))
        attend(r, g, pending.pop(0))


def _swa(sinks, qa, ka, vat, sza, *, tq):
    b, s, _ = qa.shape
    kvw = 2 * SWA_KV_WIDTH
    per = tq // WINDOW
    cur = lambda bi, i: (bi, i, 0)
    prev = lambda bi, i: (bi, jnp.maximum(i * per - 1, 0), 0)
    cur_t = lambda bi, i: (bi, 0, i)
    prev_t = lambda bi, i: (bi, 0, jnp.maximum(i * per - 1, 0))
    key = np.arange(2 * WINDOW)[:, None]
    qt = np.tile(np.arange(WINDOW), SWA_Q_HEADS // SWA_KV_HEADS)[None, :]
    band = np.where(key < WINDOW, key > qt, (key - WINDOW) <= qt)
    planes = np.stack([band, band & (key >= WINDOW)])
    bias = jnp.asarray(np.where(planes, 0.0, -np.inf).astype(np.float32))
    return pl.pallas_call(
        functools.partial(_swa_kernel, tq=tq),
        grid=(b, s // tq),
        in_specs=[pl.BlockSpec(memory_space=pltpu.SMEM),
                  pl.BlockSpec((1, tq, SWA_WIDTH), cur),
                  pl.BlockSpec((1, tq, kvw), cur),
                  pl.BlockSpec((1, WINDOW, kvw), prev),
                  pl.BlockSpec((1, SWA_KV_WIDTH, tq), cur_t),
                  pl.BlockSpec((1, SWA_KV_WIDTH, WINDOW), prev_t),
                  pl.BlockSpec(bias.shape, lambda bi, i: (0, 0, 0), pipeline_mode=pl.Buffered(1)),
                  pl.BlockSpec((1, tq, SWA_WIDTH), cur)],
        out_specs=pl.BlockSpec((1, tq, SWA_WIDTH), cur),
        out_shape=jax.ShapeDtypeStruct((b, s, SWA_WIDTH), BF16),
        scratch_shapes=[pltpu.VMEM((tq + WINDOW, kvw), BF16),
                        pltpu.VMEM((SWA_KV_WIDTH, tq + WINDOW), BF16)],
        compiler_params=pltpu.CompilerParams(dimension_semantics=("arbitrary", "arbitrary")),
        name="swa_attn",
    )(sinks, qa, ka, ka, vat, vat, bias, sza)


_ACC_ROWS = HEAD_DIM + _ONES_ROWS


def _fox_schedule(n_rows):
    tiles = [(i, i) for i in range(n_rows)] + [(i, j) for i in range(n_rows) for j in range(i)]
    return np.asarray(tiles, np.int32).T.copy()


def _fox_kernel(tbl_ref, q_ref, k_ref, vt_ref, bias_ref, sz_ref, o_ref,
                s0, s1, mx0, mx1, p0, p1, a0, a1, m_ref, acc_ref, *, t, n_rows):
    nt = (((1,), (1,)), ((), ()))
    g = t // SUBLANES
    h = t // 2
    n_diag = n_rows
    n_tiles = n_rows * (n_rows + 1) // 2
    assert n_diag % 2 == 0 and n_tiles % 2 == 0 and n_diag >= 2 and n_tiles - n_diag >= 2

    def step(j, diag, s_w, mx_w, s_r, mx_r, p_w, a_w, p_r, a_r):
        stage1 = stage2 = stage3 = None
        if diag[0] is not None:
            qrow = pl.multiple_of(tbl_ref[0, j] * t, t)
            krow = pl.multiple_of(tbl_ref[1, j] * t, t)

            def stage1(hh):
                if diag[0]:
                    top = lax.dot_general(k_ref[0, hh, pl.ds(krow, h), :], q_ref[0, hh, pl.ds(qrow, t), :],
                                          nt, preferred_element_type=F32) + bias_ref[0:h, :]
                    bot = lax.dot_general(k_ref[0, hh, pl.ds(pl.multiple_of(krow + h, h), h), :],
                                          q_ref[0, hh, pl.ds(pl.multiple_of(qrow + h, h), h), :],
                                          nt, preferred_element_type=F32) + bias_ref[h:, h:]
                    s_w[hh, 0:h] = top
                    s_w[hh, h:, h:] = bot
                    top_mx = jnp.max(top.reshape(h // SUBLANES, SUBLANES, t), axis=0)
                    bot_mx = jnp.max(bot.reshape(h // SUBLANES, SUBLANES, h), axis=0)
                    mx_w[hh, :, 0:h] = top_mx[:, 0:h]
                    mx_w[hh, :, h:] = jnp.maximum(top_mx[:, h:], bot_mx)
                else:
                    s = lax.dot_general(k_ref[0, hh, pl.ds(krow, t), :], q_ref[0, hh, pl.ds(qrow, t), :],
                                        nt, preferred_element_type=F32)
                    s_w[hh] = s
                    mx_w[hh] = jnp.max(s.reshape(g, SUBLANES, t), axis=0)

        if diag[1] is not None:
            srow = tbl_ref[0, j - 1]

            def stage2(hh):
                m_tile = jnp.max(mx_r[hh], axis=0, keepdims=True)
                if diag[1]:
                    m_new = jnp.broadcast_to(m_tile, (SUBLANES, t))
                else:
                    m_old = m_ref[srow, hh]
                    m_new = jnp.maximum(m_old, m_tile)
                    a_w[hh] = jnp.exp2(m_old - m_new)
                m_ref[srow, hh] = m_new
                if diag[1]:
                    top = jnp.exp2(s_r[hh, 0:h].reshape(h // SUBLANES, SUBLANES, t) - m_new[None])
                    bot = jnp.exp2(s_r[hh, h:, h:].reshape(h // SUBLANES, SUBLANES, h) - m_new[None, :, h:])
                    p_w[hh, 0:h] = top.reshape(h, t).astype(BF16)
                    p_w[hh, h:, h:] = bot.reshape(h, h).astype(BF16)
                else:
                    p = jnp.exp2(s_r[hh].reshape(g, SUBLANES, t) - m_new[None])
                    p_w[hh] = p.reshape(t, t).astype(BF16)

        if diag[2] is not None:
            vcol = pl.multiple_of(tbl_ref[1, j - 2] * t, t)
            arow = tbl_ref[0, j - 2]

            def stage3(hh):
                vt = vt_ref[0, hh * HEAD_DIM:(hh + 1) * HEAD_DIM, pl.ds(vcol, t)]
                lhs = jnp.concatenate([vt, jnp.ones((_ONES_ROWS, t), BF16)], axis=0)
                if diag[2]:
                    acc_ref[arow, hh, :, 0:h] = jnp.dot(lhs[:, 0:h], p_r[hh, 0:h, 0:h],
                                                        preferred_element_type=F32)
                    acc_ref[arow, hh, :, h:] = jnp.dot(lhs, p_r[hh, :, h:], preferred_element_type=F32)
                else:
                    part = jnp.dot(lhs, p_r[hh], preferred_element_type=F32)
                    old = acc_ref[arow, hh].reshape(_ACC_ROWS // SUBLANES, SUBLANES, t)
                    acc_ref[arow, hh] = (a_r[hh][None] * old).reshape(_ACC_ROWS, t) + part

        if diag == (True, True, True):
            order = [(stage1, 0), (stage3, 0), (stage1, 1), (stage3, 1), (stage2, 0), (stage2, 1)]
        else:
            order = [(stage, hh) for n, stage in enumerate((stage1, stage2, stage3)) if diag[n] is not None
                     for hh in range(2)]
        for stage, hh in order:
            stage(hh)

    def step_pair(j, diag_even, diag_odd):
        step(j, diag_even, s0, mx0, s1, mx1, p1, a1, p0, a0)
        step(j + 1, diag_odd, s1, mx1, s0, mx0, p0, a0, p1, a1)

    def steady(first, last, diag):
        def body(kk, carry):
            step_pair(first + 2 * kk, diag, diag)
            return carry

        lax.fori_loop(0, (last - first) // 2, body, 0)

    step_pair(0, (True, None, None), (True, True, None))
    steady(2, n_diag, (True, True, True))
    step_pair(n_diag, (False, True, True), (False, False, True))
    steady(n_diag + 2, n_tiles, (False, False, False))
    step_pair(n_tiles, (None, False, False), (None, None, False))

    def finish_row(i, carry):
        halves = []
        for hh in range(2):
            a = acc_ref[i, hh]
            halves.append(a[0:HEAD_DIM] / a[HEAD_DIM:HEAD_DIM + 1])
        o_t = jnp.concatenate(halves, axis=0)
        rows = pl.ds(pl.multiple_of(i * t, t), t)
        o_ref[0, rows, :] = (o_t.T * sz_ref[0, rows, :].astype(F32)).astype(BF16)
        return carry

    lax.fori_loop(0, n_rows, finish_row, 0, unroll=8)


def _fox(q_aug, k_aug, vbt, szb, *, t):
    b, _, s, _ = q_aug.shape
    npair = FOX_HEADS // 2
    n_rows = s // t
    tbl = _fox_schedule(n_rows)
    tri = np.triu(np.ones((t, t), bool))
    bias = jnp.asarray(np.where(tri, 0.0, -np.inf).astype(np.float32))
    pair = lambda bi, p: (bi, p, 0, 0)
    stat = pltpu.VMEM((2, SUBLANES, t), F32)
    return pl.pallas_call(
        functools.partial(_fox_kernel, t=t, n_rows=n_rows),
        grid=(b, npair),
        in_specs=[pl.BlockSpec(memory_space=pltpu.SMEM),
                  pl.BlockSpec((1, 2, s, LANES), pair),
                  pl.BlockSpec((1, 2, s, LANES), pair),
                  pl.BlockSpec((1, 2 * HEAD_DIM, s), lambda bi, p: (bi, p, 0)),
                  pl.BlockSpec((t, t), lambda bi, p: (0, 0), pipeline_mode=pl.Buffered(1)),
                  pl.BlockSpec((1, s, LANES), lambda bi, p: (bi, 0, p))],
        out_specs=pl.BlockSpec((1, s, LANES), lambda bi, p: (bi, 0, p)),
        out_shape=jax.ShapeDtypeStruct((b, s, FOX_WIDTH), BF16),
        scratch_shapes=[pltpu.VMEM((2, t, t), F32), pltpu.VMEM((2, t, t), F32),
                        stat, stat,
                        pltpu.VMEM((2, t, t), BF16), pltpu.VMEM((2, t, t), BF16),
                        stat, stat,
                        pltpu.VMEM((n_rows, 2, SUBLANES, t), F32),
                        pltpu.VMEM((n_rows, 2, _ACC_ROWS, t), F32)],
        compiler_params=pltpu.CompilerParams(
            dimension_semantics=("arbitrary", "arbitrary"),
            vmem_limit_bytes=_FOX_VMEM),
        name="fox_attn",
    )(jnp.asarray(tbl), q_aug, k_aug, vbt, bias, szb)


def _out_kernel(x_ref, ga_ref, gb_ref, sga_ref, sgb_ref, gate_ref,
                woa_ref, wob_ref, wout_ref, gf_ref, o_ref):
    ya = jnp.dot(ga_ref[0], woa_ref[...], preferred_element_type=F32)
    yb = jnp.dot(gb_ref[0], wob_ref[...], preferred_element_type=F32)
    merged = sga_ref[0].astype(F32) * ya + sgb_ref[0].astype(F32) * yb
    d = jnp.dot(merged.astype(BF16), wout_ref[...], preferred_element_type=F32)
    xo = x_ref[0] + gate_ref[0] * d
    ms = jnp.mean(xo * xo, axis=-1, keepdims=True)
    o_ref[0] = xo * lax.rsqrt(ms + NORM_EPS) * gf_ref[...]


def _out(x, gated_a, gated_b, sga, sgb, gate, woa, wob, wout, gf, *, tm):
    b, s, _ = x.shape
    row = lambda bi, i: (bi, i, 0)
    per_b = lambda bi, i: (bi, 0, 0)
    const2 = lambda bi, i: (0, 0)
    once = pl.Buffered(1)
    return pl.pallas_call(
        _out_kernel,
        grid=(b, s // tm),
        in_specs=[pl.BlockSpec((1, tm, D_MODEL), row),
                  pl.BlockSpec((1, tm, SWA_WIDTH), row), pl.BlockSpec((1, tm, FOX_WIDTH), row),
                  pl.BlockSpec((1, tm, D_MODEL), row), pl.BlockSpec((1, tm, D_MODEL), row),
                  pl.BlockSpec((1, 1, D_MODEL), per_b),
                  pl.BlockSpec((SWA_WIDTH, D_MODEL), const2, pipeline_mode=once),
                  pl.BlockSpec((FOX_WIDTH, D_MODEL), const2, pipeline_mode=once),
                  pl.BlockSpec((D_MODEL, D_MODEL), const2, pipeline_mode=once),
                  pl.BlockSpec((1, D_MODEL), const2)],
        out_specs=pl.BlockSpec((1, tm, D_MODEL), row),
        out_shape=jax.ShapeDtypeStruct((b, s, D_MODEL), F32),
        compiler_params=pltpu.CompilerParams(
            dimension_semantics=("arbitrary", "arbitrary"),
            vmem_limit_bytes=_OUT_VMEM),
        name="out_proj",
    )(x, gated_a, gated_b, sga, sgb, gate, woa, wob, wout, gf)


def _pair_halves(w, nheads, dup):
    rows = w.shape[0]
    if dup:
        w5 = jnp.broadcast_to(w.reshape(rows, nheads, 1, 2, HALF), (rows, nheads, 2, 2, HALF))
    else:
        w5 = w.reshape(rows, nheads // 2, 2, 2, HALF)
    return w5.transpose(0, 1, 3, 2, 4).reshape(rows, -1)


def _layout_w_in(w):
    o = np.cumsum([0, SWA_WIDTH, SWA_KV_WIDTH, SWA_KV_WIDTH, SWA_WIDTH, FOX_WIDTH, FOX_WIDTH, FOX_WIDTH,
                   FOX_HEADS, FOX_WIDTH, D_MODEL, D_MODEL]).tolist()
    qa, ka, va, za, qb, kb, vb, fb, zb, ga, gb = [w[:, o[k]:o[k + 1]] for k in range(11)]
    cols = [_pair_halves(qa, SWA_Q_HEADS, False), _pair_halves(ka, SWA_KV_HEADS, True),
            za, qb, kb, jnp.tile(fb, (1, LANES // FOX_HEADS)), zb, ga, gb]
    w_all = jnp.concatenate(cols, axis=1).astype(BF16)
    assert w_all.shape[1] == _PROJ_COLS
    return w_all, jnp.concatenate([va, vb], axis=1).T.astype(BF16)


def kernel(x, c, positions, w_ada, b_ada, g_norm, w_in, b_f, sinks, w_o_swa, w_o_fox, w_out, g_final):
    b, s, _ = x.shape
    depth = w_in.shape[0]
    assert depth == 1, "the output stage fuses the final RMSNorm into the single layer"
    inv_freq = ROPE_THETA ** (-jnp.arange(0, HEAD_DIM, 2, dtype=F32) / HEAD_DIM)
    invf = jnp.broadcast_to(inv_freq[:, None], (HALF, _PROJ_TM))
    pos_f = positions.astype(F32)[:, None, :]
    e_mat = jnp.asarray(_aug_placement(), BF16)
    c_pad = jnp.zeros((SUBLANES, D_MODEL), F32).at[:b].set(c)
    for l in range(depth):
        ada = _ada(c_pad, w_ada[l], b_ada[l][None, :])[:b]
        shift, scale, gate = [ada[:, None, k * D_MODEL:(k + 1) * D_MODEL] for k in range(3)]
        w_all, w_vt = _layout_w_in(w_in[l])
        bf_rep = jnp.tile(b_f[l].astype(F32), LANES // FOX_HEADS)[None, :]
        qa, ka, vat, sza, q_aug, k_aug, vbt, szb, sga, sgb = _proj(
            x, pos_f, scale, shift, g_norm[l][None, :], invf, bf_rep, w_all, w_vt, e_mat, tm=_PROJ_TM)
        gated_a = _swa(sinks[l].astype(F32), qa, ka, vat, sza, tq=_SWA_TQ)
        gated_b = _fox(q_aug, k_aug, vbt, szb, t=_FOX_T)
        x = _out(x, gated_a, gated_b, sga, sgb, gate,
                 w_o_swa[l].astype(BF16), w_o_fox[l].astype(BF16), w_out[l].astype(BF16),
                 g_final[None, :], tm=_OUT_TM)
    return x
```
